```python
import jax, jax.numpy as jnp
from jax import lax
import numpy as np

D_MODEL = 2048
BATCH = 8
SEQ = 2048
DEPTH = 2
DEC_BATCH = 128
DEC_SEQ = 8
PAST_LEN = 2048
PAGE_SIZE = 128

GROUP_W = D_MODEL // 4
POOL_WINDOWS = (2, 4, 8, 16)
POOL_GROUPS = 4
POOL_GC = GROUP_W // POOL_GROUPS
POOL_BUF = 16 - 1
GMLP_GROUPS = 4
GMLP_GC = GROUP_W // GMLP_GROUPS
GMLP_CHUNK = 128
NSA_DH = 128
NSA_H = GROUP_W // NSA_DH
CMP_BLOCK = 32
SEL_BLOCK = 64
SEL_TOP = 16
WINDOW = 512
Q_BLOCK = 128
GLA_H = 4
GLA_DV = GROUP_W // GLA_H
GLA_DK = GLA_DV // 2
GLA_RANK = 16
GLA_TAU = 16.0
GLA_CHUNK = 64
MEM_LEN = 256
MEM_H = 4
MEM_DH = 128
N_EXPERTS = 32
TOP_K = 4
D_FF = D_MODEL
SWIGLU_LIMIT = 7.0
SWIGLU_ALPHA = 1.702
MOE_BLOCK = 128
DN_ALPHA = (2.0 * DEPTH) ** 0.25
DN_BETA = (8.0 * DEPTH) ** -0.25
LN_EPS = 1e-5
SPLITS = (GROUP_W, 2 * GROUP_W, GROUP_W, 6 * NSA_DH, 3 * NSA_H, GLA_H * GLA_DK, GLA_H * GLA_DK, GROUP_W, GLA_RANK, GROUP_W)
D_IN = sum(SPLITS)

kernel_name = "hybrid_pool_sgu_nsa_gla_moe_decode_step"


def layer_norm(x, g, b):
    xf = x.astype(jnp.float32)
    mu = jnp.mean(xf, -1, keepdims=True)
    xc = xf - mu
    var = jnp.mean(xc * xc, -1, keepdims=True)
    return (xc * lax.rsqrt(var + LN_EPS) * g + b).astype(x.dtype)


def masked_softmax(s, mask):
    s = jnp.where(mask, s.astype(jnp.float32), -jnp.inf)
    m = jnp.max(s, -1, keepdims=True)
    m = jnp.where(jnp.isfinite(m), m, 0.0)
    e = jnp.exp(s - m)
    den = jnp.sum(e, -1, keepdims=True)
    return e / jnp.where(den > 0, den, 1.0)


def alibi_slopes(n):
    return 2.0 ** (-8.0 * jnp.arange(1, n + 1, dtype=jnp.float32) / n)


def pool_mixer(xa_ext, t0, pool_w, pool_scale):
    B, L, C = xa_ext.shape
    T = L - POOL_BUF
    xf = xa_ext.astype(jnp.float32)
    csum = jnp.concatenate([jnp.zeros((B, 1, C), jnp.float32), jnp.cumsum(xf, axis=1)], axis=1)
    win = jnp.repeat(jnp.array(POOL_WINDOWS, jnp.int32), POOL_GC)
    end = POOL_BUF + 1 + jnp.arange(T)
    start = end[:, None] - win[None, :]
    s_start = jnp.take_along_axis(csum, jnp.broadcast_to(start[None], (B, T, C)), axis=1)
    count = jnp.minimum(win[None, :], t0 + 1 + jnp.arange(T)[:, None]).astype(jnp.float32)
    d = (csum[:, POOL_BUF + 1:] - s_start) / count - xf[:, POOL_BUF:]
    d = d.reshape(B, T, POOL_GROUPS, POOL_GC)
    y = jnp.einsum('btgc,gce->btge', d, pool_w.astype(jnp.float32)).reshape(B, T, C) * pool_scale
    return y.astype(xa_ext.dtype)


def chunk_mlp(uv, ln_g, ln_b, w_s, b_s):
    B, T, _ = uv.shape
    uv = jax.nn.gelu(uv)
    u, v = jnp.split(uv, 2, axis=-1)
    v = layer_norm(v, ln_g, ln_b)
    cl = min(GMLP_CHUNK, T)
    n = T // cl
    w = jnp.where(jnp.tril(jnp.ones((cl, cl), bool))[None], w_s[:, :cl, :cl], 0.0)
    vc = v.reshape(B, n, cl, GMLP_GROUPS, GMLP_GC)
    mixed = jnp.einsum('gts,bnsgc->bntgc', w, vc) + b_s[:, :cl].T[None, None, :, :, None]
    y = u * mixed.reshape(B, T, GROUP_W)
    return y.astype(uv.dtype), v


def nsa_mixer(q, q_start, kv_sel, kv_win, w_start, phi, gates, slopes):
    B, Tq, H, DH = q.shape
    L = kv_sel.shape[1]
    Lp = -(-L // SEL_BLOCK) * SEL_BLOCK
    kv_sel = jnp.pad(kv_sel, ((0, 0), (0, Lp - L), (0, 0), (0, 0)))
    n_cmp = Lp // CMP_BLOCK
    n_sel = Lp // SEL_BLOCK
    kv_cmp = jnp.einsum('bnlsd,slde->bnse', kv_sel[:, :, :2].reshape(B, n_cmp, CMP_BLOCK, 2, DH), phi)
    k_cmp, v_cmp = kv_cmp[:, :, 0], kv_cmp[:, :, 1]
    cmp_end = jnp.arange(n_cmp) * CMP_BLOCK + (CMP_BLOCK - 1)
    k_sel = kv_sel[:, :, 2].reshape(B, n_sel, SEL_BLOCK, DH)
    v_sel = kv_sel[:, :, 3].reshape(B, n_sel, SEL_BLOCK, DH)
    top = min(SEL_TOP, n_sel)
    kv_win = jnp.pad(kv_win, ((0, 0), (WINDOW, 0), (0, 0), (0, 0)))
    qb = min(Q_BLOCK, Tq)
    n_qb = Tq // qb
    blk = jnp.arange(n_sel)
    gather = jax.vmap(lambda arr, idx: arr[idx])

    def block(i):
        q0 = i * qb
        qi = lax.dynamic_slice_in_dim(q, q0, qb, axis=1)
        gi = lax.dynamic_slice_in_dim(gates, q0, qb, axis=1)
        pos = q_start + q0 + jnp.arange(qb)
        dist_c = (pos[:, None] - cmp_end[None, :]).astype(jnp.float32)
        s_c = jnp.einsum('bqhd,bnd->bqhn', qi, k_cmp) - slopes[None, :, None] * dist_c[:, None, :]
        p_c = masked_softmax(s_c, (cmp_end[None, :] <= pos[:, None])[None, :, None, :])
        o_c = jnp.einsum('bqhn,bnd->bqhd', p_c, v_cmp)
        imp = p_c.sum(2).reshape(B, qb, n_sel, SEL_BLOCK // CMP_BLOCK).sum(-1)
        cur = pos // SEL_BLOCK
        forced = (blk[None] == 0) | (blk[None] == cur[:, None]) | (blk[None] == cur[:, None] - 1)
        valid = blk[None] <= cur[:, None]
        score = jnp.where(valid, jnp.where(forced, jnp.inf, imp), -jnp.inf)
        top_s, top_i = lax.top_k(score, top)
        k_g = gather(k_sel, top_i)
        v_g = gather(v_sel, top_i)
        kpos = top_i[..., None] * SEL_BLOCK + jnp.arange(SEL_BLOCK)
        m_s = (top_s > -jnp.inf)[..., None] & (kpos <= pos[None, :, None, None])
        dist_s = (pos[None, :, None, None] - kpos).astype(jnp.float32)
        s_s = jnp.einsum('bqhd,bqkld->bqhkl', qi, k_g) - slopes[None, None, :, None, None] * dist_s[:, :, None]
        p_s = masked_softmax(s_s.reshape(B, qb, H, top * SEL_BLOCK), m_s.reshape(B, qb, 1, top * SEL_BLOCK))
        o_s = jnp.einsum('bqhn,bqnd->bqhd', p_s, v_g.reshape(B, qb, top * SEL_BLOCK, DH))
        kw = lax.dynamic_slice_in_dim(kv_win, q_start + q0 - w_start, WINDOW + qb, axis=1)
        wpos = q_start + q0 - WINDOW + jnp.arange(WINDOW + qb)
        dw = pos[:, None] - wpos[None, :]
        m_w = (dw >= 0) & (dw < WINDOW) & (wpos[None, :] >= 0)
        s_w = jnp.einsum('bqhd,bkd->bqhk', qi, kw[:, :, 0]) - slopes[None, :, None] * dw[:, None, :].astype(jnp.float32)
        p_w = masked_softmax(s_w, m_w[None, :, None, :])
        o_w = jnp.einsum('bqhk,bkd->bqhd', p_w, kw[:, :, 1])
        return gi[..., 0:1] * o_c + gi[..., 1:2] * o_s + gi[..., 2:3] * o_w

    out = lax.map(block, jnp.arange(n_qb))
    return out.transpose(1, 0, 2, 3, 4).reshape(B, Tq, H * DH).astype(q.dtype)


def gla_mixer(q, k, v, log_a, s0):
    B, T, H, DK = q.shape
    DV = v.shape[-1]
    c = min(GLA_CHUNK, T)
    n = -(-T // c)
    pad = n * c - T

    def prep(a):
        a = jnp.pad(a.astype(jnp.float32), ((0, 0), (0, pad), (0, 0), (0, 0)))
        return a.reshape(B, n, c, H, a.shape[-1]).transpose(1, 0, 3, 2, 4)

    qc, kc, vc = prep(q), prep(k), prep(v)
    bc = jnp.cumsum(prep(log_a), axis=3)
    causal = jnp.tril(jnp.ones((c, c), bool))[:, :, None]

    def step(S, xs):
        qi, ki, vi, bi = xs
        inter = jnp.einsum('bhtk,bhkv->bhtv', qi * jnp.exp(bi), S)
        diff = jnp.where(causal, bi[:, :, :, None, :] - bi[:, :, None, :, :], -jnp.inf)
        att = jnp.einsum('bhtk,bhsk,bhtsk->bhts', qi, ki, jnp.exp(diff))
        o = inter + jnp.einsum('bhts,bhsv->bhtv', att, vi)
        bl = bi[:, :, -1:, :]
        S = jnp.exp(bl[:, :, 0, :])[..., None] * S + jnp.einsum('bhsk,bhsv->bhkv', ki * jnp.exp(bl - bi), vi)
        return S, o

    S, o = lax.scan(step, s0.astype(jnp.float32), (qc, kc, vc, bc))
    o = o.transpose(1, 0, 3, 2, 4).reshape(B, n * c, H, DV)[:, :T]
    return o, S


def mem_attention(h, mem_kv, wq, wo):
    B, T, _ = h.shape
    q = (h @ wq).reshape(B, T, MEM_H, MEM_DH) * (MEM_DH ** -0.5)
    s = jnp.einsum('bthd,bmhd->bhtm', q, mem_kv[:, :, 0]).astype(jnp.float32)
    p = jax.nn.softmax(s, axis=-1)
    o = jnp.einsum('bhtm,bmhd->bthd', p, mem_kv[:, :, 1])
    return (o.reshape(B, T, MEM_H * MEM_DH).astype(h.dtype) @ wo).astype(h.dtype)


def moe_ffn(x, router_w, router_b, w1, b1, w2, b2):
    N, D = x.shape
    logits = (x @ router_w + router_b).astype(jnp.float32)
    top_v, top_e = lax.top_k(logits, TOP_K)
    gate = jax.nn.softmax(top_v, axis=-1)
    n_assign = N * TOP_K
    e_flat = top_e.reshape(-1)
    tok_flat = jnp.repeat(jnp.arange(N, dtype=jnp.int32), TOP_K)
    order = jnp.argsort(e_flat)
    e_sorted = e_flat[order]
    counts = jnp.bincount(e_flat, length=N_EXPERTS)
    padded = (counts + MOE_BLOCK - 1) // MOE_BLOCK * MOE_BLOCK
    start = jnp.cumsum(counts) - counts
    pad_end = jnp.cumsum(padded)
    pad_start = pad_end - padded
    dest = pad_start[e_sorted] + jnp.arange(n_assign) - start[e_sorted]
    n_blocks = -(-(n_assign + N_EXPERTS * (MOE_BLOCK - 1)) // MOE_BLOCK)
    cap = n_blocks * MOE_BLOCK
    slot_tok = jnp.full((cap,), N, jnp.int32).at[dest].set(tok_flat[order])
    slot_gate = jnp.zeros((cap,), jnp.float32).at[dest].set(gate.reshape(-1)[order])
    blk_expert = jnp.minimum(jnp.searchsorted(pad_end, jnp.arange(n_blocks) * MOE_BLOCK, side='right'), N_EXPERTS - 1)
    xb = jnp.concatenate([x, jnp.zeros((1, D), x.dtype)], 0)[slot_tok].reshape(n_blocks, MOE_BLOCK, D)

    def expert_block(args):
        xe, e = args
        h = (xe @ w1[e] + b1[e]).astype(jnp.float32)
        g, u = jnp.split(h, 2, axis=-1)
        g = jnp.minimum(g, SWIGLU_LIMIT)
        u = jnp.clip(u, -SWIGLU_LIMIT, SWIGLU_LIMIT)
        a = (u + 1.0) * g * jax.nn.sigmoid(SWIGLU_ALPHA * g)
        return a.astype(xe.dtype) @ w2[e] + b2[e]

    yb = lax.map(expert_block, (xb, blk_expert))
    y = jax.ops.segment_sum(yb.reshape(cap, D).astype(jnp.float32) * slot_gate[:, None], slot_tok, num_segments=N + 1)[:N]
    return y.astype(x.dtype)


def trunk_layer(x, pos0, pool_prefix, kv_sel_past, kv_win_past, w_start, s0, mem_kv, p):
    B, T, _ = x.shape
    dt = x.dtype
    proj = x @ p['w_in']
    cuts = [int(c) for c in np.cumsum(SPLITS)[:-1]]
    xa, uv, qc, kvc, gc, qd, kd, vd, lrd, god = jnp.split(proj, cuts, axis=-1)
    ya = pool_mixer(jnp.concatenate([pool_prefix.astype(dt), xa], axis=1), pos0, p['pool_w'], p['pool_scale'])
    yb, v_rows = chunk_mlp(uv, p['gmlp_ln_g'], p['gmlp_ln_b'], p['gmlp_ws'], p['gmlp_bs'])
    kvc = kvc.reshape(B, T, 6, NSA_DH)
    kv_sel_new, kv_win_new = kvc[:, :, :4], kvc[:, :, 4:]
    yc = nsa_mixer(qc.reshape(B, T, NSA_H, NSA_DH) * (NSA_DH ** -0.5), pos0,
                   jnp.concatenate([kv_sel_past.astype(dt), kv_sel_new], axis=1),
                   jnp.concatenate([kv_win_past.astype(dt), kv_win_new], axis=1), w_start,
                   p['nsa_phi'], jax.nn.sigmoid(gc).reshape(B, T, NSA_H, 3), alibi_slopes(NSA_H))
    log_a = jax.nn.log_sigmoid((lrd @ p['gla_wa'] + p['gla_ba']).astype(jnp.float32)) / GLA_TAU
    od, s_new = gla_mixer(qd.reshape(B, T, GLA_H, GLA_DK) * (GLA_DK ** -0.5), kd.reshape(B, T, GLA_H, GLA_DK),
                          vd.reshape(B, T, GLA_H, GLA_DV), log_a.reshape(B, T, GLA_H, GLA_DK), s0)
    od = od * lax.rsqrt(jnp.mean(od * od, -1, keepdims=True) + LN_EPS) * p['gla_norm_g']
    yd = (od.reshape(B, T, GROUP_W) * jax.nn.silu(god.astype(jnp.float32))).astype(dt)
    mix = jnp.concatenate([ya, yb.astype(dt), yc, yd], axis=-1) @ p['w_out']
    x = layer_norm(DN_ALPHA * x + mix, p['ln1_g'], p['ln1_b'])
    x = layer_norm(DN_ALPHA * x + mem_attention(x, mem_kv.astype(dt), p['mem_wq'], p['mem_wo']), p['ln2_g'], p['ln2_b'])
    y = moe_ffn(x.reshape(B * T, D_MODEL), p['router_w'], p['router_b'], p['moe_w1'], p['moe_b1'],
                p['moe_w2'], p['moe_b2']).reshape(B, T, D_MODEL)
    x = layer_norm(DN_ALPHA * x + y, p['ln3_g'], p['ln3_b'])
    return x, xa, v_rows, kv_sel_new, kv_win_new, s_new


def setup_inputs(seed: int = 0) -> dict:
    key = jax.random.key(seed)
    ks = iter(jax.random.split(key, 48))

    def nrm(shape, scale=1.0):
        return jax.random.normal(next(ks), shape, jnp.float32) * scale

    def gain(shape):
        return 1.0 + nrm(shape, 0.02)

    n_pages = PAST_LEN // PAGE_SIZE
    n_used = DEC_BATCH * n_pages
    n_pool = n_used + max(1, n_used // 4)
    win_buf = min(WINDOW, PAST_LEN)
    perm = jax.random.permutation(next(ks), n_pool)
    page_table = perm[:n_used].reshape(DEC_BATCH, n_pages).astype(jnp.int32)
    return {
        'x_prompt': nrm((BATCH, SEQ, D_MODEL)),
        'x_sample': nrm((DEC_BATCH, DEC_SEQ, D_MODEL)),
        'cache_pool': nrm((DEPTH, DEC_BATCH, POOL_BUF, GROUP_W)),
        'cache_nsa_kv': nrm((DEPTH, n_pool, PAGE_SIZE, 4, NSA_DH)),
        'cache_win_kv': nrm((DEPTH, DEC_BATCH, win_buf, 2, NSA_DH)),
        'state_gla': nrm((DEPTH, DEC_BATCH, GLA_H, GLA_DK, GLA_DV)),
        'cache_mem_kv': nrm((DEPTH, DEC_BATCH, MEM_LEN, 2, MEM_H, MEM_DH)),
        'page_table': page_table,
        'mem_prompt': nrm((BATCH, MEM_LEN, D_MODEL)),
        'ln_in_g': gain((D_MODEL,)),
        'ln_in_b': nrm((D_MODEL,), 0.02),
        'w_in': nrm((DEPTH, D_MODEL, D_IN), D_MODEL ** -0.5),
        'w_out': nrm((DEPTH, 4 * GROUP_W, D_MODEL), (4 * GROUP_W) ** -0.5 * DN_BETA),
        'pool_w': nrm((DEPTH, POOL_GROUPS, POOL_GC, POOL_GC), POOL_GC ** -0.5),
        'pool_scale': gain((DEPTH, GROUP_W)),
        'gmlp_ln_g': gain((DEPTH, GROUP_W)),
        'gmlp_ln_b': nrm((DEPTH, GROUP_W), 0.02),
        'gmlp_ws': nrm((DEPTH, GMLP_GROUPS, GMLP_CHUNK, GMLP_CHUNK), GMLP_CHUNK ** -0.5),
        'gmlp_bs': gain((DEPTH, GMLP_GROUPS, GMLP_CHUNK)),
        'nsa_phi': nrm((DEPTH, 2, CMP_BLOCK, NSA_DH, NSA_DH), (CMP_BLOCK * NSA_DH) ** -0.5),
        'gla_wa': nrm((DEPTH, GLA_RANK, GLA_H * GLA_DK), GLA_RANK ** -0.5),
        'gla_ba': nrm((DEPTH, GLA_H * GLA_DK), 0.02),
        'gla_norm_g': gain((DEPTH, GLA_DV)),
        'ln1_g': gain((DEPTH, D_MODEL)),
        'ln1_b': nrm((DEPTH, D_MODEL), 0.02),
        'mem_wq': nrm((DEPTH, D_MODEL, MEM_H * MEM_DH), D_MODEL ** -0.5),
        'mem_wkv': nrm((DEPTH, D_MODEL, 2 * MEM_H * MEM_DH), D_MODEL ** -0.5),
        'mem_wo': nrm((DEPTH, MEM_H * MEM_DH, D_MODEL), (MEM_H * MEM_DH) ** -0.5 * DN_BETA),
        'ln2_g': gain((DEPTH, D_MODEL)),
        'ln2_b': nrm((DEPTH, D_MODEL), 0.02),
        'router_w': nrm((DEPTH, D_MODEL, N_EXPERTS), D_MODEL ** -0.5),
        'router_b': nrm((DEPTH, N_EXPERTS), 0.01),
        'moe_w1': nrm((DEPTH, N_EXPERTS, D_MODEL, 2 * D_FF), D_MODEL ** -0.5),
        'moe_b1': nrm((DEPTH, N_EXPERTS, 2 * D_FF), 0.02),
        'moe_w2': nrm((DEPTH, N_EXPERTS, D_FF, D_MODEL), D_FF ** -0.5 * DN_BETA),
        'moe_b2': nrm((DEPTH, N_EXPERTS, D_MODEL), 0.02),
        'ln3_g': gain((DEPTH, D_MODEL)),
        'ln3_b': nrm((DEPTH, D_MODEL), 0.02),
    }


def reference(x_prompt, x_sample, cache_pool, cache_nsa_kv, cache_win_kv, state_gla, cache_mem_kv, page_table,
              mem_prompt, ln_in_g, ln_in_b, w_in, w_out, pool_w, pool_scale, gmlp_ln_g, gmlp_ln_b, gmlp_ws, gmlp_bs,
              nsa_phi, gla_wa, gla_ba, gla_norm_g, ln1_g, ln1_b, mem_wq, mem_wkv, mem_wo, ln2_g, ln2_b,
              router_w, router_b, moe_w1, moe_b1, moe_w2, moe_b2, ln3_g, ln3_b):
    B, T = x_prompt.shape[0], x_prompt.shape[1]
    Bd = x_sample.shape[0]
    past_len = page_table.shape[1] * cache_nsa_kv.shape[2]
    win_buf = cache_win_kv.shape[2]
    dt = x_prompt.dtype
    xp = layer_norm(x_prompt, ln_in_g, ln_in_b)
    xs = layer_norm(x_sample, ln_in_g, ln_in_b)
    pool_p, nsa_p, win_p, gla_p, mem_p = [], [], [], [], []
    pool_s, chunk_s, nsa_s, win_s, gla_s = [], [], [], [], []
    for l in range(DEPTH):
        p = {'w_in': w_in[l], 'w_out': w_out[l], 'pool_w': pool_w[l], 'pool_scale': pool_scale[l],
             'gmlp_ln_g': gmlp_ln_g[l], 'gmlp_ln_b': gmlp_ln_b[l], 'gmlp_ws': gmlp_ws[l], 'gmlp_bs': gmlp_bs[l],
             'nsa_phi': nsa_phi[l], 'gla_wa': gla_wa[l], 'gla_ba': gla_ba[l], 'gla_norm_g': gla_norm_g[l],
             'ln1_g': ln1_g[l], 'ln1_b': ln1_b[l], 'mem_wq': mem_wq[l], 'mem_wo': mem_wo[l],
             'ln2_g': ln2_g[l], 'ln2_b': ln2_b[l], 'router_w': router_w[l], 'router_b': router_b[l],
             'moe_w1': moe_w1[l], 'moe_b1': moe_b1[l], 'moe_w2': moe_w2[l], 'moe_b2': moe_b2[l],
             'ln3_g': ln3_g[l], 'ln3_b': ln3_b[l]}
        mem_kv_p = (mem_prompt @ mem_wkv[l]).reshape(B, mem_prompt.shape[1], 2, MEM_H, MEM_DH)
        xp, xa_p, _, kvs_p, kvw_p, st_p = trunk_layer(
            xp, 0, jnp.zeros((B, POOL_BUF, GROUP_W), dt), jnp.zeros((B, 0, 4, NSA_DH), dt),
            jnp.zeros((B, 0, 2, NSA_DH), dt), 0, jnp.zeros((B, GLA_H, GLA_DK, GLA_DV), jnp.float32), mem_kv_p, p)
        pool_p.append(xa_p[:, T - POOL_BUF:])
        nsa_p.append(kvs_p)
        win_p.append(kvw_p[:, T - min(WINDOW, T):])
        gla_p.append(st_p)
        mem_p.append(mem_kv_p)
        past_kv = cache_nsa_kv[l][page_table].reshape(Bd, past_len, 4, NSA_DH)
        xs, xa_s, v_s, kvs_s, kvw_s, st_s = trunk_layer(
            xs, past_len, cache_pool[l], past_kv, cache_win_kv[l], past_len - win_buf, state_gla[l],
            cache_mem_kv[l], p)
        pool_s.append(xa_s)
        chunk_s.append(v_s)
        nsa_s.append(kvs_s)
        win_s.append(kvw_s)
        gla_s.append(st_s)
    return (xp, xs, jnp.stack(pool_p), jnp.stack(nsa_p), jnp.stack(win_p), jnp.stack(gla_p), jnp.stack(mem_p),
            jnp.stack(pool_s), jnp.stack(chunk_s), jnp.stack(nsa_s), jnp.stack(win_s), jnp.stack(gla_s))
```

```python
import functools

import jax
import jax.numpy as jnp
import numpy as np
from jax import lax
from jax.experimental import pallas as pl
from jax.experimental.pallas import tpu as pltpu

BF = jnp.bfloat16
F32 = jnp.float32

D_MODEL = 2048
DEPTH = 2
GROUP_W = 512
LANES = 128
POOL_WINDOWS = (2, 4, 8, 16)
POOL_BUF = 15
POOL_PRE = 16
GMLP_CHUNK = 128
NSA_DH = 128
NSA_H = 4
CMP_BLOCK = 32
SEL_BLOCK = 64
SEL_TOP = 16
WINDOW = 512
Q_BLOCK = 128
GLA_H = 4
GLA_DK = 64
GLA_DV = 128
GLA_RANK = 16
GLA_TAU = 16.0
GLA_CHUNK = 64
MEM_LEN = 256
MEM_H = 4
MEM_DH = 128
N_EXPERTS = 32
TOP_K = 4
D_FF = 2048
SWIGLU_LIMIT = 7.0
SWIGLU_ALPHA = 1.702
DN_ALPHA = (2.0 * DEPTH) ** 0.25
LN_EPS = 1e-5
NEG_INF = float("-inf")

CB_XA, CB_U, CB_V, CB_QC = 0, 4, 8, 12
CB_CMPK, CB_CMPV, CB_SELK, CB_SELV, CB_WINK, CB_WINV = 16, 17, 18, 19, 20, 21
CB_QD, CB_KD, CB_VD, CB_GOD, CB_MISC = 22, 26, 30, 34, 38
N_CB = 40
D_PROJ = N_CB * LANES
MISC_GATE0 = 0
MISC_LR0 = 12

MOE_TM = 512
MOE_TN = 512
VMEM_LIMIT = 56 * 1024 * 1024


def _cparams(sem):
    return pltpu.CompilerParams(dimension_semantics=sem, vmem_limit_bytes=VMEM_LIMIT)


def _layer_norm(x, g, b):
    mu = jnp.mean(x, -1, keepdims=True)
    xc = x - mu
    var = jnp.mean(xc * xc, -1, keepdims=True)
    return xc * lax.rsqrt(var + LN_EPS) * g + b


def _dot(a, b):
    return jnp.dot(a, b, preferred_element_type=F32)


def _dot_nt(a, b):
    return lax.dot_general(a, b, (((1,), (1,)), ((), ())), preferred_element_type=F32)


def _dot_tn(a, b):
    return lax.dot_general(a, b, (((0,), (0,)), ((), ())), preferred_element_type=F32)


def _sigmoid(x):
    return 1.0 / (1.0 + jnp.exp(-x))


def _ln_kernel(x_ref, g_ref, b_ref, o_ref):
    o_ref[...] = _layer_norm(x_ref[...], g_ref[...], b_ref[...])


def ln_rows(x, g, b, tm):
    m, d = x.shape
    return pl.pallas_call(
        _ln_kernel,
        grid=(m // tm,),
        in_specs=[pl.BlockSpec((tm, d), lambda i: (i, 0)),
                  pl.BlockSpec((1, d), lambda i: (0, 0)),
                  pl.BlockSpec((1, d), lambda i: (0, 0))],
        out_specs=pl.BlockSpec((tm, d), lambda i: (i, 0)),
        out_shape=jax.ShapeDtypeStruct((m, d), F32),
        compiler_params=_cparams(("parallel",)),
        name="ln_rows",
    )(x, g.reshape(1, d), b.reshape(1, d))


def _mm_kernel(x_ref, w_ref, o_ref, xb_ref):
    @pl.when(pl.program_id(1) == 0)
    def _():
        xb_ref[...] = x_ref[...].astype(BF)

    o_ref[...] = _dot(xb_ref[...], w_ref[...]).astype(o_ref.dtype)


def matmul(x, w, tm, tn, out_dtype=F32):
    m, k = x.shape
    n = w.shape[1]
    assert m % tm == 0 and n % tn == 0, (m, tm, n, tn)
    return pl.pallas_call(
        _mm_kernel,
        grid=(m // tm, n // tn),
        in_specs=[pl.BlockSpec((tm, k), lambda i, j: (i, 0)),
                  pl.BlockSpec((k, tn), lambda i, j: (0, j))],
        out_specs=pl.BlockSpec((tm, tn), lambda i, j: (i, j)),
        out_shape=jax.ShapeDtypeStruct((m, n), out_dtype),
        scratch_shapes=[pltpu.VMEM((tm, k), BF)],
        compiler_params=_cparams(("parallel", "arbitrary")),
        name="matmul",
    )(x, w)


def _mm_res_ln_kernel(*refs, n_in):
    xs = refs[:n_in]
    w_ref, r_ref, g_ref, b_ref, o_ref = refs[n_in:]
    acc = None
    k0 = 0
    for x_ref in xs:
        kk = x_ref.shape[1]
        part = _dot(x_ref[...].astype(BF), w_ref[k0:k0 + kk, :])
        acc = part if acc is None else acc + part
        k0 += kk
    o_ref[...] = _layer_norm(DN_ALPHA * r_ref[...] + acc, g_ref[...], b_ref[...])


def matmul_res_ln(xs, w, resid, g, b, tm):
    m, d = resid.shape
    in_specs = [pl.BlockSpec((tm, x.shape[1]), lambda i: (i, 0)) for x in xs]
    in_specs += [pl.BlockSpec(w.shape, lambda i: (0, 0)),
                 pl.BlockSpec((tm, d), lambda i: (i, 0)),
                 pl.BlockSpec((1, d), lambda i: (0, 0)),
                 pl.BlockSpec((1, d), lambda i: (0, 0))]
    return pl.pallas_call(
        functools.partial(_mm_res_ln_kernel, n_in=len(xs)),
        grid=(m // tm,),
        in_specs=in_specs,
        out_specs=pl.BlockSpec((tm, d), lambda i: (i, 0)),
        out_shape=jax.ShapeDtypeStruct((m, d), F32),
        compiler_params=_cparams(("parallel",)),
        name="matmul_res_ln",
    )(*xs, w, resid, g.reshape(1, d), b.reshape(1, d))


def _res_ln_kernel(y_ref, r_ref, g_ref, b_ref, o_ref):
    o_ref[...] = _layer_norm(DN_ALPHA * r_ref[...] + y_ref[...], g_ref[...], b_ref[...])


def res_ln(y, resid, g, b, tm):
    m, d = resid.shape
    return pl.pallas_call(
        _res_ln_kernel,
        grid=(m // tm,),
        in_specs=[pl.BlockSpec((tm, d), lambda i: (i, 0)),
                  pl.BlockSpec((tm, d), lambda i: (i, 0)),
                  pl.BlockSpec((1, d), lambda i: (0, 0)),
                  pl.BlockSpec((1, d), lambda i: (0, 0))],
        out_specs=pl.BlockSpec((tm, d), lambda i: (i, 0)),
        out_shape=jax.ShapeDtypeStruct((m, d), F32),
        compiler_params=_cparams(("parallel",)),
        name="res_ln",
    )(y, resid, g.reshape(1, d), b.reshape(1, d))


def _pool_kernel(x_ref, pre_ref, w_ref, sc_ref, o_ref, ext_ref, *, bb, tt, n_tiles, t0):
    j = pl.program_id(1)

    @pl.when(j == 0)
    def _():
        ext_ref[:, 0:POOL_PRE, :] = pre_ref[...]

    x = x_ref[...].reshape(bb, tt, GROUP_W)
    ext_ref[:, POOL_PRE:, :] = x
    t_idx = j * tt + lax.broadcasted_iota(jnp.int32, (1, tt, LANES), 1)
    for g, win in enumerate(POOL_WINDOWS):
        cs = slice(g * LANES, (g + 1) * LANES)
        xg = ext_ref[:, POOL_PRE:POOL_PRE + tt, cs]
        s = xg
        for k in range(1, win):
            s = s + ext_ref[:, POOL_PRE - k:POOL_PRE - k + tt, cs]
        cnt = jnp.minimum(win, t0 + 1 + t_idx).astype(F32)
        d = s / cnt - xg
        y = _dot(d.reshape(bb * tt, LANES).astype(BF), w_ref[g])
        o_ref[:, cs] = (y * sc_ref[:, cs]).astype(o_ref.dtype)
    if n_tiles > 1:
        ext_ref[:, 0:POOL_PRE, :] = ext_ref[:, tt:tt + POOL_PRE, :]


def pool_mixer(proj, row0, n_b, t_len, prefix, pool_w, pool_scale, t0, bb, tt):
    n_tiles = t_len // tt
    rb = bb * tt
    base = row0 // rb
    return pl.pallas_call(
        functools.partial(_pool_kernel, bb=bb, tt=tt, n_tiles=n_tiles, t0=t0),
        grid=(n_b // bb, n_tiles),
        in_specs=[pl.BlockSpec((rb, GROUP_W), lambda i, j: (base + i * n_tiles + j, CB_XA // 4)),
                  pl.BlockSpec((bb, POOL_PRE, GROUP_W), lambda i, j: (i, 0, 0)),
                  pl.BlockSpec((4, LANES, LANES), lambda i, j: (0, 0, 0)),
                  pl.BlockSpec((1, GROUP_W), lambda i, j: (0, 0))],
        out_specs=pl.BlockSpec((rb, GROUP_W), lambda i, j: (i * n_tiles + j, 0)),
        out_shape=jax.ShapeDtypeStruct((n_b * t_len, GROUP_W), BF),
        scratch_shapes=[pltpu.VMEM((bb, POOL_PRE + tt, GROUP_W), F32)],
        compiler_params=_cparams(("parallel", "arbitrary")),
        name="pool_mixer",
    )(proj, prefix, pool_w.astype(BF), pool_scale.reshape(1, GROUP_W))


def _gelu_tanh(x):
    return 0.5 * x * (1.0 + jnp.tanh(np.sqrt(2.0 / np.pi) * (x + 0.044715 * (x * x * x))))


def _gmlp_kernel(u_ref, v_ref, g_ref, b_ref, w_ref, bias_ref, y_ref, vo_ref, *, n_ch):
    for c in range(n_ch):
        rows = slice(c * GMLP_CHUNK, (c + 1) * GMLP_CHUNK)
        u = _gelu_tanh(u_ref[rows, :])
        v = _layer_norm(_gelu_tanh(v_ref[rows, :]), g_ref[...], b_ref[...])
        vo_ref[rows, :] = v
        vb = v.astype(BF)
        for g in range(4):
            cs = slice(g * LANES, (g + 1) * LANES)
            mixed = _dot(w_ref[g], vb[:, cs]) + bias_ref[:, cs]
            y_ref[rows, cs] = (u[:, cs] * mixed).astype(y_ref.dtype)


def gmlp_mixer(proj, row0, n_rows, ln_g, ln_b, w_mix, bias_full, n_ch):
    rb = n_ch * GMLP_CHUNK
    base = row0 // rb
    return pl.pallas_call(
        functools.partial(_gmlp_kernel, n_ch=n_ch),
        grid=(n_rows // rb,),
        in_specs=[pl.BlockSpec((rb, GROUP_W), lambda i: (base + i, CB_U // 4)),
                  pl.BlockSpec((rb, GROUP_W), lambda i: (base + i, CB_V // 4)),
                  pl.BlockSpec((1, GROUP_W), lambda i: (0, 0)),
                  pl.BlockSpec((1, GROUP_W), lambda i: (0, 0)),
                  pl.BlockSpec((4, GMLP_CHUNK, GMLP_CHUNK), lambda i: (0, 0, 0)),
                  pl.BlockSpec((GMLP_CHUNK, GROUP_W), lambda i: (0, 0))],
        out_specs=[pl.BlockSpec((rb, GROUP_W), lambda i: (i, 0)),
                   pl.BlockSpec((rb, GROUP_W), lambda i: (i, 0))],
        out_shape=[jax.ShapeDtypeStruct((n_rows, GROUP_W), BF),
                   jax.ShapeDtypeStruct((n_rows, GROUP_W), F32)],
        compiler_params=_cparams(("parallel",)),
        name="gmlp_mixer",
    )(proj, proj, ln_g.reshape(1, GROUP_W), ln_b.reshape(1, GROUP_W), w_mix, bias_full)


def _cmp_kernel(x_ref, phi_ref, o_ref):
    acc = None
    for l in range(CMP_BLOCK):
        part = _dot(x_ref[:, l, :].astype(BF), phi_ref[l])
        acc = part if acc is None else acc + part
    o_ref[...] = acc


def cmp_project(x3, col_block, blk0, n_blk, phi_s, tb):
    assert n_blk % tb == 0 and blk0 % tb == 0, (n_blk, blk0, tb)
    base = blk0 // tb
    return pl.pallas_call(
        _cmp_kernel,
        grid=(n_blk // tb,),
        in_specs=[pl.BlockSpec((tb, CMP_BLOCK, LANES), lambda i: (base + i, 0, col_block)),
                  pl.BlockSpec((CMP_BLOCK, NSA_DH, NSA_DH), lambda i: (0, 0, 0))],
        out_specs=pl.BlockSpec((tb, NSA_DH), lambda i: (i, 0)),
        out_shape=jax.ShapeDtypeStruct((n_blk, NSA_DH), F32),
        compiler_params=_cparams(("parallel",)),
        name="cmp_project",
    )(x3, phi_s)


def _alibi_slope(h):
    return 2.0 ** (-8.0 * (h + 1) / NSA_H)


def _head_rows_const(n_q, fn):
    r = lax.broadcasted_iota(jnp.int32, (NSA_H * n_q, 1), 0)
    out = jnp.full((NSA_H * n_q, 1), fn(NSA_H - 1), F32)
    for h in range(NSA_H - 2, -1, -1):
        out = jnp.where(r < (h + 1) * n_q, fn(h), out)
    return out


def _stack_heads(q):
    return jnp.concatenate([q[:, h * NSA_DH:(h + 1) * NSA_DH] for h in range(NSA_H)], axis=0)


def _select_blocks(score, n_blk):
    blk = lax.broadcasted_iota(jnp.int32, score.shape, 0)
    cnt = jnp.zeros(score.shape, F32)
    for i in range(n_blk):
        row = score[i:i + 1, :]
        beats = (row > score) | ((row == score) & (blk > i))
        cnt = cnt + jnp.where(beats, 1.0, 0.0)
    return jnp.where((cnt < float(SEL_TOP)) & (score > NEG_INF), 1.0, 0.0)


def _softmax_segments(segs):
    m = None
    for s, _ in segs:
        mi = jnp.max(s, axis=1, keepdims=True)
        m = mi if m is None else jnp.maximum(m, mi)
    m = jnp.where(m > NEG_INF, m, 0.0)
    den = None
    acc = None
    for s, v in segs:
        e = jnp.exp(s - m)
        d = jnp.sum(e, axis=1, keepdims=True)
        a = _dot(e.astype(BF), v)
        den = d if den is None else den + d
        acc = a if acc is None else acc + a
    return acc / jnp.where(den > 0, den, 1.0)


def _nsa_prompt_kernel(q_ref, misc_ref, kc_ref, vc_ref, ks_ref, vs_ref, kw_ref, vw_ref, e_ref, o_ref,
                       imp_ref, mask_ref, m_ref, l_ref, acc_ref, *, t_len):
    n_q = Q_BLOCK
    n_cmp = t_len // CMP_BLOCK
    n_sel = t_len // SEL_BLOCK
    kt = 512
    q0 = pl.program_id(1) * n_q
    q4 = _stack_heads(q_ref[...] * (NSA_DH ** -0.5)).astype(BF)
    slope_r = _head_rows_const(n_q, _alibi_slope)
    qpos_r = q0 + lax.rem(lax.broadcasted_iota(jnp.int32, (NSA_H * n_q, 1), 0), n_q)

    lane = lax.broadcasted_iota(jnp.int32, (1, NSA_H * n_q), 1)
    qpos_l = q0 + lax.rem(lane, n_q)
    slope_l = jnp.full((1, NSA_H * n_q), _alibi_slope(NSA_H - 1), F32)
    for h in range(NSA_H - 2, -1, -1):
        slope_l = jnp.where(lane < (h + 1) * n_q, _alibi_slope(h), slope_l)
    cmp_end = lax.broadcasted_iota(jnp.int32, (n_cmp, 1), 0) * CMP_BLOCK + (CMP_BLOCK - 1)
    s_c = _dot_nt(kc_ref[...].astype(BF), q4) - slope_l * (qpos_l - cmp_end).astype(F32)
    s_c = jnp.where(cmp_end <= qpos_l, s_c, NEG_INF)
    m_c = jnp.max(s_c, axis=0, keepdims=True)
    m_c = jnp.where(m_c > NEG_INF, m_c, 0.0)
    e_c = jnp.exp(s_c - m_c)
    den_c = jnp.sum(e_c, axis=0, keepdims=True)
    p_c = e_c / jnp.where(den_c > 0, den_c, 1.0)
    o_c = _dot_tn(p_c.astype(BF), vc_ref[...].astype(BF))

    imp = p_c[:, 0:n_q]
    for h in range(1, NSA_H):
        imp = imp + p_c[:, h * n_q:(h + 1) * n_q]
    imp_ref[...] = imp
    imp = imp_ref[pl.ds(0, n_sel, stride=2), :] + imp_ref[pl.ds(1, n_sel, stride=2), :]
    blk = lax.broadcasted_iota(jnp.int32, (n_sel, 1), 0)
    cur = (q0 + lax.broadcasted_iota(jnp.int32, (1, n_q), 1)) // SEL_BLOCK
    forced = (blk == 0) | (blk == cur) | (blk == cur - 1)
    score = jnp.where(blk <= cur, jnp.where(forced, jnp.inf, imp), NEG_INF)
    sel = _select_blocks(score, n_sel)
    mask_ref[...] = _dot_tn(sel.astype(BF), e_ref[...])

    m_ref[...] = jnp.full(m_ref.shape, NEG_INF, F32)
    l_ref[...] = jnp.zeros(l_ref.shape, F32)
    acc_ref[...] = jnp.zeros(acc_ref.shape, F32)

    def sel_tile(t, carry):
        k0 = pl.multiple_of(t * kt, kt)
        s = _dot_nt(q4, ks_ref[pl.ds(k0, kt), :].astype(BF))
        d = qpos_r - (k0 + lax.broadcasted_iota(jnp.int32, (1, kt), 1))
        mk = mask_ref[:, pl.ds(k0, kt)]
        mk = jnp.concatenate([mk] * NSA_H, axis=0)
        s = jnp.where((d >= 0) & (mk > 0.5), s - slope_r * d.astype(F32), NEG_INF)
        m_old = m_ref[...]
        m_new = jnp.maximum(m_old, jnp.max(s, axis=1, keepdims=True))
        m_use = jnp.where(m_new > NEG_INF, m_new, 0.0)
        alpha = jnp.exp(m_old - m_use)
        p = jnp.exp(s - m_use)
        l_ref[...] = alpha * l_ref[...] + jnp.sum(p, axis=1, keepdims=True)
        acc_ref[...] = alpha * acc_ref[...] + _dot(p.astype(BF), vs_ref[pl.ds(k0, kt), :].astype(BF))
        m_ref[...] = m_new
        return carry

    lax.fori_loop(0, (q0 + n_q + kt - 1) // kt, sel_tile, 0)
    l_s = l_ref[...]
    o_s = acc_ref[...] / jnp.where(l_s > 0, l_s, 1.0)

    nw = WINDOW + n_q
    w0 = pl.multiple_of(jnp.maximum(q0 - WINDOW, 0), n_q)
    s_w = _dot_nt(q4, kw_ref[pl.ds(w0, nw), :].astype(BF))
    d_w = qpos_r - (w0 + lax.broadcasted_iota(jnp.int32, (1, nw), 1))
    s_w = jnp.where((d_w >= 0) & (d_w < WINDOW), s_w - slope_r * d_w.astype(F32), NEG_INF)
    o_w = _softmax_segments([(s_w, vw_ref[pl.ds(w0, nw), :].astype(BF))])

    gates = _sigmoid(misc_ref[...])
    for h in range(NSA_H):
        rows = slice(h * n_q, (h + 1) * n_q)
        c = MISC_GATE0 + 3 * h
        out = (gates[:, c:c + 1] * o_c[rows, :] + gates[:, c + 1:c + 2] * o_s[rows, :]
               + gates[:, c + 2:c + 3] * o_w[rows, :])
        o_ref[:, h * NSA_DH:(h + 1) * NSA_DH] = out.astype(o_ref.dtype)


def nsa_prompt(proj, n_b, t_len, k_cmp, v_cmp, expand):
    n_qb = t_len // Q_BLOCK
    n_cmp = t_len // CMP_BLOCK
    n_sel = t_len // SEL_BLOCK
    seq = lambda cb: pl.BlockSpec((t_len, LANES), lambda b, i: (b, cb))
    return pl.pallas_call(
        functools.partial(_nsa_prompt_kernel, t_len=t_len),
        grid=(n_b, n_qb),
        in_specs=[pl.BlockSpec((Q_BLOCK, GROUP_W), lambda b, i: (b * n_qb + i, CB_QC // 4)),
                  pl.BlockSpec((Q_BLOCK, LANES), lambda b, i: (b * n_qb + i, CB_MISC)),
                  pl.BlockSpec((n_cmp, NSA_DH), lambda b, i: (b, 0)),
                  pl.BlockSpec((n_cmp, NSA_DH), lambda b, i: (b, 0)),
                  seq(CB_SELK), seq(CB_SELV), seq(CB_WINK), seq(CB_WINV),
                  pl.BlockSpec((n_sel, t_len), lambda b, i: (0, 0))],
        out_specs=pl.BlockSpec((Q_BLOCK, GROUP_W), lambda b, i: (b * n_qb + i, 0)),
        out_shape=jax.ShapeDtypeStruct((n_b * t_len, GROUP_W), BF),
        scratch_shapes=[pltpu.VMEM((n_cmp, Q_BLOCK), F32),
                        pltpu.VMEM((Q_BLOCK, t_len), F32),
                        pltpu.VMEM((NSA_H * Q_BLOCK, 1), F32),
                        pltpu.VMEM((NSA_H * Q_BLOCK, 1), F32),
                        pltpu.VMEM((NSA_H * Q_BLOCK, NSA_DH), F32)],
        compiler_params=_cparams(("parallel", "arbitrary")),
        name="nsa_prompt",
    )(proj, proj, k_cmp, v_cmp, proj, proj, proj, proj, expand)


def _page_gather_kernel(pt_ref, *refs, n_pp):
    ins, o_ref = refs[:n_pp], refs[n_pp]
    for p in range(n_pp):
        o_ref[p] = ins[p][0]


def gather_pages(cache, page_table, n_pp):
    n_b, n_pages = page_table.shape
    _, page, c = cache.shape
    steps = n_pages // n_pp

    def in_spec(p):
        return pl.BlockSpec((1, page, c), lambda b, s, pt: (pt[b, s * n_pp + p], 0, 0))

    return pl.pallas_call(
        functools.partial(_page_gather_kernel, n_pp=n_pp),
        grid_spec=pltpu.PrefetchScalarGridSpec(
            num_scalar_prefetch=1,
            grid=(n_b, steps),
            in_specs=[in_spec(p) for p in range(n_pp)],
            out_specs=pl.BlockSpec((n_pp, page, c), lambda b, s, pt: (b * steps + s, 0, 0))),
        out_shape=jax.ShapeDtypeStruct((n_b * n_pages, page, c), cache.dtype),
        compiler_params=_cparams(("parallel", "arbitrary")),
        name="gather_pages",
    )(page_table, *([cache] * n_pp))


def _nsa_sample_kernel(q_ref, misc_ref, kc_ref, vc_ref, past_ref, ksn_ref, vsn_ref, winc_ref, kwn_ref, vwn_ref,
                       pair_ref, e_ref, o_ref, *, past_len, t_q):
    n_cmp = past_len // CMP_BLOCK
    n_sel = past_len // SEL_BLOCK + 1
    n_r = NSA_H * t_q
    q4 = _stack_heads(q_ref[...] * (NSA_DH ** -0.5)).astype(BF)
    slope_r = _head_rows_const(t_q, _alibi_slope)
    qpos_r = past_len + lax.rem(lax.broadcasted_iota(jnp.int32, (n_r, 1), 0), t_q)

    cmp_end = lax.broadcasted_iota(jnp.int32, (1, n_cmp), 1) * CMP_BLOCK + (CMP_BLOCK - 1)
    s_c = _dot_nt(q4, kc_ref[...].astype(BF)) - slope_r * (qpos_r - cmp_end).astype(F32)
    s_c = jnp.where(cmp_end <= qpos_r, s_c, NEG_INF)
    m_c = jnp.max(s_c, axis=1, keepdims=True)
    m_c = jnp.where(m_c > NEG_INF, m_c, 0.0)
    e_c = jnp.exp(s_c - m_c)
    den_c = jnp.sum(e_c, axis=1, keepdims=True)
    p_c = e_c / jnp.where(den_c > 0, den_c, 1.0)
    o_c = _dot(p_c.astype(BF), vc_ref[...].astype(BF))

    imp = p_c[0:t_q, :]
    for h in range(1, NSA_H):
        imp = imp + p_c[h * t_q:(h + 1) * t_q, :]
    hi = imp.astype(BF)
    lo = (imp - hi.astype(F32)).astype(BF)
    imp = _dot(hi, pair_ref[...]) + _dot(lo, pair_ref[...])
    blk = lax.broadcasted_iota(jnp.int32, (1, LANES), 1)
    cur = (past_len + lax.broadcasted_iota(jnp.int32, (t_q, 1), 0)) // SEL_BLOCK
    forced = (blk == 0) | (blk == cur) | (blk == cur - 1)
    score = jnp.where(blk <= cur, jnp.where(forced, jnp.inf, imp), NEG_INF)
    cnt = jnp.zeros(score.shape, F32)
    for i in range(n_sel):
        col = score[:, i:i + 1]
        beats = (col > score) | ((col == score) & (blk > i))
        cnt = cnt + jnp.where(beats, 1.0, 0.0)
    sel = jnp.where((cnt < float(SEL_TOP)) & (score > NEG_INF), 1.0, 0.0)
    mask = _dot(sel.astype(BF), e_ref[...])
    mask = jnp.concatenate([mask] * NSA_H, axis=0)

    past = past_ref[...]
    ks_p = past[:, 2 * NSA_DH:3 * NSA_DH].astype(BF)
    vs_p = past[:, 3 * NSA_DH:4 * NSA_DH].astype(BF)
    d_p = qpos_r - lax.broadcasted_iota(jnp.int32, (1, past_len), 1)
    s_p = jnp.where(mask[:, 0:past_len] > 0.5, _dot_nt(q4, ks_p) - slope_r * d_p.astype(F32), NEG_INF)
    d_n = qpos_r - (past_len + lax.broadcasted_iota(jnp.int32, (1, t_q), 1))
    s_n = _dot_nt(q4, ksn_ref[...].astype(BF)) - slope_r * d_n.astype(F32)
    s_n = jnp.where((d_n >= 0) & (mask[:, past_len:past_len + t_q] > 0.5), s_n, NEG_INF)
    o_s = _softmax_segments([(s_p, vs_p), (s_n, vsn_ref[...].astype(BF))])

    winc = winc_ref[...]
    n_wc = winc.shape[0]
    d_c = qpos_r - (past_len - n_wc + lax.broadcasted_iota(jnp.int32, (1, n_wc), 1))
    s_wc = _dot_nt(q4, winc[:, 0:NSA_DH].astype(BF)) - slope_r * d_c.astype(F32)
    s_wc = jnp.where(d_c < WINDOW, s_wc, NEG_INF)
    s_wn = _dot_nt(q4, kwn_ref[...].astype(BF)) - slope_r * d_n.astype(F32)
    s_wn = jnp.where(d_n >= 0, s_wn, NEG_INF)
    o_w = _softmax_segments([(s_wc, winc[:, NSA_DH:2 * NSA_DH].astype(BF)), (s_wn, vwn_ref[...].astype(BF))])

    gates = _sigmoid(misc_ref[...])
    for h in range(NSA_H):
        rows = slice(h * t_q, (h + 1) * t_q)
        c = MISC_GATE0 + 3 * h
        out = (gates[:, c:c + 1] * o_c[rows, :] + gates[:, c + 1:c + 2] * o_s[rows, :]
               + gates[:, c + 2:c + 3] * o_w[rows, :])
        o_ref[:, h * NSA_DH:(h + 1) * NSA_DH] = out.astype(o_ref.dtype)


def nsa_sample(proj, row0, n_b, t_q, k_cmp, v_cmp, past, win_cache, wc_b0, pair, expand):
    past_len = past.shape[1]
    n_cmp = past_len // CMP_BLOCK
    n_wc = win_cache.shape[1]
    base = row0 // t_q
    new = lambda cb: pl.BlockSpec((t_q, LANES), lambda b: (base + b, cb))
    return pl.pallas_call(
        functools.partial(_nsa_sample_kernel, past_len=past_len, t_q=t_q),
        grid=(n_b,),
        in_specs=[pl.BlockSpec((t_q, GROUP_W), lambda b: (base + b, CB_QC // 4)),
                  new(CB_MISC),
                  pl.BlockSpec((n_cmp, NSA_DH), lambda b: (b, 0)),
                  pl.BlockSpec((n_cmp, NSA_DH), lambda b: (b, 0)),
                  pl.BlockSpec((None, past_len, 4 * NSA_DH), lambda b: (b, 0, 0)),
                  new(CB_SELK), new(CB_SELV),
                  pl.BlockSpec((None, n_wc, 2 * NSA_DH), lambda b: (wc_b0 + b, 0, 0)),
                  new(CB_WINK), new(CB_WINV),
                  pl.BlockSpec(pair.shape, lambda b: (0, 0)),
                  pl.BlockSpec(expand.shape, lambda b: (0, 0))],
        out_specs=pl.BlockSpec((t_q, GROUP_W), lambda b: (b, 0)),
        out_shape=jax.ShapeDtypeStruct((n_b * t_q, GROUP_W), BF),
        compiler_params=_cparams(("parallel",)),
        name="nsa_sample",
    )(proj, proj, k_cmp, v_cmp, past, proj, proj, win_cache, proj, proj, pair, expand)


def _log_sigmoid(z):
    return jnp.minimum(z, 0.0) - jnp.log(1.0 + jnp.exp(-jnp.abs(z)))


def _cumsum_rows(x, tri):
    hi = x.astype(BF)
    lo = (x - hi.astype(F32)).astype(BF)
    return _dot(tri, hi) + _dot(tri, lo)


def _rows_to_col(row):
    n = row.shape[1]
    eye = lax.broadcasted_iota(jnp.int32, (n, n), 0) == lax.broadcasted_iota(jnp.int32, (n, n), 1)
    return jnp.sum(jnp.where(eye, row, 0.0), axis=1, keepdims=True)


def _gla_out(o, god, ng):
    o = o * lax.rsqrt(jnp.mean(o * o, -1, keepdims=True) + LN_EPS) * ng
    return o * (god * _sigmoid(god))


def _gla_prompt_kernel(q_ref, k_ref, v_ref, god_ref, misc_ref, wa_ref, ba_ref, ng_ref, o_ref, s_ref, st_ref,
                       *, n_chunks):
    c = GLA_CHUNK
    st_ref[...] = jnp.zeros(st_ref.shape, F32)
    r_i = lax.broadcasted_iota(jnp.int32, (c, c), 0)
    c_i = lax.broadcasted_iota(jnp.int32, (c, c), 1)
    causal = r_i >= c_i
    tri = jnp.where(causal, 1.0, 0.0).astype(BF)

    def chunk(n, carry):
        rows = pl.ds(pl.multiple_of(n * c, c), c)
        z = _dot(misc_ref[rows, :].astype(BF), wa_ref[...]) + ba_ref[...]
        bc = _cumsum_rows(_log_sigmoid(z) / GLA_TAU, tri)
        bl = bc[c - 1:c, :]
        q = q_ref[rows, :] * (GLA_DK ** -0.5)
        k = k_ref[rows, :]
        v = v_ref[rows, :].astype(BF)
        qe = (q * jnp.exp(bc)).astype(BF)
        att = jnp.where(causal, _dot_nt(qe, (k * jnp.exp(-bc)).astype(BF)), 0.0)
        st = st_ref[...]
        o = _dot(qe, st.astype(BF)) + _dot(att.astype(BF), v)
        st_ref[...] = _rows_to_col(jnp.exp(bl)) * st + _dot_tn((k * jnp.exp(bl - bc)).astype(BF), v)
        o_ref[rows, :] = _gla_out(o, god_ref[rows, :], ng_ref[...]).astype(o_ref.dtype)
        return carry

    lax.fori_loop(0, n_chunks, chunk, 0)
    s_ref[...] = st_ref[0:GLA_DK, :]


def gla_prompt(proj, n_b, t_len, wa_pad, ba_pad, norm_g):
    seq = lambda cb0: pl.BlockSpec((t_len, LANES), lambda b, h: (b, cb0 + h))
    return pl.pallas_call(
        functools.partial(_gla_prompt_kernel, n_chunks=t_len // GLA_CHUNK),
        grid=(n_b, GLA_H),
        in_specs=[seq(CB_QD), seq(CB_KD), seq(CB_VD), seq(CB_GOD),
                  pl.BlockSpec((t_len, LANES), lambda b, h: (b, CB_MISC)),
                  pl.BlockSpec((LANES, LANES), lambda b, h: (0, h)),
                  pl.BlockSpec((1, LANES), lambda b, h: (0, h)),
                  pl.BlockSpec((1, GLA_DV), lambda b, h: (0, 0))],
        out_specs=[pl.BlockSpec((t_len, GLA_DV), lambda b, h: (b, h)),
                   pl.BlockSpec((None, None, GLA_DK, GLA_DV), lambda b, h: (b, h, 0, 0))],
        out_shape=[jax.ShapeDtypeStruct((n_b * t_len, GROUP_W), BF),
                   jax.ShapeDtypeStruct((n_b, GLA_H, GLA_DK, GLA_DV), F32)],
        scratch_shapes=[pltpu.VMEM((LANES, GLA_DV), F32)],
        compiler_params=_cparams(("parallel", "arbitrary")),
        name="gla_prompt",
    )(proj, proj, proj, proj, proj, wa_pad, ba_pad, norm_g.reshape(1, GLA_DV))


def _gla_sample_kernel(q_ref, k_ref, v_ref, god_ref, misc_ref, wa_ref, ba_ref, ng_ref, s0_ref, o_ref, s_ref,
                       *, bb, t_q):
    n_r = bb * t_q
    r_i = lax.broadcasted_iota(jnp.int32, (n_r, n_r), 0)
    c_i = lax.broadcasted_iota(jnp.int32, (n_r, n_r), 1)
    causal = (r_i >= c_i) & (r_i // t_q == c_i // t_q)
    tri = jnp.where(causal, 1.0, 0.0).astype(BF)
    row_b = lax.broadcasted_iota(jnp.int32, (n_r, 1), 0) // t_q

    z = _dot(misc_ref[...].astype(BF), wa_ref[...]) + ba_ref[...]
    bc = _cumsum_rows(_log_sigmoid(z) / GLA_TAU, tri)
    q = q_ref[...] * (GLA_DK ** -0.5)
    k = k_ref[...]
    v = v_ref[...].astype(BF)
    qe = q * jnp.exp(bc)
    att = jnp.where(causal, _dot_nt(qe.astype(BF), (k * jnp.exp(-bc)).astype(BF)), 0.0)
    o = _dot(att.astype(BF), v)
    for b in range(bb):
        mine = row_b == b
        bl = bc[(b + 1) * t_q - 1:(b + 1) * t_q, :]
        s0 = s0_ref[b, 0]
        o = o + _dot(jnp.where(mine, qe, 0.0)[:, 0:GLA_DK].astype(BF), s0.astype(BF))
        k2 = jnp.where(mine, k * jnp.exp(bl - bc), 0.0).astype(BF)
        upd = _dot_tn(k2, v)
        s_ref[b, 0] = _rows_to_col(jnp.exp(bl))[0:GLA_DK, :] * s0 + upd[0:GLA_DK, :]
    o_ref[...] = _gla_out(o, god_ref[...], ng_ref[...]).astype(o_ref.dtype)


def gla_sample(proj, row0, n_b, t_q, wa_pad, ba_pad, norm_g, s0, s0_b0, bb):
    rb = bb * t_q
    base = row0 // rb
    s_i0 = s0_b0 // bb
    seq = lambda cb0: pl.BlockSpec((rb, LANES), lambda i, h: (base + i, cb0 + h))
    return pl.pallas_call(
        functools.partial(_gla_sample_kernel, bb=bb, t_q=t_q),
        grid=(n_b // bb, GLA_H),
        in_specs=[seq(CB_QD), seq(CB_KD), seq(CB_VD), seq(CB_GOD),
                  pl.BlockSpec((rb, LANES), lambda i, h: (base + i, CB_MISC)),
                  pl.BlockSpec((LANES, LANES), lambda i, h: (0, h)),
                  pl.BlockSpec((1, LANES), lambda i, h: (0, h)),
                  pl.BlockSpec((1, GLA_DV), lambda i, h: (0, 0)),
                  pl.BlockSpec((bb, 1, GLA_DK, GLA_DV), lambda i, h: (s_i0 + i, h, 0, 0))],
        out_specs=[pl.BlockSpec((rb, GLA_DV), lambda i, h: (i, h)),
                   pl.BlockSpec((bb, 1, GLA_DK, GLA_DV), lambda i, h: (i, h, 0, 0))],
        out_shape=[jax.ShapeDtypeStruct((n_b * t_q, GROUP_W), BF),
                   jax.ShapeDtypeStruct((n_b, GLA_H, GLA_DK, GLA_DV), F32)],
        compiler_params=_cparams(("parallel", "parallel")),
        name="gla_sample",
    )(proj, proj, proj, proj, proj, wa_pad, ba_pad, norm_g.reshape(1, GLA_DV), s0)


def _mem_attn_kernel(q_ref, kv_ref, o_ref):
    q = q_ref[...] * (MEM_DH ** -0.5)
    kv = kv_ref[...]
    for h in range(MEM_H):
        cs = slice(h * MEM_DH, (h + 1) * MEM_DH)
        s = _dot_nt(q[:, cs].astype(BF), kv[:, cs].astype(BF))
        e = jnp.exp(s - jnp.max(s, axis=1, keepdims=True))
        p = e / jnp.sum(e, axis=1, keepdims=True)
        vh = kv[:, MEM_H * MEM_DH + h * MEM_DH:MEM_H * MEM_DH + (h + 1) * MEM_DH]
        o_ref[:, cs] = _dot(p.astype(BF), vh.astype(BF)).astype(o_ref.dtype)


def mem_attention(qm, row0, n_b, t_len, kv, kv_b0, tq):
    n_t = t_len // tq
    base = row0 // tq
    w = MEM_H * MEM_DH
    return pl.pallas_call(
        _mem_attn_kernel,
        grid=(n_b, n_t),
        in_specs=[pl.BlockSpec((tq, w), lambda b, i: (base + b * n_t + i, 0)),
                  pl.BlockSpec((None, MEM_LEN, 2 * w), lambda b, i: (kv_b0 + b, 0, 0))],
        out_specs=pl.BlockSpec((tq, w), lambda b, i: (b * n_t + i, 0)),
        out_shape=jax.ShapeDtypeStruct((n_b * t_len, w), BF),
        compiler_params=_cparams(("parallel", "parallel")),
        name="mem_attention",
    )(qm, kv)


def _router_kernel(x_ref, w_ref, b_ref, idx_ref, gate_ref):
    logits = jnp.dot(x_ref[...], w_ref[...], precision=lax.Precision.HIGHEST,
                     preferred_element_type=F32) + b_ref[...]
    lane = lax.broadcasted_iota(jnp.int32, logits.shape, 1).astype(F32)
    vals = jnp.where(lane < N_EXPERTS, logits, NEG_INF)
    idx_out = jnp.zeros(logits.shape, F32)
    e_out = jnp.zeros(logits.shape, F32)
    den = None
    top0 = None
    for k in range(TOP_K):
        m = jnp.max(vals, axis=1, keepdims=True)
        idx = jnp.min(jnp.where(vals == m, lane, float(LANES)), axis=1, keepdims=True)
        if k == 0:
            top0 = m
        e = jnp.exp(m - top0)
        den = e if den is None else den + e
        idx_out = jnp.where(lane == k, idx, idx_out)
        e_out = jnp.where(lane == k, e, e_out)
        vals = jnp.where(lane == idx, NEG_INF, vals)
    idx_ref[...] = idx_out.astype(jnp.int32)
    gate_ref[...] = e_out / den


def moe_router(x, w_pad, b_pad, tm):
    m, d = x.shape
    return pl.pallas_call(
        _router_kernel,
        grid=(m // tm,),
        in_specs=[pl.BlockSpec((tm, d), lambda i: (i, 0)),
                  pl.BlockSpec((d, LANES), lambda i: (0, 0)),
                  pl.BlockSpec((1, LANES), lambda i: (0, 0))],
        out_specs=[pl.BlockSpec((tm, LANES), lambda i: (i, 0)),
                   pl.BlockSpec((tm, LANES), lambda i: (i, 0))],
        out_shape=[jax.ShapeDtypeStruct((m, LANES), jnp.int32),
                   jax.ShapeDtypeStruct((m, LANES), F32)],
        compiler_params=_cparams(("parallel",)),
        name="moe_router",
    )(x, w_pad, b_pad)


def _moe_up_kernel(be_ref, nu_ref, x_ref, wg_ref, wu_ref, bg_ref, bu_ref, o_ref, wgb_ref, wub_ref):
    i = pl.program_id(1)
    changed = jnp.logical_or(i == 0, be_ref[i] != be_ref[jnp.maximum(i - 1, 0)])

    @pl.when(changed)
    def _():
        wgb_ref[...] = wg_ref[...].astype(BF)
        wub_ref[...] = wu_ref[...].astype(BF)

    @pl.when(i < nu_ref[0])
    def _():
        x = x_ref[...]
        g = jnp.minimum(_dot(x, wgb_ref[...]) + bg_ref[...], SWIGLU_LIMIT)
        u = jnp.clip(_dot(x, wub_ref[...]) + bu_ref[...], -SWIGLU_LIMIT, SWIGLU_LIMIT)
        o_ref[...] = ((u + 1.0) * g * _sigmoid(SWIGLU_ALPHA * g)).astype(o_ref.dtype)

    @pl.when(i >= nu_ref[0])
    def _():
        o_ref[...] = jnp.zeros(o_ref.shape, o_ref.dtype)


def moe_up(xg, blk_expert, n_used, w1, b1):
    cap, d = xg.shape
    n_blk = cap // MOE_TM
    n_j = D_FF // MOE_TN
    return pl.pallas_call(
        _moe_up_kernel,
        grid_spec=pltpu.PrefetchScalarGridSpec(
            num_scalar_prefetch=2,
            grid=(n_j, n_blk),
            in_specs=[pl.BlockSpec((MOE_TM, d), lambda j, i, be, nu: (jnp.minimum(i, nu[0] - 1), 0)),
                      pl.BlockSpec((None, d, MOE_TN), lambda j, i, be, nu: (be[i], 0, j)),
                      pl.BlockSpec((None, d, MOE_TN), lambda j, i, be, nu: (be[i], 0, n_j + j)),
                      pl.BlockSpec((None, 1, MOE_TN), lambda j, i, be, nu: (be[i], 0, j)),
                      pl.BlockSpec((None, 1, MOE_TN), lambda j, i, be, nu: (be[i], 0, n_j + j))],
            out_specs=pl.BlockSpec((MOE_TM, MOE_TN), lambda j, i, be, nu: (i, j)),
            scratch_shapes=[pltpu.VMEM((d, MOE_TN), BF), pltpu.VMEM((d, MOE_TN), BF)]),
        out_shape=jax.ShapeDtypeStruct((cap, D_FF), BF),
        compiler_params=_cparams(("arbitrary", "arbitrary")),
        name="moe_up",
    )(blk_expert, n_used, xg, w1, w1, b1, b1)


def _moe_down_kernel(be_ref, nu_ref, a_ref, w_ref, b_ref, o_ref, wb_ref):
    i = pl.program_id(1)
    changed = jnp.logical_or(i == 0, be_ref[i] != be_ref[jnp.maximum(i - 1, 0)])

    @pl.when(changed)
    def _():
        wb_ref[...] = w_ref[...].astype(BF)

    @pl.when(i < nu_ref[0])
    def _():
        o_ref[...] = _dot(a_ref[...], wb_ref[...]) + b_ref[...]

    @pl.when(i >= nu_ref[0])
    def _():
        o_ref[...] = jnp.zeros(o_ref.shape, o_ref.dtype)


def moe_down(act, blk_expert, n_used, w2, b2):
    cap, f = act.shape
    d = w2.shape[2]
    n_blk = cap // MOE_TM
    return pl.pallas_call(
        _moe_down_kernel,
        grid_spec=pltpu.PrefetchScalarGridSpec(
            num_scalar_prefetch=2,
            grid=(d // MOE_TN, n_blk),
            in_specs=[pl.BlockSpec((MOE_TM, f), lambda j, i, be, nu: (jnp.minimum(i, nu[0] - 1), 0)),
                      pl.BlockSpec((None, f, MOE_TN), lambda j, i, be, nu: (be[i], 0, j)),
                      pl.BlockSpec((None, 1, MOE_TN), lambda j, i, be, nu: (be[i], 0, j))],
            out_specs=pl.BlockSpec((MOE_TM, MOE_TN), lambda j, i, be, nu: (i, j)),
            scratch_shapes=[pltpu.VMEM((f, MOE_TN), BF)]),
        out_shape=jax.ShapeDtypeStruct((cap, d), F32),
        compiler_params=_cparams(("arbitrary", "arbitrary")),
        name="moe_down",
    )(blk_expert, n_used, act, w2, b2)


def moe_ffn(x, router_w, router_b, w1, b1, w2, b2, e0):
    n, d = x.shape
    w_pad = jnp.pad(router_w, ((0, 0), (0, LANES - N_EXPERTS)))
    b_pad = jnp.pad(router_b, (0, LANES - N_EXPERTS)).reshape(1, LANES)
    idx, gate = moe_router(x, w_pad, b_pad, 512)
    top_e = idx[:, :TOP_K]
    gate = gate[:, :TOP_K]

    n_assign = n * TOP_K
    e_flat = top_e.reshape(-1)
    order = jnp.argsort(e_flat, stable=True)
    e_sorted = e_flat[order]
    counts = jnp.bincount(e_flat, length=N_EXPERTS)
    padded = (counts + MOE_TM - 1) // MOE_TM * MOE_TM
    start = jnp.cumsum(counts) - counts
    pad_end = jnp.cumsum(padded)
    pad_start = pad_end - padded
    dest = (pad_start[e_sorted] + jnp.arange(n_assign) - start[e_sorted]).astype(jnp.int32)
    n_blk = -(-(n_assign + N_EXPERTS * (MOE_TM - 1)) // MOE_TM)
    cap = n_blk * MOE_TM
    slot_tok = jnp.zeros((cap,), jnp.int32).at[dest].set((order // TOP_K).astype(jnp.int32))
    slot_of_assign = jnp.zeros((n_assign,), jnp.int32).at[order].set(dest)
    blk_expert = e0 + jnp.minimum(jnp.searchsorted(pad_end, jnp.arange(n_blk) * MOE_TM, side='right'),
                                  N_EXPERTS - 1).astype(jnp.int32)
    n_used = (pad_end[-1] // MOE_TM).astype(jnp.int32).reshape(1)

    xg = x.astype(BF)[slot_tok]
    act = moe_up(xg, blk_expert, n_used, w1, b1)
    yb = moe_down(act, blk_expert, n_used, w2, b2)
    y = jnp.sum(yb[slot_of_assign.reshape(n, TOP_K)] * gate[:, :, None], axis=1)
    return y


def _pad_heads(w):
    lead = w.shape[:-1]
    w = w.reshape(lead + (GLA_H, GLA_DK))
    w = jnp.pad(w, [(0, 0)] * len(lead) + [(0, 0), (0, LANES - GLA_DK)])
    return w.reshape(lead + (GLA_H * LANES,))


def _relayout_w_in(w):
    c = np.cumsum((GROUP_W, 2 * GROUP_W, GROUP_W, 6 * NSA_DH, 3 * NSA_H, GLA_H * GLA_DK, GLA_H * GLA_DK, GROUP_W,
                   GLA_RANK, GROUP_W))
    gc, qd, kd, vd, lrd, god = (w[:, c[3]:c[4]], w[:, c[4]:c[5]], w[:, c[5]:c[6]], w[:, c[6]:c[7]],
                                w[:, c[7]:c[8]], w[:, c[8]:c[9]])
    misc = jnp.pad(jnp.concatenate([gc, lrd], axis=1), ((0, 0), (0, LANES - 3 * NSA_H - GLA_RANK)))
    out = jnp.concatenate([w[:, :c[3]], _pad_heads(qd), _pad_heads(kd), vd, god, misc,
                           jnp.zeros((w.shape[0], LANES), w.dtype)], axis=1)
    assert out.shape[1] == D_PROJ
    return out.astype(BF)


def kernel(x_prompt, x_sample, cache_pool, cache_nsa_kv, cache_win_kv, state_gla, cache_mem_kv, page_table,
           mem_prompt, ln_in_g, ln_in_b, w_in, w_out, pool_w, pool_scale, gmlp_ln_g, gmlp_ln_b, gmlp_ws, gmlp_bs,
           nsa_phi, gla_wa, gla_ba, gla_norm_g, ln1_g, ln1_b, mem_wq, mem_wkv, mem_wo, ln2_g, ln2_b,
           router_w, router_b, moe_w1, moe_b1, moe_w2, moe_b2, ln3_g, ln3_b):
    n_bp, t_p, d = x_prompt.shape
    n_bs, t_s, _ = x_sample.shape
    n_p = n_bp * t_p
    n_s = n_bs * t_s
    n_pages = page_table.shape[1]
    page = cache_nsa_kv.shape[2]
    past_len = n_pages * page
    n_wc = cache_win_kv.shape[2]
    assert t_p % 512 == 0 and n_s % 512 == 0 and t_s == 8 and past_len % SEL_BLOCK == 0 and n_wc == WINDOW

    x = jnp.concatenate([ln_rows(x_prompt.reshape(n_p, d), ln_in_g, ln_in_b, 512),
                         ln_rows(x_sample.reshape(n_s, d), ln_in_g, ln_in_b, 512)], axis=0)
    n_tok = n_p + n_s
    tm_big = 1024 if n_tok % 1024 == 0 else 512
    n_pool = cache_nsa_kv.shape[1]
    cache_pages = cache_nsa_kv.reshape(DEPTH * n_pool, page, 4 * NSA_DH)
    cache_win = cache_win_kv.reshape(DEPTH * n_bs, n_wc, 2 * NSA_DH)
    cache_mem = cache_mem_kv.reshape(DEPTH * n_bs, MEM_LEN, 2 * MEM_H * MEM_DH)
    state0 = state_gla.reshape(DEPTH * n_bs, GLA_H, GLA_DK, GLA_DV)
    w1_all = moe_w1.reshape(DEPTH * N_EXPERTS, d, 2 * D_FF)
    b1_all = moe_b1.reshape(DEPTH * N_EXPERTS, 1, 2 * D_FF)
    w2_all = moe_w2.reshape(DEPTH * N_EXPERTS, D_FF, d)
    b2_all = moe_b2.reshape(DEPTH * N_EXPERTS, 1, d)

    expand_p = (np.arange(t_p)[None, :] // SEL_BLOCK == np.arange(t_p // SEL_BLOCK)[:, None])
    expand_p = jnp.asarray(expand_p, BF)
    expand_s = (np.arange(past_len + LANES)[None, :] // SEL_BLOCK == np.arange(LANES)[:, None])
    expand_s = jnp.asarray(expand_s, BF)
    pair_s = jnp.asarray(np.arange(past_len // CMP_BLOCK)[:, None] // 2 == np.arange(LANES)[None, :], BF)
    bb_s = LANES // t_s
    eye_bb = jnp.eye(bb_s, dtype=F32)

    outs = {k: [] for k in ("pool_p", "nsa_p", "win_p", "gla_p", "mem_p", "pool_s", "chunk_s", "nsa_s", "win_s",
                            "gla_s")}
    for l in range(DEPTH):
        proj = matmul(x, _relayout_w_in(w_in[l]), tm_big, 512)

        pre_p = jnp.zeros((n_bp, POOL_PRE, GROUP_W), F32)
        pre_s = jnp.pad(cache_pool[l], ((0, 0), (1, 0), (0, 0)))
        ya = jnp.concatenate([
            pool_mixer(proj, 0, n_bp, t_p, pre_p, pool_w[l], pool_scale[l], 0, 1, 512),
            pool_mixer(proj, n_p, n_bs, t_s, pre_s, pool_w[l], pool_scale[l], past_len, bb_s, t_s)], axis=0)

        tril = jnp.tril(jnp.ones((GMLP_CHUNK, GMLP_CHUNK), F32))
        w_mix_p = (gmlp_ws[l] * tril).astype(BF)
        bias_p = jnp.repeat(gmlp_bs[l].T, LANES, axis=1)
        ws_s = gmlp_ws[l][:, :t_s, :t_s] * tril[:t_s, :t_s]
        w_mix_s = jnp.einsum('ab,gts->gatbs', eye_bb, ws_s).reshape(4, LANES, LANES).astype(BF)
        bias_s = jnp.tile(jnp.repeat(gmlp_bs[l][:, :t_s].T, LANES, axis=1), (bb_s, 1))
        yb_p, _ = gmlp_mixer(proj, 0, n_p, gmlp_ln_g[l], gmlp_ln_b[l], w_mix_p, bias_p, 4)
        yb_s, v_s = gmlp_mixer(proj, n_p, n_s, gmlp_ln_g[l], gmlp_ln_b[l], w_mix_s, bias_s, 4)
        yb = jnp.concatenate([yb_p, yb_s], axis=0)

        phi = nsa_phi[l].astype(BF)
        proj3 = proj.reshape(n_tok // CMP_BLOCK, CMP_BLOCK, D_PROJ)
        tb_p = t_p // CMP_BLOCK
        kc_p = cmp_project(proj3, CB_CMPK, 0, n_p // CMP_BLOCK, phi[0], tb_p)
        vc_p = cmp_project(proj3, CB_CMPV, 0, n_p // CMP_BLOCK, phi[1], tb_p)
        yc_p = nsa_prompt(proj, n_bp, t_p, kc_p, vc_p, expand_p)
        past = gather_pages(cache_pages, page_table + l * n_pool, 8)
        past3 = past.reshape(n_bs * past_len // CMP_BLOCK, CMP_BLOCK, 4 * NSA_DH)
        kc_s = cmp_project(past3, 0, 0, past3.shape[0], phi[0], 256)
        vc_s = cmp_project(past3, 1, 0, past3.shape[0], phi[1], 256)
        yc_s = nsa_sample(proj, n_p, n_bs, t_s, kc_s, vc_s, past.reshape(n_bs, past_len, 4 * NSA_DH),
                          cache_win, l * n_bs, pair_s, expand_s)
        yc = jnp.concatenate([yc_p, yc_s], axis=0)

        wa_pad = jnp.zeros((LANES, GLA_H * LANES), F32).at[MISC_LR0:MISC_LR0 + GLA_RANK].set(_pad_heads(gla_wa[l]))
        wa_pad = wa_pad.astype(BF)
        ba_pad = _pad_heads(gla_ba[l]).reshape(1, GLA_H * LANES)
        yd_p, st_p = gla_prompt(proj, n_bp, t_p, wa_pad, ba_pad, gla_norm_g[l])
        yd_s, st_s = gla_sample(proj, n_p, n_bs, t_s, wa_pad, ba_pad, gla_norm_g[l], state0, l * n_bs, bb_s)
        yd = jnp.concatenate([yd_p, yd_s], axis=0)

        x = matmul_res_ln([ya, yb, yc, yd], w_out[l].astype(BF), x, ln1_g[l], ln1_b[l], 256)

        qm = matmul(x, mem_wq[l].astype(BF), tm_big, 512)
        mem_kv_p = matmul(mem_prompt.reshape(n_bp * MEM_LEN, d), mem_wkv[l].astype(BF), 512, 512)
        w_kv = 2 * MEM_H * MEM_DH
        om = jnp.concatenate([
            mem_attention(qm, 0, n_bp, t_p, mem_kv_p.reshape(n_bp, MEM_LEN, w_kv), 0, 512),
            mem_attention(qm, n_p, n_bs, t_s, cache_mem, l * n_bs, t_s)], axis=0)
        x = matmul_res_ln([om], mem_wo[l].astype(BF), x, ln2_g[l], ln2_b[l], 256)

        y = moe_ffn(x, router_w[l], router_b[l], w1_all, b1_all, w2_all, b2_all, l * N_EXPERTS)
        x = res_ln(y, x, ln3_g[l], ln3_b[l], 512)

        pp = proj[:n_p].reshape(n_bp, t_p, D_PROJ)
        ps = proj[n_p:].reshape(n_bs, t_s, D_PROJ)
        outs["pool_p"].append(pp[:, t_p - POOL_BUF:, :GROUP_W])
        outs["nsa_p"].append(pp[:, :, CB_CMPK * LANES:CB_WINK * LANES].reshape(n_bp, t_p, 4, NSA_DH))
        outs["win_p"].append(pp[:, t_p - min(WINDOW, t_p):, CB_WINK * LANES:CB_QD * LANES]
                             .reshape(n_bp, min(WINDOW, t_p), 2, NSA_DH))
        outs["gla_p"].append(st_p)
        outs["mem_p"].append(mem_kv_p.reshape(n_bp, MEM_LEN, 2, MEM_H, MEM_DH))
        outs["pool_s"].append(ps[:, :, :GROUP_W])
        outs["chunk_s"].append(v_s.reshape(n_bs, t_s, GROUP_W))
        outs["nsa_s"].append(ps[:, :, CB_CMPK * LANES:CB_WINK * LANES].reshape(n_bs, t_s, 4, NSA_DH))
        outs["win_s"].append(ps[:, :, CB_WINK * LANES:CB_QD * LANES].reshape(n_bs, t_s, 2, NSA_DH))
        outs["gla_s"].append(st_s)

    st = lambda k: jnp.stack(outs[k])
    return (x[:n_p].reshape(n_bp, t_p, d), x[n_p:].reshape(n_bs, t_s, d), st("pool_p"), st("nsa_p"), st("win_p"),
            st("gla_p"), st("mem_p"), st("pool_s"), st("chunk_s"), st("nsa_s"), st("win_s"), st("gla_s"))
```

```python
import functools

import jax
import jax.numpy as jnp
import numpy as np
from jax import lax
from jax.experimental import pallas as pl
from jax.experimental.pallas import tpu as pltpu

BF = jnp.bfloat16
F32 = jnp.float32

D_MODEL = 2048
DEPTH = 2
GROUP_W = 512
LANES = 128
POOL_WINDOWS = (2, 4, 8, 16)
POOL_BUF = 15
POOL_PRE = 16
GMLP_CHUNK = 128
NSA_DH = 128
NSA_H = 4
CMP_BLOCK = 32
SEL_BLOCK = 64
SEL_TOP = 16
WINDOW = 512
Q_BLOCK = 128
GLA_H = 4
GLA_DK = 64
GLA_DV = 128
GLA_RANK = 16
GLA_TAU = 16.0
GLA_CHUNK = 64
MEM_LEN = 256
MEM_H = 4
MEM_DH = 128
N_EXPERTS = 32
TOP_K = 4
D_FF = 2048
SWIGLU_LIMIT = 7.0
SWIGLU_ALPHA = 1.702
DN_ALPHA = (2.0 * DEPTH) ** 0.25
LN_EPS = 1e-5
NEG_INF = float("-inf")

CB_XA, CB_U, CB_V, CB_QC = 0, 4, 8, 12
CB_CMPK, CB_CMPV, CB_SELK, CB_SELV, CB_WINK, CB_WINV = 16, 17, 18, 19, 20, 21
CB_MISC, CB_QD, CB_KD, CB_VD, CB_GOD = 22, 24, 28, 32, 36
N_CB = 40
D_PROJ = N_CB * LANES
MISC_GATE0 = 0
MISC_LR0 = 12

MOE_TM = 512
MOE_TN = 512
MOE_TOK = 256
VMEM_LIMIT = 56 * 1024 * 1024


def _cparams(sem):
    return pltpu.CompilerParams(dimension_semantics=sem, vmem_limit_bytes=VMEM_LIMIT)


def _layer_norm(x, g, b):
    mu = jnp.mean(x, -1, keepdims=True)
    xc = x - mu
    var = jnp.mean(xc * xc, -1, keepdims=True)
    return xc * lax.rsqrt(var + LN_EPS) * g + b


def _dot(a, b):
    return jnp.dot(a, b, preferred_element_type=F32)


def _dot_nt(a, b):
    return lax.dot_general(a, b, (((1,), (1,)), ((), ())), preferred_element_type=F32)


def _dot_tn(a, b):
    return lax.dot_general(a, b, (((0,), (0,)), ((), ())), preferred_element_type=F32)


def _sigmoid(x):
    return 1.0 / (1.0 + jnp.exp(-x))


def _ln_kernel(x_ref, g_ref, b_ref, o_ref):
    o_ref[...] = _layer_norm(x_ref[...], g_ref[...], b_ref[...])


def ln_rows(x, g, b, tm):
    m, d = x.shape
    assert m % tm == 0, (m, tm)
    return pl.pallas_call(
        _ln_kernel,
        grid=(m // tm,),
        in_specs=[pl.BlockSpec((tm, d), lambda i: (i, 0)),
                  pl.BlockSpec((1, d), lambda i: (0, 0)),
                  pl.BlockSpec((1, d), lambda i: (0, 0))],
        out_specs=pl.BlockSpec((tm, d), lambda i: (i, 0)),
        out_shape=jax.ShapeDtypeStruct((m, d), F32),
        compiler_params=_cparams(("parallel",)),
        name="ln_rows",
    )(x, g.reshape(1, d), b.reshape(1, d))


def _mm_kernel(x_ref, w_ref, o_ref, xb_ref):
    @pl.when(pl.program_id(1) == 0)
    def _():
        xb_ref[...] = x_ref[...].astype(BF)

    o_ref[...] = _dot(xb_ref[...], w_ref[...]).astype(o_ref.dtype)


def matmul(x, w, tm, tn, out_dtype=F32):
    m, k = x.shape
    n = w.shape[1]
    assert m % tm == 0 and n % tn == 0, (m, tm, n, tn)
    return pl.pallas_call(
        _mm_kernel,
        grid=(m // tm, n // tn),
        in_specs=[pl.BlockSpec((tm, k), lambda i, j: (i, 0)),
                  pl.BlockSpec((k, tn), lambda i, j: (0, j))],
        out_specs=pl.BlockSpec((tm, tn), lambda i, j: (i, j)),
        out_shape=jax.ShapeDtypeStruct((m, n), out_dtype),
        scratch_shapes=[pltpu.VMEM((tm, k), BF)],
        compiler_params=_cparams(("parallel", "arbitrary")),
        name="matmul",
    )(x, w)


def _mm_res_ln_kernel(*refs, n_in):
    xs = refs[:n_in]
    w_ref, r_ref, g_ref, b_ref, o_ref = refs[n_in:]
    acc = None
    k0 = 0
    for x_ref in xs:
        kk = x_ref.shape[1]
        part = _dot(x_ref[...].astype(BF), w_ref[k0:k0 + kk, :])
        acc = part if acc is None else acc + part
        k0 += kk
    o_ref[...] = _layer_norm(DN_ALPHA * r_ref[...] + acc, g_ref[...], b_ref[...])


def matmul_res_ln(xs, w, resid, g, b, tm):
    m, d = resid.shape
    assert m % tm == 0, (m, tm)
    in_specs = [pl.BlockSpec((tm, x.shape[1]), lambda i: (i, 0)) for x in xs]
    in_specs += [pl.BlockSpec(w.shape, lambda i: (0, 0)),
                 pl.BlockSpec((tm, d), lambda i: (i, 0)),
                 pl.BlockSpec((1, d), lambda i: (0, 0)),
                 pl.BlockSpec((1, d), lambda i: (0, 0))]
    return pl.pallas_call(
        functools.partial(_mm_res_ln_kernel, n_in=len(xs)),
        grid=(m // tm,),
        in_specs=in_specs,
        out_specs=pl.BlockSpec((tm, d), lambda i: (i, 0)),
        out_shape=jax.ShapeDtypeStruct((m, d), F32),
        compiler_params=_cparams(("parallel",)),
        name="matmul_res_ln",
    )(*xs, w, resid, g.reshape(1, d), b.reshape(1, d))


def _pool_kernel(x_ref, pre_ref, w_ref, sc_ref, o_ref, ext_ref, *, bb, tt, n_tiles, t0):
    j = pl.program_id(1)

    @pl.when(j == 0)
    def _():
        ext_ref[:, 0:POOL_PRE, :] = pre_ref[...]

    x = x_ref[...].reshape(bb, tt, GROUP_W)
    ext_ref[:, POOL_PRE:, :] = x
    t_idx = j * tt + lax.broadcasted_iota(jnp.int32, (1, tt, LANES), 1)
    for g, win in enumerate(POOL_WINDOWS):
        cs = slice(g * LANES, (g + 1) * LANES)
        xg = ext_ref[:, POOL_PRE:POOL_PRE + tt, cs]
        s = xg
        for k in range(1, win):
            s = s + ext_ref[:, POOL_PRE - k:POOL_PRE - k + tt, cs]
        cnt = jnp.minimum(win, t0 + 1 + t_idx).astype(F32)
        d = s / cnt - xg
        y = _dot(d.reshape(bb * tt, LANES).astype(BF), w_ref[g])
        o_ref[:, cs] = (y * sc_ref[:, cs]).astype(o_ref.dtype)
    if n_tiles > 1:
        ext_ref[:, 0:POOL_PRE, :] = ext_ref[:, tt:tt + POOL_PRE, :]


def pool_mixer(proj, row0, n_b, t_len, prefix, pool_w, pool_scale, t0, bb, tt):
    n_tiles = t_len // tt
    rb = bb * tt
    base = row0 // rb
    return pl.pallas_call(
        functools.partial(_pool_kernel, bb=bb, tt=tt, n_tiles=n_tiles, t0=t0),
        grid=(n_b // bb, n_tiles),
        in_specs=[pl.BlockSpec((rb, GROUP_W), lambda i, j: (base + i * n_tiles + j, CB_XA // 4)),
                  pl.BlockSpec((bb, POOL_PRE, GROUP_W), lambda i, j: (i, 0, 0)),
                  pl.BlockSpec((4, LANES, LANES), lambda i, j: (0, 0, 0)),
                  pl.BlockSpec((1, GROUP_W), lambda i, j: (0, 0))],
        out_specs=pl.BlockSpec((rb, GROUP_W), lambda i, j: (i * n_tiles + j, 0)),
        out_shape=jax.ShapeDtypeStruct((n_b * t_len, GROUP_W), BF),
        scratch_shapes=[pltpu.VMEM((bb, POOL_PRE + tt, GROUP_W), F32)],
        compiler_params=_cparams(("parallel", "arbitrary")),
        name="pool_mixer",
    )(proj, prefix, pool_w.astype(BF), pool_scale.reshape(1, GROUP_W))


def _gelu_tanh(x):
    return 0.5 * x * (1.0 + jnp.tanh(np.sqrt(2.0 / np.pi) * (x + 0.044715 * (x * x * x))))


def _gmlp_kernel(u_ref, v_ref, g_ref, b_ref, w_ref, bias_ref, y_ref, vo_ref, *, n_ch):
    for c in range(n_ch):
        rows = slice(c * GMLP_CHUNK, (c + 1) * GMLP_CHUNK)
        u = _gelu_tanh(u_ref[rows, :])
        v = _layer_norm(_gelu_tanh(v_ref[rows, :]), g_ref[...], b_ref[...])
        vo_ref[rows, :] = v
        vb = v.astype(BF)
        for g in range(4):
            cs = slice(g * LANES, (g + 1) * LANES)
            mixed = _dot(w_ref[g], vb[:, cs]) + bias_ref[:, cs]
            y_ref[rows, cs] = (u[:, cs] * mixed).astype(y_ref.dtype)


def gmlp_mixer(proj, row0, n_rows, ln_g, ln_b, w_mix, bias_full, n_ch):
    rb = n_ch * GMLP_CHUNK
    base = row0 // rb
    return pl.pallas_call(
        functools.partial(_gmlp_kernel, n_ch=n_ch),
        grid=(n_rows // rb,),
        in_specs=[pl.BlockSpec((rb, GROUP_W), lambda i: (base + i, CB_U // 4)),
                  pl.BlockSpec((rb, GROUP_W), lambda i: (base + i, CB_V // 4)),
                  pl.BlockSpec((1, GROUP_W), lambda i: (0, 0)),
                  pl.BlockSpec((1, GROUP_W), lambda i: (0, 0)),
                  pl.BlockSpec((4, GMLP_CHUNK, GMLP_CHUNK), lambda i: (0, 0, 0)),
                  pl.BlockSpec((GMLP_CHUNK, GROUP_W), lambda i: (0, 0))],
        out_specs=[pl.BlockSpec((rb, GROUP_W), lambda i: (i, 0)),
                   pl.BlockSpec((rb, GROUP_W), lambda i: (i, 0))],
        out_shape=[jax.ShapeDtypeStruct((n_rows, GROUP_W), BF),
                   jax.ShapeDtypeStruct((n_rows, GROUP_W), F32)],
        compiler_params=_cparams(("parallel",)),
        name="gmlp_mixer",
    )(proj, proj, ln_g.reshape(1, GROUP_W), ln_b.reshape(1, GROUP_W), w_mix, bias_full)


def _cmp_accumulate(row_pair, phi_ref):
    acc = None
    for l2 in range(CMP_BLOCK // 2):
        xa, xb = row_pair(l2)
        part = _dot(jnp.concatenate([xa, xb], axis=1).astype(BF), phi_ref[l2])
        acc = part if acc is None else acc + part
    return acc


def _cmp_kernel(xk_ref, xv_ref, phik_ref, phiv_ref, ok_ref, ov_ref, *, tb):
    for x_ref, phi_ref, o_ref in ((xk_ref, phik_ref, ok_ref), (xv_ref, phiv_ref, ov_ref)):
        o_ref[...] = _cmp_accumulate(
            lambda l2: (x_ref[pl.ds(2 * l2, tb, stride=CMP_BLOCK), :],
                        x_ref[pl.ds(2 * l2 + 1, tb, stride=CMP_BLOCK), :]), phi_ref)


def cmp_project(proj, n_blk, phik2, phiv2, tb):
    assert n_blk % tb == 0, (n_blk, tb)
    rows = tb * CMP_BLOCK
    return pl.pallas_call(
        functools.partial(_cmp_kernel, tb=tb),
        grid=(n_blk // tb,),
        in_specs=[pl.BlockSpec((rows, LANES), lambda i: (i, CB_CMPK)),
                  pl.BlockSpec((rows, LANES), lambda i: (i, CB_CMPV)),
                  pl.BlockSpec(phik2.shape, lambda i: (0, 0, 0)),
                  pl.BlockSpec(phiv2.shape, lambda i: (0, 0, 0))],
        out_specs=[pl.BlockSpec((tb, NSA_DH), lambda i: (i, 0)),
                   pl.BlockSpec((tb, NSA_DH), lambda i: (i, 0))],
        out_shape=[jax.ShapeDtypeStruct((n_blk, NSA_DH), F32),
                   jax.ShapeDtypeStruct((n_blk, NSA_DH), F32)],
        compiler_params=_cparams(("parallel",)),
        name="cmp_project",
    )(proj, proj, phik2, phiv2)


def _cmp_pool_kernel(x_ref, phik_ref, phiv_ref, ok_ref, ov_ref):
    for s, phi_ref, o_ref in ((0, phik_ref, ok_ref), (1, phiv_ref, ov_ref)):
        o_ref[...] = _cmp_accumulate(lambda l2: (x_ref[:, 2 * l2, s, :], x_ref[:, 2 * l2 + 1, s, :]), phi_ref)


def cmp_pool(cache_blocks, blk0, n_blk, phik2, phiv2, tb):
    assert n_blk % tb == 0 and blk0 % tb == 0, (n_blk, blk0, tb)
    base = blk0 // tb
    return pl.pallas_call(
        _cmp_pool_kernel,
        grid=(n_blk // tb,),
        in_specs=[pl.BlockSpec((tb, CMP_BLOCK, 4, NSA_DH), lambda i: (base + i, 0, 0, 0)),
                  pl.BlockSpec(phik2.shape, lambda i: (0, 0, 0)),
                  pl.BlockSpec(phiv2.shape, lambda i: (0, 0, 0))],
        out_specs=[pl.BlockSpec((tb, NSA_DH), lambda i: (i, 0)),
                   pl.BlockSpec((tb, NSA_DH), lambda i: (i, 0))],
        out_shape=[jax.ShapeDtypeStruct((n_blk, NSA_DH), F32),
                   jax.ShapeDtypeStruct((n_blk, NSA_DH), F32)],
        compiler_params=_cparams(("parallel",)),
        name="cmp_pool",
    )(cache_blocks, phik2, phiv2)


def _alibi_slope(h):
    return 2.0 ** (-8.0 * (h + 1) / NSA_H)


def _head_rows_const(n_q, fn):
    r = lax.broadcasted_iota(jnp.int32, (NSA_H * n_q, 1), 0)
    out = jnp.full((NSA_H * n_q, 1), fn(NSA_H - 1), F32)
    for h in range(NSA_H - 2, -1, -1):
        out = jnp.where(r < (h + 1) * n_q, fn(h), out)
    return out


def _stack_heads(q):
    return jnp.concatenate([q[:, h * NSA_DH:(h + 1) * NSA_DH] for h in range(NSA_H)], axis=0)


def _select_blocks(score, n_blk):
    blk = lax.broadcasted_iota(jnp.int32, score.shape, 0)
    cnt = jnp.zeros(score.shape, F32)
    for i in range(n_blk):
        row = score[i:i + 1, :]
        beats = (row > score) | ((row == score) & (blk > i))
        cnt = cnt + jnp.where(beats, 1.0, 0.0)
    return jnp.where((cnt < float(SEL_TOP)) & (score > NEG_INF), 1.0, 0.0)


def _softmax_segments(segs):
    m = None
    for s, _ in segs:
        mi = jnp.max(s, axis=1, keepdims=True)
        m = mi if m is None else jnp.maximum(m, mi)
    m = jnp.where(m > NEG_INF, m, 0.0)
    den = None
    acc = None
    for s, v in segs:
        e = jnp.exp(s - m)
        d = jnp.sum(e, axis=1, keepdims=True)
        a = _dot(e.astype(BF), v)
        den = d if den is None else den + d
        acc = a if acc is None else acc + a
    return acc / jnp.where(den > 0, den, 1.0)


def _gated_heads(o_ref, gates, o_c, o_s, o_w, n_q):
    for h in range(NSA_H):
        rows = slice(h * n_q, (h + 1) * n_q)
        c = MISC_GATE0 + 3 * h
        out = (gates[:, c:c + 1] * o_c[rows, :] + gates[:, c + 1:c + 2] * o_s[rows, :]
               + gates[:, c + 2:c + 3] * o_w[rows, :])
        o_ref[:, h * NSA_DH:(h + 1) * NSA_DH] = out.astype(o_ref.dtype)


def _nsa_prompt_kernel(q_ref, misc_ref, kc_ref, vc_ref, ks_ref, vs_ref, kw_ref, vw_ref, e_ref, o_ref,
                       imp_ref, mask_ref, m_ref, l_ref, acc_ref, *, t_len):
    n_q = Q_BLOCK
    n_cmp = t_len // CMP_BLOCK
    n_sel = t_len // SEL_BLOCK
    kt = 512
    q0 = pl.program_id(1) * n_q
    q4 = _stack_heads(q_ref[...] * (NSA_DH ** -0.5)).astype(BF)
    slope_r = _head_rows_const(n_q, _alibi_slope)
    qpos_r = q0 + lax.rem(lax.broadcasted_iota(jnp.int32, (NSA_H * n_q, 1), 0), n_q)

    lane = lax.broadcasted_iota(jnp.int32, (1, NSA_H * n_q), 1)
    qpos_l = q0 + lax.rem(lane, n_q)
    slope_l = jnp.full((1, NSA_H * n_q), _alibi_slope(NSA_H - 1), F32)
    for h in range(NSA_H - 2, -1, -1):
        slope_l = jnp.where(lane < (h + 1) * n_q, _alibi_slope(h), slope_l)
    cmp_end = lax.broadcasted_iota(jnp.int32, (n_cmp, 1), 0) * CMP_BLOCK + (CMP_BLOCK - 1)
    s_c = _dot_nt(kc_ref[...].astype(BF), q4) - slope_l * (qpos_l - cmp_end).astype(F32)
    s_c = jnp.where(cmp_end <= qpos_l, s_c, NEG_INF)
    m_c = jnp.max(s_c, axis=0, keepdims=True)
    m_c = jnp.where(m_c > NEG_INF, m_c, 0.0)
    e_c = jnp.exp(s_c - m_c)
    den_c = jnp.sum(e_c, axis=0, keepdims=True)
    p_c = e_c / jnp.where(den_c > 0, den_c, 1.0)
    o_c = _dot_tn(p_c.astype(BF), vc_ref[...].astype(BF))

    imp = p_c[:, 0:n_q]
    for h in range(1, NSA_H):
        imp = imp + p_c[:, h * n_q:(h + 1) * n_q]
    imp_ref[...] = imp
    imp = imp_ref[pl.ds(0, n_sel, stride=2), :] + imp_ref[pl.ds(1, n_sel, stride=2), :]
    blk = lax.broadcasted_iota(jnp.int32, (n_sel, 1), 0)
    cur = (q0 + lax.broadcasted_iota(jnp.int32, (1, n_q), 1)) // SEL_BLOCK
    forced = (blk == 0) | (blk == cur) | (blk == cur - 1)
    score = jnp.where(blk <= cur, jnp.where(forced, jnp.inf, imp), NEG_INF)
    sel = _select_blocks(score, n_sel)
    mask_ref[...] = _dot_tn(sel.astype(BF), e_ref[...])

    m_ref[...] = jnp.full(m_ref.shape, NEG_INF, F32)
    l_ref[...] = jnp.zeros(l_ref.shape, F32)
    acc_ref[...] = jnp.zeros(acc_ref.shape, F32)

    def sel_tile(t, carry):
        k0 = pl.multiple_of(t * kt, kt)
        s = _dot_nt(q4, ks_ref[pl.ds(k0, kt), :].astype(BF))
        d = qpos_r - (k0 + lax.broadcasted_iota(jnp.int32, (1, kt), 1))
        mk = mask_ref[:, pl.ds(k0, kt)]
        mk = jnp.concatenate([mk] * NSA_H, axis=0)
        s = jnp.where((d >= 0) & (mk > 0.5), s - slope_r * d.astype(F32), NEG_INF)
        m_old = m_ref[...]
        m_new = jnp.maximum(m_old, jnp.max(s, axis=1, keepdims=True))
        m_use = jnp.where(m_new > NEG_INF, m_new, 0.0)
        alpha = jnp.exp(m_old - m_use)
        p = jnp.exp(s - m_use)
        l_ref[...] = alpha * l_ref[...] + jnp.sum(p, axis=1, keepdims=True)
        acc_ref[...] = alpha * acc_ref[...] + _dot(p.astype(BF), vs_ref[pl.ds(k0, kt), :].astype(BF))
        m_ref[...] = m_new
        return carry

    lax.fori_loop(0, (q0 + n_q + kt - 1) // kt, sel_tile, 0)
    l_s = l_ref[...]
    o_s = acc_ref[...] / jnp.where(l_s > 0, l_s, 1.0)

    nw = WINDOW + n_q
    w0 = pl.multiple_of(jnp.maximum(q0 - WINDOW, 0), n_q)
    s_w = _dot_nt(q4, kw_ref[pl.ds(w0, nw), :].astype(BF))
    d_w = qpos_r - (w0 + lax.broadcasted_iota(jnp.int32, (1, nw), 1))
    s_w = jnp.where((d_w >= 0) & (d_w < WINDOW), s_w - slope_r * d_w.astype(F32), NEG_INF)
    o_w = _softmax_segments([(s_w, vw_ref[pl.ds(w0, nw), :].astype(BF))])

    _gated_heads(o_ref, _sigmoid(misc_ref[...]), o_c, o_s, o_w, n_q)


def nsa_prompt(proj, n_b, t_len, k_cmp, v_cmp, expand):
    n_qb = t_len // Q_BLOCK
    n_cmp = t_len // CMP_BLOCK
    n_sel = t_len // SEL_BLOCK
    seq = lambda cb: pl.BlockSpec((t_len, LANES), lambda b, i: (b, cb))
    return pl.pallas_call(
        functools.partial(_nsa_prompt_kernel, t_len=t_len),
        grid=(n_b, n_qb),
        in_specs=[pl.BlockSpec((Q_BLOCK, GROUP_W), lambda b, i: (b * n_qb + i, CB_QC // 4)),
                  pl.BlockSpec((Q_BLOCK, LANES), lambda b, i: (b * n_qb + i, CB_MISC)),
                  pl.BlockSpec((n_cmp, NSA_DH), lambda b, i: (b, 0)),
                  pl.BlockSpec((n_cmp, NSA_DH), lambda b, i: (b, 0)),
                  seq(CB_SELK), seq(CB_SELV), seq(CB_WINK), seq(CB_WINV),
                  pl.BlockSpec((n_sel, t_len), lambda b, i: (0, 0))],
        out_specs=pl.BlockSpec((Q_BLOCK, GROUP_W), lambda b, i: (b * n_qb + i, 0)),
        out_shape=jax.ShapeDtypeStruct((n_b * t_len, GROUP_W), BF),
        scratch_shapes=[pltpu.VMEM((n_cmp, Q_BLOCK), F32),
                        pltpu.VMEM((Q_BLOCK, t_len), F32),
                        pltpu.VMEM((NSA_H * Q_BLOCK, 1), F32),
                        pltpu.VMEM((NSA_H * Q_BLOCK, 1), F32),
                        pltpu.VMEM((NSA_H * Q_BLOCK, NSA_DH), F32)],
        compiler_params=_cparams(("parallel", "arbitrary")),
        name="nsa_prompt",
    )(proj, proj, k_cmp, v_cmp, proj, proj, proj, proj, expand)


def _nsa_sample_kernel(pt_ref, q_ref, misc_ref, ksn_ref, vsn_ref, kwn_ref, vwn_ref, winc_ref, pair_ref, e_ref,
                       *rest, n_pages, page, t_q):
    pages = rest[:n_pages]
    kcs = rest[n_pages:2 * n_pages]
    vcs = rest[2 * n_pages:3 * n_pages]
    o_ref, kc_ref, vc_ref, ks_ref, vs_ref = rest[3 * n_pages:]
    past_len = n_pages * page
    cpp = page // CMP_BLOCK
    n_cmp = past_len // CMP_BLOCK
    n_sel = past_len // SEL_BLOCK + 1
    n_r = NSA_H * t_q
    for j in range(n_pages):
        kc_ref[j * cpp:(j + 1) * cpp, :] = kcs[j][...]
        vc_ref[j * cpp:(j + 1) * cpp, :] = vcs[j][...]
        ks_ref[j * page:(j + 1) * page, :] = pages[j][:, 2, :].astype(BF)
        vs_ref[j * page:(j + 1) * page, :] = pages[j][:, 3, :].astype(BF)

    q4 = _stack_heads(q_ref[...] * (NSA_DH ** -0.5)).astype(BF)
    slope_r = _head_rows_const(t_q, _alibi_slope)
    qpos_r = past_len + lax.rem(lax.broadcasted_iota(jnp.int32, (n_r, 1), 0), t_q)

    cmp_end = lax.broadcasted_iota(jnp.int32, (1, n_cmp), 1) * CMP_BLOCK + (CMP_BLOCK - 1)
    s_c = _dot_nt(q4, kc_ref[...].astype(BF)) - slope_r * (qpos_r - cmp_end).astype(F32)
    s_c = jnp.where(cmp_end <= qpos_r, s_c, NEG_INF)
    m_c = jnp.max(s_c, axis=1, keepdims=True)
    m_c = jnp.where(m_c > NEG_INF, m_c, 0.0)
    e_c = jnp.exp(s_c - m_c)
    den_c = jnp.sum(e_c, axis=1, keepdims=True)
    p_c = e_c / jnp.where(den_c > 0, den_c, 1.0)
    o_c = _dot(p_c.astype(BF), vc_ref[...].astype(BF))

    imp = p_c[0:t_q, :]
    for h in range(1, NSA_H):
        imp = imp + p_c[h * t_q:(h + 1) * t_q, :]
    hi = imp.astype(BF)
    lo = (imp - hi.astype(F32)).astype(BF)
    imp = _dot(hi, pair_ref[...]) + _dot(lo, pair_ref[...])
    blk = lax.broadcasted_iota(jnp.int32, (1, LANES), 1)
    cur = (past_len + lax.broadcasted_iota(jnp.int32, (t_q, 1), 0)) // SEL_BLOCK
    forced = (blk == 0) | (blk == cur) | (blk == cur - 1)
    score = jnp.where(blk <= cur, jnp.where(forced, jnp.inf, imp), NEG_INF)
    cnt = jnp.zeros(score.shape, F32)
    for i in range(n_sel):
        col = score[:, i:i + 1]
        beats = (col > score) | ((col == score) & (blk > i))
        cnt = cnt + jnp.where(beats, 1.0, 0.0)
    sel = jnp.where((cnt < float(SEL_TOP)) & (score > NEG_INF), 1.0, 0.0)
    mask = _dot(sel.astype(BF), e_ref[...])
    mask = jnp.concatenate([mask] * NSA_H, axis=0)

    d_p = qpos_r - lax.broadcasted_iota(jnp.int32, (1, past_len), 1)
    s_p = jnp.where(mask[:, 0:past_len] > 0.5, _dot_nt(q4, ks_ref[...]) - slope_r * d_p.astype(F32), NEG_INF)
    d_n = qpos_r - (past_len + lax.broadcasted_iota(jnp.int32, (1, t_q), 1))
    s_n = _dot_nt(q4, ksn_ref[...].astype(BF)) - slope_r * d_n.astype(F32)
    s_n = jnp.where((d_n >= 0) & (mask[:, past_len:past_len + t_q] > 0.5), s_n, NEG_INF)
    o_s = _softmax_segments([(s_p, vs_ref[...]), (s_n, vsn_ref[...].astype(BF))])

    n_wc = winc_ref.shape[0]
    d_c = qpos_r - (past_len - n_wc + lax.broadcasted_iota(jnp.int32, (1, n_wc), 1))
    s_wc = _dot_nt(q4, winc_ref[:, 0, :].astype(BF)) - slope_r * d_c.astype(F32)
    s_wc = jnp.where(d_c < WINDOW, s_wc, NEG_INF)
    s_wn = _dot_nt(q4, kwn_ref[...].astype(BF)) - slope_r * d_n.astype(F32)
    s_wn = jnp.where(d_n >= 0, s_wn, NEG_INF)
    o_w = _softmax_segments([(s_wc, winc_ref[:, 1, :].astype(BF)), (s_wn, vwn_ref[...].astype(BF))])

    _gated_heads(o_ref, _sigmoid(misc_ref[...]), o_c, o_s, o_w, t_q)


def nsa_sample(proj, row0, n_b, t_q, cache_nsa, layer, page_table, kc_pool, vc_pool, cache_win, pair, expand):
    n_pages = page_table.shape[1]
    page = cache_nsa.shape[2]
    cpp = page // CMP_BLOCK
    past_len = n_pages * page
    n_wc = cache_win.shape[2]
    base = row0 // t_q
    new = lambda cb: pl.BlockSpec((t_q, LANES), lambda b, pt: (base + b, cb))

    def page_spec(j):
        return pl.BlockSpec((None, None, page, 4, NSA_DH), lambda b, pt: (layer, pt[b, j], 0, 0, 0))

    def cmp_spec(j):
        return pl.BlockSpec((None, cpp, NSA_DH), lambda b, pt: (pt[b, j], 0, 0))

    in_specs = [pl.BlockSpec((t_q, GROUP_W), lambda b, pt: (base + b, CB_QC // 4)),
                new(CB_MISC), new(CB_SELK), new(CB_SELV), new(CB_WINK), new(CB_WINV),
                pl.BlockSpec((None, None, n_wc, 2, NSA_DH), lambda b, pt: (layer, b, 0, 0, 0)),
                pl.BlockSpec(pair.shape, lambda b, pt: (0, 0)),
                pl.BlockSpec(expand.shape, lambda b, pt: (0, 0))]
    in_specs += [page_spec(j) for j in range(n_pages)]
    in_specs += [cmp_spec(j) for j in range(n_pages)] * 2
    return pl.pallas_call(
        functools.partial(_nsa_sample_kernel, n_pages=n_pages, page=page, t_q=t_q),
        grid_spec=pltpu.PrefetchScalarGridSpec(
            num_scalar_prefetch=1,
            grid=(n_b,),
            in_specs=in_specs,
            out_specs=pl.BlockSpec((t_q, GROUP_W), lambda b, pt: (b, 0)),
            scratch_shapes=[pltpu.VMEM((past_len // CMP_BLOCK, NSA_DH), F32),
                            pltpu.VMEM((past_len // CMP_BLOCK, NSA_DH), F32),
                            pltpu.VMEM((past_len, NSA_DH), BF),
                            pltpu.VMEM((past_len, NSA_DH), BF)]),
        out_shape=jax.ShapeDtypeStruct((n_b * t_q, GROUP_W), BF),
        compiler_params=_cparams(("arbitrary",)),
        name="nsa_sample",
    )(page_table, proj, proj, proj, proj, proj, proj, cache_win, pair, expand,
      *([cache_nsa] * n_pages), *([kc_pool] * n_pages), *([vc_pool] * n_pages))


def _log_sigmoid(z):
    return jnp.minimum(z, 0.0) - jnp.log(1.0 + jnp.exp(-jnp.abs(z)))


def _cumsum_rows(x, tri):
    hi = x.astype(BF)
    lo = (x - hi.astype(F32)).astype(BF)
    return _dot(tri, hi) + _dot(tri, lo)


def _rows_to_col(row):
    n = row.shape[1]
    eye = lax.broadcasted_iota(jnp.int32, (n, n), 0) == lax.broadcasted_iota(jnp.int32, (n, n), 1)
    return jnp.sum(jnp.where(eye, row, 0.0), axis=1, keepdims=True)


def _gla_out(o, god, ng):
    o = o * lax.rsqrt(jnp.mean(o * o, -1, keepdims=True) + LN_EPS) * ng
    return o * (god * _sigmoid(god))


def _gla_prompt_kernel(q_ref, k_ref, v_ref, god_ref, misc_ref, wa_ref, ba_ref, ng_ref, o_ref, s_ref, st_ref,
                       *, n_chunks):
    c = GLA_CHUNK
    st_ref[...] = jnp.zeros(st_ref.shape, F32)
    r_i = lax.broadcasted_iota(jnp.int32, (c, c), 0)
    c_i = lax.broadcasted_iota(jnp.int32, (c, c), 1)
    causal = r_i >= c_i
    tri = jnp.where(causal, 1.0, 0.0).astype(BF)

    def chunk(n, carry):
        rows = pl.ds(pl.multiple_of(n * c, c), c)
        misc = misc_ref[rows, :].astype(BF)
        for h in range(GLA_H):
            cs = slice(h * LANES, (h + 1) * LANES)
            z = _dot(misc, wa_ref[:, cs]) + ba_ref[:, cs]
            bc = _cumsum_rows(_log_sigmoid(z) / GLA_TAU, tri)
            bl = bc[c - 1:c, :]
            q = q_ref[rows, cs] * (GLA_DK ** -0.5)
            k = k_ref[rows, cs]
            v = v_ref[rows, cs].astype(BF)
            qe = (q * jnp.exp(bc)).astype(BF)
            att = jnp.where(causal, _dot_nt(qe, (k * jnp.exp(-bc)).astype(BF)), 0.0)
            st = st_ref[h]
            o = _dot(qe, st.astype(BF)) + _dot(att.astype(BF), v)
            st_ref[h] = _rows_to_col(jnp.exp(bl)) * st + _dot_tn((k * jnp.exp(bl - bc)).astype(BF), v)
            o_ref[rows, cs] = _gla_out(o, god_ref[rows, cs], ng_ref[...]).astype(o_ref.dtype)
        return carry

    lax.fori_loop(0, n_chunks, chunk, 0)
    for h in range(GLA_H):
        s_ref[h] = st_ref[h, 0:GLA_DK, :]


def gla_prompt(proj, n_b, t_len, wa_pad, ba_pad, norm_g):
    w = GLA_H * LANES
    seq = lambda cb0: pl.BlockSpec((t_len, w), lambda b: (b, cb0 // GLA_H))
    return pl.pallas_call(
        functools.partial(_gla_prompt_kernel, n_chunks=t_len // GLA_CHUNK),
        grid=(n_b,),
        in_specs=[seq(CB_QD), seq(CB_KD), seq(CB_VD), seq(CB_GOD),
                  pl.BlockSpec((t_len, LANES), lambda b: (b, CB_MISC)),
                  pl.BlockSpec((LANES, w), lambda b: (0, 0)),
                  pl.BlockSpec((1, w), lambda b: (0, 0)),
                  pl.BlockSpec((1, GLA_DV), lambda b: (0, 0))],
        out_specs=[pl.BlockSpec((t_len, w), lambda b: (b, 0)),
                   pl.BlockSpec((None, GLA_H, GLA_DK, GLA_DV), lambda b: (b, 0, 0, 0))],
        out_shape=[jax.ShapeDtypeStruct((n_b * t_len, GROUP_W), BF),
                   jax.ShapeDtypeStruct((n_b, GLA_H, GLA_DK, GLA_DV), F32)],
        scratch_shapes=[pltpu.VMEM((GLA_H, LANES, GLA_DV), F32)],
        compiler_params=_cparams(("parallel",)),
        name="gla_prompt",
    )(proj, proj, proj, proj, proj, wa_pad, ba_pad, norm_g.reshape(1, GLA_DV))


def _gla_sample_kernel(q_ref, k_ref, v_ref, god_ref, misc_ref, wa_ref, ba_ref, ng_ref, s0_ref, o_ref, s_ref,
                       *, bb, t_q):
    n_r = bb * t_q
    r_i = lax.broadcasted_iota(jnp.int32, (n_r, n_r), 0)
    c_i = lax.broadcasted_iota(jnp.int32, (n_r, n_r), 1)
    causal = (r_i >= c_i) & (r_i // t_q == c_i // t_q)
    tri = jnp.where(causal, 1.0, 0.0).astype(BF)
    row_b = lax.broadcasted_iota(jnp.int32, (n_r, 1), 0) // t_q

    z = _dot(misc_ref[...].astype(BF), wa_ref[...]) + ba_ref[...]
    bc = _cumsum_rows(_log_sigmoid(z) / GLA_TAU, tri)
    q = q_ref[...] * (GLA_DK ** -0.5)
    k = k_ref[...]
    v = v_ref[...].astype(BF)
    qe = q * jnp.exp(bc)
    att = jnp.where(causal, _dot_nt(qe.astype(BF), (k * jnp.exp(-bc)).astype(BF)), 0.0)
    o = _dot(att.astype(BF), v)
    for b in range(bb):
        mine = row_b == b
        bl = bc[(b + 1) * t_q - 1:(b + 1) * t_q, :]
        s0 = s0_ref[b, 0]
        o = o + _dot(jnp.where(mine, qe, 0.0)[:, 0:GLA_DK].astype(BF), s0.astype(BF))
        k2 = jnp.where(mine, k * jnp.exp(bl - bc), 0.0).astype(BF)
        upd = _dot_tn(k2, v)
        s_ref[b, 0] = _rows_to_col(jnp.exp(bl))[0:GLA_DK, :] * s0 + upd[0:GLA_DK, :]
    o_ref[...] = _gla_out(o, god_ref[...], ng_ref[...]).astype(o_ref.dtype)


def gla_sample(proj, row0, n_b, t_q, wa_pad, ba_pad, norm_g, s0, s0_b0, bb):
    rb = bb * t_q
    base = row0 // rb
    s_i0 = s0_b0 // bb
    seq = lambda cb0: pl.BlockSpec((rb, LANES), lambda i, h: (base + i, cb0 + h))
    return pl.pallas_call(
        functools.partial(_gla_sample_kernel, bb=bb, t_q=t_q),
        grid=(n_b // bb, GLA_H),
        in_specs=[seq(CB_QD), seq(CB_KD), seq(CB_VD), seq(CB_GOD),
                  pl.BlockSpec((rb, LANES), lambda i, h: (base + i, CB_MISC)),
                  pl.BlockSpec((LANES, LANES), lambda i, h: (0, h)),
                  pl.BlockSpec((1, LANES), lambda i, h: (0, h)),
                  pl.BlockSpec((1, GLA_DV), lambda i, h: (0, 0)),
                  pl.BlockSpec((bb, 1, GLA_DK, GLA_DV), lambda i, h: (s_i0 + i, h, 0, 0))],
        out_specs=[pl.BlockSpec((rb, GLA_DV), lambda i, h: (i, h)),
                   pl.BlockSpec((bb, 1, GLA_DK, GLA_DV), lambda i, h: (i, h, 0, 0))],
        out_shape=[jax.ShapeDtypeStruct((n_b * t_q, GROUP_W), BF),
                   jax.ShapeDtypeStruct((n_b, GLA_H, GLA_DK, GLA_DV), F32)],
        compiler_params=_cparams(("parallel", "parallel")),
        name="gla_sample",
    )(proj, proj, proj, proj, proj, wa_pad, ba_pad, norm_g.reshape(1, GLA_DV), s0)


def _mem_attn_kernel(q_ref, kv_ref, o_ref):
    q = q_ref[...] * (MEM_DH ** -0.5)
    for h in range(MEM_H):
        cs = slice(h * MEM_DH, (h + 1) * MEM_DH)
        s = _dot_nt(q[:, cs].astype(BF), kv_ref[:, 0, h, :].astype(BF))
        e = jnp.exp(s - jnp.max(s, axis=1, keepdims=True))
        p = e / jnp.sum(e, axis=1, keepdims=True)
        o_ref[:, cs] = _dot(p.astype(BF), kv_ref[:, 1, h, :].astype(BF)).astype(o_ref.dtype)


def mem_attention(qm, row0, n_b, t_len, kv, layer, tq):
    n_t = t_len // tq
    base = row0 // tq
    w = MEM_H * MEM_DH
    return pl.pallas_call(
        _mem_attn_kernel,
        grid=(n_b, n_t),
        in_specs=[pl.BlockSpec((tq, w), lambda b, i: (base + b * n_t + i, 0)),
                  pl.BlockSpec((None, None, MEM_LEN, 2, MEM_H, MEM_DH), lambda b, i: (layer, b, 0, 0, 0, 0))],
        out_specs=pl.BlockSpec((tq, w), lambda b, i: (b * n_t + i, 0)),
        out_shape=jax.ShapeDtypeStruct((n_b * t_len, w), BF),
        compiler_params=_cparams(("parallel", "parallel")),
        name="mem_attention",
    )(qm, kv)


def _router_kernel(x_ref, w_ref, b_ref, tri_ref, idx_ref, gate_ref, rank_ref, cnt_ref, run_ref):
    @pl.when(pl.program_id(0) == 0)
    def _():
        run_ref[...] = jnp.zeros(run_ref.shape, F32)

    logits = jnp.dot(x_ref[...], w_ref[...], precision=lax.Precision.HIGHEST,
                     preferred_element_type=F32) + b_ref[...]
    lane = lax.broadcasted_iota(jnp.int32, logits.shape, 1).astype(F32)
    vals = jnp.where(lane < N_EXPERTS, logits, NEG_INF)
    idx_out = jnp.zeros(logits.shape, F32)
    e_out = jnp.zeros(logits.shape, F32)
    den = None
    top0 = None
    onehots = []
    for k in range(TOP_K):
        m = jnp.max(vals, axis=1, keepdims=True)
        idx = jnp.min(jnp.where(vals == m, lane, float(LANES)), axis=1, keepdims=True)
        if k == 0:
            top0 = m
        e = jnp.exp(m - top0)
        den = e if den is None else den + e
        idx_out = jnp.where(lane == k, idx, idx_out)
        e_out = jnp.where(lane == k, e, e_out)
        onehots.append(jnp.where(lane == idx, 1.0, 0.0))
        vals = jnp.where(lane == idx, NEG_INF, vals)
    idx_ref[...] = idx_out.astype(jnp.int32)
    gate_ref[...] = e_out / den

    chosen = onehots[0] + onehots[1] + onehots[2] + onehots[3]
    before = _dot(tri_ref[...], chosen.astype(BF)) + run_ref[...]
    rank_out = jnp.zeros(logits.shape, F32)
    for k in range(TOP_K):
        r = jnp.sum(before * onehots[k], axis=1, keepdims=True)
        rank_out = jnp.where(lane == k, r, rank_out)
    rank_ref[...] = rank_out.astype(jnp.int32)
    run_ref[...] = run_ref[...] + jnp.sum(chosen, axis=0, keepdims=True)
    cnt_ref[...] = run_ref[...]


def moe_router(x, w_pad, b_pad, tm):
    m, d = x.shape
    assert m % tm == 0, (m, tm)
    tri = jnp.asarray(np.tril(np.ones((tm, tm), np.float32), -1), BF)
    row = lambda dt: jax.ShapeDtypeStruct((m, LANES), dt)
    return pl.pallas_call(
        _router_kernel,
        grid=(m // tm,),
        in_specs=[pl.BlockSpec((tm, d), lambda i: (i, 0)),
                  pl.BlockSpec((d, LANES), lambda i: (0, 0)),
                  pl.BlockSpec((1, LANES), lambda i: (0, 0)),
                  pl.BlockSpec((tm, tm), lambda i: (0, 0))],
        out_specs=[pl.BlockSpec((tm, LANES), lambda i: (i, 0)),
                   pl.BlockSpec((tm, LANES), lambda i: (i, 0)),
                   pl.BlockSpec((tm, LANES), lambda i: (i, 0)),
                   pl.BlockSpec((1, LANES), lambda i: (0, 0))],
        out_shape=[row(jnp.int32), row(F32), row(jnp.int32), jax.ShapeDtypeStruct((1, LANES), F32)],
        scratch_shapes=[pltpu.VMEM((1, LANES), F32)],
        compiler_params=_cparams(("arbitrary",)),
        name="moe_router",
    )(x, w_pad, b_pad, tri)


def _slot_ids_to_smem(slot_ref, ids_ref, sem):
    cp = pltpu.make_async_copy(slot_ref.at[0], ids_ref, sem)
    cp.start()
    cp.wait()


def _dispatch_kernel(slot_ref, x_hbm, xg_in, xg_out, ids_ref, sem_ids, sem, *, tn):
    del xg_in
    _slot_ids_to_smem(slot_ref, ids_ref, sem_ids)
    n0 = pl.program_id(0) * tn

    def issue(t, carry):
        for k in range(TOP_K):
            pltpu.make_async_copy(x_hbm.at[pl.ds(n0 + t, 1), :],
                                  xg_out.at[pl.ds(ids_ref[0, t * TOP_K + k], 1), :], sem).start()
        return carry

    lax.fori_loop(0, tn, issue, 0)
    pltpu.make_async_copy(x_hbm.at[pl.ds(0, TOP_K * tn), :], xg_out.at[pl.ds(0, TOP_K * tn), :], sem).wait()


def moe_dispatch(x, slots, cap):
    n, d = x.shape
    tn = MOE_TOK
    assert n % tn == 0, (n, tn)
    return pl.pallas_call(
        functools.partial(_dispatch_kernel, tn=tn),
        grid=(n // tn,),
        in_specs=[pl.BlockSpec((1, 1, TOP_K * tn), lambda i: (i, 0, 0)),
                  pl.BlockSpec(memory_space=pl.ANY),
                  pl.BlockSpec(memory_space=pl.ANY)],
        out_specs=pl.BlockSpec(memory_space=pl.ANY),
        out_shape=jax.ShapeDtypeStruct((cap, d), x.dtype),
        scratch_shapes=[pltpu.SMEM((1, TOP_K * tn), jnp.int32),
                        pltpu.SemaphoreType.DMA(()),
                        pltpu.SemaphoreType.DMA(())],
        input_output_aliases={2: 0},
        compiler_params=_cparams(("arbitrary",)),
        name="moe_dispatch",
    )(slots.reshape(n // tn, 1, TOP_K * tn), x, jnp.zeros((cap, d), x.dtype))


def _moe_up_kernel(be_ref, nu_ref, x_ref, wg_ref, wu_ref, bg_ref, bu_ref, o_ref, wgb_ref, wub_ref):
    i = pl.program_id(1)
    changed = jnp.logical_or(i == 0, be_ref[i] != be_ref[jnp.maximum(i - 1, 0)])

    @pl.when(changed)
    def _():
        wgb_ref[...] = wg_ref[...].astype(BF)
        wub_ref[...] = wu_ref[...].astype(BF)

    @pl.when(i < nu_ref[0])
    def _():
        x = x_ref[...].astype(BF)
        g = jnp.minimum(_dot(x, wgb_ref[...]) + bg_ref[...], SWIGLU_LIMIT)
        u = jnp.clip(_dot(x, wub_ref[...]) + bu_ref[...], -SWIGLU_LIMIT, SWIGLU_LIMIT)
        o_ref[...] = ((u + 1.0) * g * _sigmoid(SWIGLU_ALPHA * g)).astype(o_ref.dtype)

    @pl.when(i >= nu_ref[0])
    def _():
        o_ref[...] = jnp.zeros(o_ref.shape, o_ref.dtype)


def moe_up(xg, blk_expert, n_used, w1, b1):
    cap, d = xg.shape
    n_blk = cap // MOE_TM
    n_j = D_FF // MOE_TN
    return pl.pallas_call(
        _moe_up_kernel,
        grid_spec=pltpu.PrefetchScalarGridSpec(
            num_scalar_prefetch=2,
            grid=(n_j, n_blk),
            in_specs=[pl.BlockSpec((MOE_TM, d), lambda j, i, be, nu: (jnp.minimum(i, nu[0] - 1), 0)),
                      pl.BlockSpec((None, d, MOE_TN), lambda j, i, be, nu: (be[i], 0, j)),
                      pl.BlockSpec((None, d, MOE_TN), lambda j, i, be, nu: (be[i], 0, n_j + j)),
                      pl.BlockSpec((None, 1, MOE_TN), lambda j, i, be, nu: (be[i], 0, j)),
                      pl.BlockSpec((None, 1, MOE_TN), lambda j, i, be, nu: (be[i], 0, n_j + j))],
            out_specs=pl.BlockSpec((MOE_TM, MOE_TN), lambda j, i, be, nu: (i, j)),
            scratch_shapes=[pltpu.VMEM((d, MOE_TN), BF), pltpu.VMEM((d, MOE_TN), BF)]),
        out_shape=jax.ShapeDtypeStruct((cap, D_FF), BF),
        compiler_params=_cparams(("arbitrary", "arbitrary")),
        name="moe_up",
    )(blk_expert, n_used, xg, w1, w1, b1, b1)


def _moe_down_kernel(be_ref, nu_ref, a_ref, w_ref, b_ref, o_ref, wb_ref):
    i = pl.program_id(1)
    changed = jnp.logical_or(i == 0, be_ref[i] != be_ref[jnp.maximum(i - 1, 0)])

    @pl.when(changed)
    def _():
        wb_ref[...] = w_ref[...].astype(BF)

    @pl.when(i < nu_ref[0])
    def _():
        o_ref[...] = _dot(a_ref[...], wb_ref[...]) + b_ref[...]

    @pl.when(i >= nu_ref[0])
    def _():
        o_ref[...] = jnp.zeros(o_ref.shape, o_ref.dtype)


def moe_down(act, blk_expert, n_used, w2, b2):
    cap, f = act.shape
    d = w2.shape[2]
    n_blk = cap // MOE_TM
    return pl.pallas_call(
        _moe_down_kernel,
        grid_spec=pltpu.PrefetchScalarGridSpec(
            num_scalar_prefetch=2,
            grid=(d // MOE_TN, n_blk),
            in_specs=[pl.BlockSpec((MOE_TM, f), lambda j, i, be, nu: (jnp.minimum(i, nu[0] - 1), 0)),
                      pl.BlockSpec((None, f, MOE_TN), lambda j, i, be, nu: (be[i], 0, j)),
                      pl.BlockSpec((None, 1, MOE_TN), lambda j, i, be, nu: (be[i], 0, j))],
            out_specs=pl.BlockSpec((MOE_TM, MOE_TN), lambda j, i, be, nu: (i, j)),
            scratch_shapes=[pltpu.VMEM((f, MOE_TN), BF)]),
        out_shape=jax.ShapeDtypeStruct((cap, d), F32),
        compiler_params=_cparams(("arbitrary", "arbitrary")),
        name="moe_down",
    )(blk_expert, n_used, act, w2, b2)


def _combine_ln_kernel(slot_ref, yb_hbm, gate_ref, x_ref, g_ref, b_ref, o_ref, ids_ref, buf_ref, sem_ids, sem, *, tn):
    _slot_ids_to_smem(slot_ref, ids_ref, sem_ids)

    def issue(t, carry):
        for k in range(TOP_K):
            pltpu.make_async_copy(yb_hbm.at[pl.ds(ids_ref[0, t * TOP_K + k], 1), :],
                                  buf_ref.at[k, pl.ds(t, 1), :], sem).start()
        return carry

    lax.fori_loop(0, tn, issue, 0)
    for k in range(TOP_K):
        pltpu.make_async_copy(yb_hbm.at[pl.ds(0, tn), :], buf_ref.at[k], sem).wait()
    gate = gate_ref[...]
    y = gate[:, 0:1] * buf_ref[0]
    for k in range(1, TOP_K):
        y = y + gate[:, k:k + 1] * buf_ref[k]
    o_ref[...] = _layer_norm(DN_ALPHA * x_ref[...] + y, g_ref[...], b_ref[...])


def moe_combine_ln(yb, slots, gate, x, g, b):
    n, d = x.shape
    tn = MOE_TOK
    assert n % tn == 0, (n, tn)
    return pl.pallas_call(
        functools.partial(_combine_ln_kernel, tn=tn),
        grid=(n // tn,),
        in_specs=[pl.BlockSpec((1, 1, TOP_K * tn), lambda i: (i, 0, 0)),
                  pl.BlockSpec(memory_space=pl.ANY),
                  pl.BlockSpec((tn, LANES), lambda i: (i, 0)),
                  pl.BlockSpec((tn, d), lambda i: (i, 0)),
                  pl.BlockSpec((1, d), lambda i: (0, 0)),
                  pl.BlockSpec((1, d), lambda i: (0, 0))],
        out_specs=pl.BlockSpec((tn, d), lambda i: (i, 0)),
        out_shape=jax.ShapeDtypeStruct((n, d), F32),
        scratch_shapes=[pltpu.SMEM((1, TOP_K * tn), jnp.int32),
                        pltpu.VMEM((TOP_K, tn, d), F32),
                        pltpu.SemaphoreType.DMA(()),
                        pltpu.SemaphoreType.DMA(())],
        compiler_params=_cparams(("arbitrary",)),
        name="moe_combine_ln",
    )(slots.reshape(n // tn, 1, TOP_K * tn), yb, gate, x, g.reshape(1, d), b.reshape(1, d))


def moe_ffn_ln(x, router_w, router_b, w1, b1, w2, b2, e0, ln_g, ln_b):
    n, d = x.shape
    w_pad = jnp.pad(router_w, ((0, 0), (0, LANES - N_EXPERTS)))
    b_pad = jnp.pad(router_b, (0, LANES - N_EXPERTS)).reshape(1, LANES)
    idx, gate, rank, cnt = moe_router(x, w_pad, b_pad, 512)

    counts = cnt[0, :N_EXPERTS].astype(jnp.int32)
    padded = (counts + MOE_TM - 1) // MOE_TM * MOE_TM
    pad_end = jnp.cumsum(padded)
    pad_start = pad_end - padded
    onehot = idx[:, :TOP_K, None] == jnp.arange(N_EXPERTS, dtype=jnp.int32)
    slots = (jnp.sum(jnp.where(onehot, pad_start, 0), axis=-1) + rank[:, :TOP_K]).reshape(-1)
    n_blk = -(-(n * TOP_K + N_EXPERTS * (MOE_TM - 1)) // MOE_TM)
    blk_row0 = jnp.arange(n_blk, dtype=jnp.int32) * MOE_TM
    blk_expert = e0 + jnp.minimum(jnp.sum(pad_end[None, :] <= blk_row0[:, None], axis=1), N_EXPERTS - 1)
    n_used = (pad_end[-1] // MOE_TM).astype(jnp.int32).reshape(1)

    xg = moe_dispatch(x, slots, n_blk * MOE_TM)
    act = moe_up(xg, blk_expert.astype(jnp.int32), n_used, w1, b1)
    yb = moe_down(act, blk_expert.astype(jnp.int32), n_used, w2, b2)
    return moe_combine_ln(yb, slots, gate, x, ln_g, ln_b)


def _pad_heads(w):
    lead = w.shape[:-1]
    w = w.reshape(lead + (GLA_H, GLA_DK))
    w = jnp.pad(w, [(0, 0)] * len(lead) + [(0, 0), (0, LANES - GLA_DK)])
    return w.reshape(lead + (GLA_H * LANES,))


def _relayout_w_in(w):
    c = np.cumsum((GROUP_W, 2 * GROUP_W, GROUP_W, 6 * NSA_DH, 3 * NSA_H, GLA_H * GLA_DK, GLA_H * GLA_DK, GROUP_W,
                   GLA_RANK, GROUP_W))
    gc, qd, kd, vd, lrd, god = (w[:, c[3]:c[4]], w[:, c[4]:c[5]], w[:, c[5]:c[6]], w[:, c[6]:c[7]],
                                w[:, c[7]:c[8]], w[:, c[8]:c[9]])
    misc = jnp.pad(jnp.concatenate([gc, lrd], axis=1), ((0, 0), (0, LANES - 3 * NSA_H - GLA_RANK)))
    out = jnp.concatenate([w[:, :c[3]], misc, jnp.zeros((w.shape[0], LANES), w.dtype),
                           _pad_heads(qd), _pad_heads(kd), vd, god], axis=1)
    assert out.shape[1] == D_PROJ
    return out.astype(BF)


def kernel(x_prompt, x_sample, cache_pool, cache_nsa_kv, cache_win_kv, state_gla, cache_mem_kv, page_table,
           mem_prompt, ln_in_g, ln_in_b, w_in, w_out, pool_w, pool_scale, gmlp_ln_g, gmlp_ln_b, gmlp_ws, gmlp_bs,
           nsa_phi, gla_wa, gla_ba, gla_norm_g, ln1_g, ln1_b, mem_wq, mem_wkv, mem_wo, ln2_g, ln2_b,
           router_w, router_b, moe_w1, moe_b1, moe_w2, moe_b2, ln3_g, ln3_b):
    n_bp, t_p, d = x_prompt.shape
    n_bs, t_s, _ = x_sample.shape
    n_p = n_bp * t_p
    n_s = n_bs * t_s
    n_pool = cache_nsa_kv.shape[1]
    page = cache_nsa_kv.shape[2]
    past_len = page_table.shape[1] * page
    n_wc = cache_win_kv.shape[2]
    assert t_p % 512 == 0 and n_s % 512 == 0 and t_s == 8 and past_len % SEL_BLOCK == 0 and n_wc == WINDOW

    x = jnp.concatenate([ln_rows(x_prompt.reshape(n_p, d), ln_in_g, ln_in_b, 512),
                         ln_rows(x_sample.reshape(n_s, d), ln_in_g, ln_in_b, 512)], axis=0)
    n_tok = n_p + n_s
    tm_big = 1024 if n_tok % 1024 == 0 else 512
    cpp = page // CMP_BLOCK
    cache_blocks = cache_nsa_kv.reshape(DEPTH * n_pool * cpp, CMP_BLOCK, 4, NSA_DH)
    state0 = state_gla.reshape(DEPTH * n_bs, GLA_H, GLA_DK, GLA_DV)
    w1_all = moe_w1.reshape(DEPTH * N_EXPERTS, d, 2 * D_FF)
    b1_all = moe_b1.reshape(DEPTH * N_EXPERTS, 1, 2 * D_FF)
    w2_all = moe_w2.reshape(DEPTH * N_EXPERTS, D_FF, d)
    b2_all = moe_b2.reshape(DEPTH * N_EXPERTS, 1, d)

    expand_p = (np.arange(t_p)[None, :] // SEL_BLOCK == np.arange(t_p // SEL_BLOCK)[:, None])
    expand_p = jnp.asarray(expand_p, BF)
    expand_s = (np.arange(past_len + LANES)[None, :] // SEL_BLOCK == np.arange(LANES)[:, None])
    expand_s = jnp.asarray(expand_s, BF)
    pair_s = jnp.asarray(np.arange(past_len // CMP_BLOCK)[:, None] // 2 == np.arange(LANES)[None, :], BF)
    bb_s = LANES // t_s
    eye_bb = jnp.eye(bb_s, dtype=F32)

    outs = {k: [] for k in ("pool_p", "nsa_p", "win_p", "gla_p", "mem_p", "pool_s", "chunk_s", "nsa_s", "win_s",
                            "gla_s")}
    for l in range(DEPTH):
        proj = matmul(x, _relayout_w_in(w_in[l]), tm_big, 512)

        pre_p = jnp.zeros((n_bp, POOL_PRE, GROUP_W), F32)
        pre_s = jnp.pad(cache_pool[l], ((0, 0), (1, 0), (0, 0)))
        ya = jnp.concatenate([
            pool_mixer(proj, 0, n_bp, t_p, pre_p, pool_w[l], pool_scale[l], 0, 1, 512),
            pool_mixer(proj, n_p, n_bs, t_s, pre_s, pool_w[l], pool_scale[l], past_len, bb_s, t_s)], axis=0)

        tril = jnp.tril(jnp.ones((GMLP_CHUNK, GMLP_CHUNK), F32))
        w_mix_p = (gmlp_ws[l] * tril).astype(BF)
        bias_p = jnp.repeat(gmlp_bs[l].T, LANES, axis=1)
        ws_s = gmlp_ws[l][:, :t_s, :t_s] * tril[:t_s, :t_s]
        w_mix_s = jnp.einsum('ab,gts->gatbs', eye_bb, ws_s).reshape(4, LANES, LANES).astype(BF)
        bias_s = jnp.tile(jnp.repeat(gmlp_bs[l][:, :t_s].T, LANES, axis=1), (bb_s, 1))
        yb_p, _ = gmlp_mixer(proj, 0, n_p, gmlp_ln_g[l], gmlp_ln_b[l], w_mix_p, bias_p, 4)
        yb_s, v_s = gmlp_mixer(proj, n_p, n_s, gmlp_ln_g[l], gmlp_ln_b[l], w_mix_s, bias_s, 4)
        yb = jnp.concatenate([yb_p, yb_s], axis=0)

        phi2 = nsa_phi[l].astype(BF).reshape(2, CMP_BLOCK // 2, 2 * NSA_DH, NSA_DH)
        kc_p, vc_p = cmp_project(proj, n_p // CMP_BLOCK, phi2[0], phi2[1], t_p // CMP_BLOCK)
        yc_p = nsa_prompt(proj, n_bp, t_p, kc_p, vc_p, expand_p)
        kc_pool, vc_pool = cmp_pool(cache_blocks, l * n_pool * cpp, n_pool * cpp, phi2[0], phi2[1], 128)
        yc_s = nsa_sample(proj, n_p, n_bs, t_s, cache_nsa_kv, l, page_table,
                          kc_pool.reshape(n_pool, cpp, NSA_DH), vc_pool.reshape(n_pool, cpp, NSA_DH),
                          cache_win_kv, pair_s, expand_s)
        yc = jnp.concatenate([yc_p, yc_s], axis=0)

        wa_pad = jnp.zeros((LANES, GLA_H * LANES), F32).at[MISC_LR0:MISC_LR0 + GLA_RANK].set(_pad_heads(gla_wa[l]))
        wa_pad = wa_pad.astype(BF)
        ba_pad = _pad_heads(gla_ba[l]).reshape(1, GLA_H * LANES)
        yd_p, st_p = gla_prompt(proj, n_bp, t_p, wa_pad, ba_pad, gla_norm_g[l])
        yd_s, st_s = gla_sample(proj, n_p, n_bs, t_s, wa_pad, ba_pad, gla_norm_g[l], state0, l * n_bs, bb_s)
        yd = jnp.concatenate([yd_p, yd_s], axis=0)

        x = matmul_res_ln([ya, yb, yc, yd], w_out[l].astype(BF), x, ln1_g[l], ln1_b[l], 256)

        qm = matmul(x, mem_wq[l].astype(BF), tm_big, 512)
        mem_kv_p = matmul(mem_prompt.reshape(n_bp * MEM_LEN, d), mem_wkv[l].astype(BF), 512, 512)
        mem_kv_p = mem_kv_p.reshape(1, n_bp, MEM_LEN, 2, MEM_H, MEM_DH)
        om = jnp.concatenate([
            mem_attention(qm, 0, n_bp, t_p, mem_kv_p, 0, 512),
            mem_attention(qm, n_p, n_bs, t_s, cache_mem_kv, l, t_s)], axis=0)
        x = matmul_res_ln([om], mem_wo[l].astype(BF), x, ln2_g[l], ln2_b[l], 256)

        x = moe_ffn_ln(x, router_w[l], router_b[l], w1_all, b1_all, w2_all, b2_all, l * N_EXPERTS,
                       ln3_g[l], ln3_b[l])

        seqs_p = lambda c0, c1: proj[:n_p, c0 * LANES:c1 * LANES].reshape(n_bp, t_p, (c1 - c0) * LANES)
        seqs_s = lambda c0, c1: proj[n_p:, c0 * LANES:c1 * LANES].reshape(n_bs, t_s, (c1 - c0) * LANES)
        outs["pool_p"].append(seqs_p(CB_XA, CB_U)[:, t_p - POOL_BUF:])
        outs["nsa_p"].append(seqs_p(CB_CMPK, CB_WINK).reshape(n_bp, t_p, 4, NSA_DH))
        outs["win_p"].append(seqs_p(CB_WINK, CB_MISC)[:, t_p - min(WINDOW, t_p):]
                             .reshape(n_bp, min(WINDOW, t_p), 2, NSA_DH))
        outs["gla_p"].append(st_p)
        outs["mem_p"].append(mem_kv_p[0])
        outs["pool_s"].append(seqs_s(CB_XA, CB_U))
        outs["chunk_s"].append(v_s.reshape(n_bs, t_s, GROUP_W))
        outs["nsa_s"].append(seqs_s(CB_CMPK, CB_WINK).reshape(n_bs, t_s, 4, NSA_DH))
        outs["win_s"].append(seqs_s(CB_WINK, CB_MISC).reshape(n_bs, t_s, 2, NSA_DH))
        outs["gla_s"].append(st_s)

    st = lambda k: jnp.stack(outs[k])
    return (x[:n_p].reshape(n_bp, t_p, d), x[n_p:].reshape(n_bs, t_s, d), st("pool_p"), st("nsa_p"), st("win_p"),
            st("gla_p"), st("mem_p"), st("pool_s"), st("chunk_s"), st("nsa_s"), st("win_s"), st("gla_s"))
```

```python
import functools

import jax
import jax.numpy as jnp
import numpy as np
from jax import lax
from jax.experimental import pallas as pl
from jax.experimental.pallas import tpu as pltpu

BF = jnp.bfloat16
F32 = jnp.float32

D_MODEL = 2048
DEPTH = 2
GROUP_W = 512
LANES = 128
POOL_WINDOWS = (2, 4, 8, 16)
POOL_BUF = 15
POOL_PRE = 16
GMLP_CHUNK = 128
NSA_DH = 128
NSA_H = 4
CMP_BLOCK = 32
SEL_BLOCK = 64
SEL_TOP = 16
WINDOW = 512
Q_BLOCK = 128
GLA_H = 4
GLA_DK = 64
GLA_DV = 128
GLA_RANK = 16
GLA_TAU = 16.0
GLA_CHUNK = 64
MEM_LEN = 256
MEM_H = 4
MEM_DH = 128
N_EXPERTS = 32
TOP_K = 4
D_FF = 2048
SWIGLU_LIMIT = 7.0
SWIGLU_ALPHA = 1.702
DN_ALPHA = (2.0 * DEPTH) ** 0.25
LN_EPS = 1e-5
NEG_INF = float("-inf")

CB_XA, CB_U, CB_V, CB_QC = 0, 4, 8, 12
CB_CMPK, CB_CMPV, CB_SELK, CB_SELV, CB_WINK, CB_WINV = 16, 17, 18, 19, 20, 21
CB_MISC, CB_QD, CB_KD, CB_VD, CB_GOD = 22, 24, 28, 32, 36
N_CB = 40
D_PROJ = N_CB * LANES
MISC_GATE0 = 0
MISC_LR0 = 12

MOE_TM = 512
MOE_TN = 512
MOE_TN_DOWN = 1024
MOE_TOK = 256
VMEM_LIMIT = 56 * 1024 * 1024


def _cparams(sem):
    return pltpu.CompilerParams(dimension_semantics=sem, vmem_limit_bytes=VMEM_LIMIT)


def _layer_norm(x, g, b):
    mu = jnp.mean(x, -1, keepdims=True)
    xc = x - mu
    var = jnp.mean(xc * xc, -1, keepdims=True)
    return xc * lax.rsqrt(var + LN_EPS) * g + b


def _dot(a, b):
    return jnp.dot(a, b, preferred_element_type=F32)


def _dot_nt(a, b):
    return lax.dot_general(a, b, (((1,), (1,)), ((), ())), preferred_element_type=F32)


def _dot_tn(a, b):
    return lax.dot_general(a, b, (((0,), (0,)), ((), ())), preferred_element_type=F32)


def _sigmoid(x):
    return 1.0 / (1.0 + jnp.exp(-x))


def _ln_kernel(x_ref, g_ref, b_ref, o_ref):
    o_ref[...] = _layer_norm(x_ref[...], g_ref[...], b_ref[...])


def ln_rows(x, g, b, tm):
    m, d = x.shape
    assert m % tm == 0, (m, tm)
    return pl.pallas_call(
        _ln_kernel,
        grid=(m // tm,),
        in_specs=[pl.BlockSpec((tm, d), lambda i: (i, 0)),
                  pl.BlockSpec((1, d), lambda i: (0, 0)),
                  pl.BlockSpec((1, d), lambda i: (0, 0))],
        out_specs=pl.BlockSpec((tm, d), lambda i: (i, 0)),
        out_shape=jax.ShapeDtypeStruct((m, d), F32),
        compiler_params=_cparams(("parallel",)),
        name="ln_rows",
    )(x, g.reshape(1, d), b.reshape(1, d))


def _mm_kernel(x_ref, w_ref, o_ref, xb_ref):
    @pl.when(pl.program_id(1) == 0)
    def _():
        xb_ref[...] = x_ref[...].astype(BF)

    o_ref[...] = _dot(xb_ref[...], w_ref[...]).astype(o_ref.dtype)


def matmul(x, w, tm, tn, out_dtype=F32):
    m, k = x.shape
    n = w.shape[1]
    assert m % tm == 0 and n % tn == 0, (m, tm, n, tn)
    return pl.pallas_call(
        _mm_kernel,
        grid=(m // tm, n // tn),
        in_specs=[pl.BlockSpec((tm, k), lambda i, j: (i, 0)),
                  pl.BlockSpec((k, tn), lambda i, j: (0, j))],
        out_specs=pl.BlockSpec((tm, tn), lambda i, j: (i, j)),
        out_shape=jax.ShapeDtypeStruct((m, n), out_dtype),
        scratch_shapes=[pltpu.VMEM((tm, k), BF)],
        compiler_params=_cparams(("parallel", "arbitrary")),
        name="matmul",
    )(x, w)


def _mm_res_ln_kernel(*refs, n_in):
    xs = refs[:n_in]
    w_ref, r_ref, g_ref, b_ref, o_ref = refs[n_in:]
    acc = None
    k0 = 0
    for x_ref in xs:
        kk = x_ref.shape[1]
        part = _dot(x_ref[...].astype(BF), w_ref[k0:k0 + kk, :])
        acc = part if acc is None else acc + part
        k0 += kk
    o_ref[...] = _layer_norm(DN_ALPHA * r_ref[...] + acc, g_ref[...], b_ref[...])


def matmul_res_ln(xs, w, resid, g, b, tm):
    m, d = resid.shape
    assert m % tm == 0, (m, tm)
    in_specs = [pl.BlockSpec((tm, x.shape[1]), lambda i: (i, 0)) for x in xs]
    in_specs += [pl.BlockSpec(w.shape, lambda i: (0, 0)),
                 pl.BlockSpec((tm, d), lambda i: (i, 0)),
                 pl.BlockSpec((1, d), lambda i: (0, 0)),
                 pl.BlockSpec((1, d), lambda i: (0, 0))]
    return pl.pallas_call(
        functools.partial(_mm_res_ln_kernel, n_in=len(xs)),
        grid=(m // tm,),
        in_specs=in_specs,
        out_specs=pl.BlockSpec((tm, d), lambda i: (i, 0)),
        out_shape=jax.ShapeDtypeStruct((m, d), F32),
        compiler_params=_cparams(("parallel",)),
        name="matmul_res_ln",
    )(*xs, w, resid, g.reshape(1, d), b.reshape(1, d))


def _pool_kernel(x_ref, pre_ref, w_ref, sc_ref, o_ref, ext_ref, *, bb, tt, n_tiles, t0):
    j = pl.program_id(1)

    @pl.when(j == 0)
    def _():
        ext_ref[:, 0:POOL_PRE, :] = pre_ref[...]

    x = x_ref[...].reshape(bb, tt, GROUP_W)
    ext_ref[:, POOL_PRE:, :] = x
    t_idx = j * tt + lax.broadcasted_iota(jnp.int32, (1, tt, LANES), 1)
    for g, win in enumerate(POOL_WINDOWS):
        cs = slice(g * LANES, (g + 1) * LANES)
        xg = ext_ref[:, POOL_PRE:POOL_PRE + tt, cs]
        s = xg
        for k in range(1, win):
            s = s + ext_ref[:, POOL_PRE - k:POOL_PRE - k + tt, cs]
        cnt = jnp.minimum(win, t0 + 1 + t_idx).astype(F32)
        d = s / cnt - xg
        y = _dot(d.reshape(bb * tt, LANES).astype(BF), w_ref[g])
        o_ref[:, cs] = (y * sc_ref[:, cs]).astype(o_ref.dtype)
    if n_tiles > 1:
        ext_ref[:, 0:POOL_PRE, :] = ext_ref[:, tt:tt + POOL_PRE, :]


def pool_mixer(proj, row0, n_b, t_len, prefix, pool_w, pool_scale, t0, bb, tt):
    n_tiles = t_len // tt
    rb = bb * tt
    base = row0 // rb
    return pl.pallas_call(
        functools.partial(_pool_kernel, bb=bb, tt=tt, n_tiles=n_tiles, t0=t0),
        grid=(n_b // bb, n_tiles),
        in_specs=[pl.BlockSpec((rb, GROUP_W), lambda i, j: (base + i * n_tiles + j, CB_XA // 4)),
                  pl.BlockSpec((bb, POOL_PRE, GROUP_W), lambda i, j: (i, 0, 0)),
                  pl.BlockSpec((4, LANES, LANES), lambda i, j: (0, 0, 0)),
                  pl.BlockSpec((1, GROUP_W), lambda i, j: (0, 0))],
        out_specs=pl.BlockSpec((rb, GROUP_W), lambda i, j: (i * n_tiles + j, 0)),
        out_shape=jax.ShapeDtypeStruct((n_b * t_len, GROUP_W), BF),
        scratch_shapes=[pltpu.VMEM((bb, POOL_PRE + tt, GROUP_W), F32)],
        compiler_params=_cparams(("parallel", "arbitrary")),
        name="pool_mixer",
    )(proj, prefix, pool_w.astype(BF), pool_scale.reshape(1, GROUP_W))


def _gelu_tanh(x):
    return 0.5 * x * (1.0 + jnp.tanh(np.sqrt(2.0 / np.pi) * (x + 0.044715 * (x * x * x))))


def _gmlp_kernel(u_ref, v_ref, g_ref, b_ref, w_ref, bias_ref, y_ref, vo_ref, *, n_ch):
    for c in range(n_ch):
        rows = slice(c * GMLP_CHUNK, (c + 1) * GMLP_CHUNK)
        u = _gelu_tanh(u_ref[rows, :])
        v = _layer_norm(_gelu_tanh(v_ref[rows, :]), g_ref[...], b_ref[...])
        vo_ref[rows, :] = v
        vb = v.astype(BF)
        for g in range(4):
            cs = slice(g * LANES, (g + 1) * LANES)
            mixed = _dot(w_ref[g], vb[:, cs]) + bias_ref[:, cs]
            y_ref[rows, cs] = (u[:, cs] * mixed).astype(y_ref.dtype)


def gmlp_mixer(proj, row0, n_rows, ln_g, ln_b, w_mix, bias_full, n_ch):
    rb = n_ch * GMLP_CHUNK
    base = row0 // rb
    return pl.pallas_call(
        functools.partial(_gmlp_kernel, n_ch=n_ch),
        grid=(n_rows // rb,),
        in_specs=[pl.BlockSpec((rb, GROUP_W), lambda i: (base + i, CB_U // 4)),
                  pl.BlockSpec((rb, GROUP_W), lambda i: (base + i, CB_V // 4)),
                  pl.BlockSpec((1, GROUP_W), lambda i: (0, 0)),
                  pl.BlockSpec((1, GROUP_W), lambda i: (0, 0)),
                  pl.BlockSpec((4, GMLP_CHUNK, GMLP_CHUNK), lambda i: (0, 0, 0)),
                  pl.BlockSpec((GMLP_CHUNK, GROUP_W), lambda i: (0, 0))],
        out_specs=[pl.BlockSpec((rb, GROUP_W), lambda i: (i, 0)),
                   pl.BlockSpec((rb, GROUP_W), lambda i: (i, 0))],
        out_shape=[jax.ShapeDtypeStruct((n_rows, GROUP_W), BF),
                   jax.ShapeDtypeStruct((n_rows, GROUP_W), F32)],
        compiler_params=_cparams(("parallel",)),
        name="gmlp_mixer",
    )(proj, proj, ln_g.reshape(1, GROUP_W), ln_b.reshape(1, GROUP_W), w_mix, bias_full)


def _cmp_accumulate(row_pair, phi_ref):
    acc = None
    for l2 in range(CMP_BLOCK // 2):
        xa, xb = row_pair(l2)
        part = _dot(jnp.concatenate([xa, xb], axis=1).astype(BF), phi_ref[l2])
        acc = part if acc is None else acc + part
    return acc


def _cmp_kernel(xk_ref, xv_ref, phik_ref, phiv_ref, ok_ref, ov_ref, *, tb):
    for x_ref, phi_ref, o_ref in ((xk_ref, phik_ref, ok_ref), (xv_ref, phiv_ref, ov_ref)):
        o_ref[...] = _cmp_accumulate(
            lambda l2: (x_ref[pl.ds(2 * l2, tb, stride=CMP_BLOCK), :],
                        x_ref[pl.ds(2 * l2 + 1, tb, stride=CMP_BLOCK), :]), phi_ref)


def cmp_project(proj, n_blk, phik2, phiv2, tb):
    assert n_blk % tb == 0, (n_blk, tb)
    rows = tb * CMP_BLOCK
    return pl.pallas_call(
        functools.partial(_cmp_kernel, tb=tb),
        grid=(n_blk // tb,),
        in_specs=[pl.BlockSpec((rows, LANES), lambda i: (i, CB_CMPK)),
                  pl.BlockSpec((rows, LANES), lambda i: (i, CB_CMPV)),
                  pl.BlockSpec(phik2.shape, lambda i: (0, 0, 0)),
                  pl.BlockSpec(phiv2.shape, lambda i: (0, 0, 0))],
        out_specs=[pl.BlockSpec((tb, NSA_DH), lambda i: (i, 0)),
                   pl.BlockSpec((tb, NSA_DH), lambda i: (i, 0))],
        out_shape=[jax.ShapeDtypeStruct((n_blk, NSA_DH), F32),
                   jax.ShapeDtypeStruct((n_blk, NSA_DH), F32)],
        compiler_params=_cparams(("parallel",)),
        name="cmp_project",
    )(proj, proj, phik2, phiv2)


def _cmp_pool_kernel(x_ref, phik_ref, phiv_ref, ok_ref, ov_ref):
    for s, phi_ref, o_ref in ((0, phik_ref, ok_ref), (1, phiv_ref, ov_ref)):
        o_ref[...] = _cmp_accumulate(lambda l2: (x_ref[:, 2 * l2, s, :], x_ref[:, 2 * l2 + 1, s, :]), phi_ref)


def cmp_pool(cache_blocks, blk0, n_blk, phik2, phiv2, tb):
    assert n_blk % tb == 0 and blk0 % tb == 0, (n_blk, blk0, tb)
    base = blk0 // tb
    return pl.pallas_call(
        _cmp_pool_kernel,
        grid=(n_blk // tb,),
        in_specs=[pl.BlockSpec((tb, CMP_BLOCK, 4, NSA_DH), lambda i: (base + i, 0, 0, 0)),
                  pl.BlockSpec(phik2.shape, lambda i: (0, 0, 0)),
                  pl.BlockSpec(phiv2.shape, lambda i: (0, 0, 0))],
        out_specs=[pl.BlockSpec((tb, NSA_DH), lambda i: (i, 0)),
                   pl.BlockSpec((tb, NSA_DH), lambda i: (i, 0))],
        out_shape=[jax.ShapeDtypeStruct((n_blk, NSA_DH), F32),
                   jax.ShapeDtypeStruct((n_blk, NSA_DH), F32)],
        compiler_params=_cparams(("parallel",)),
        name="cmp_pool",
    )(cache_blocks, phik2, phiv2)


def _alibi_slope(h):
    return 2.0 ** (-8.0 * (h + 1) / NSA_H)


def _head_rows_const(n_q, fn):
    r = lax.broadcasted_iota(jnp.int32, (NSA_H * n_q, 1), 0)
    out = jnp.full((NSA_H * n_q, 1), fn(NSA_H - 1), F32)
    for h in range(NSA_H - 2, -1, -1):
        out = jnp.where(r < (h + 1) * n_q, fn(h), out)
    return out


def _stack_heads(q):
    return jnp.concatenate([q[:, h * NSA_DH:(h + 1) * NSA_DH] for h in range(NSA_H)], axis=0)


def _select_blocks(score, n_blk):
    blk = lax.broadcasted_iota(jnp.int32, score.shape, 0)
    cnt = jnp.zeros(score.shape, F32)
    for i in range(n_blk):
        row = score[i:i + 1, :]
        beats = (row > score) | ((row == score) & (blk > i))
        cnt = cnt + jnp.where(beats, 1.0, 0.0)
    return jnp.where((cnt < float(SEL_TOP)) & (score > NEG_INF), 1.0, 0.0)


def _softmax_segments(segs):
    m = None
    for s, _ in segs:
        mi = jnp.max(s, axis=1, keepdims=True)
        m = mi if m is None else jnp.maximum(m, mi)
    m = jnp.where(m > NEG_INF, m, 0.0)
    den = None
    acc = None
    for s, v in segs:
        e = jnp.exp(s - m)
        d = jnp.sum(e, axis=1, keepdims=True)
        a = _dot(e.astype(BF), v)
        den = d if den is None else den + d
        acc = a if acc is None else acc + a
    return acc / jnp.where(den > 0, den, 1.0)


def _gated_heads(o_ref, gates, o_c, o_s, o_w, n_q):
    for h in range(NSA_H):
        rows = slice(h * n_q, (h + 1) * n_q)
        c = MISC_GATE0 + 3 * h
        out = (gates[:, c:c + 1] * o_c[rows, :] + gates[:, c + 1:c + 2] * o_s[rows, :]
               + gates[:, c + 2:c + 3] * o_w[rows, :])
        o_ref[:, h * NSA_DH:(h + 1) * NSA_DH] = out.astype(o_ref.dtype)


def _nsa_prompt_kernel(q_ref, misc_ref, kc_ref, vc_ref, ks_ref, vs_ref, kw_ref, vw_ref, e_ref, o_ref,
                       imp_ref, mask_ref, m_ref, l_ref, acc_ref, *, t_len):
    n_q = Q_BLOCK
    n_cmp = t_len // CMP_BLOCK
    n_sel = t_len // SEL_BLOCK
    kt = 512
    q0 = pl.program_id(1) * n_q
    q4 = _stack_heads(q_ref[...] * (NSA_DH ** -0.5)).astype(BF)
    slope_r = _head_rows_const(n_q, _alibi_slope)
    qpos_r = q0 + lax.rem(lax.broadcasted_iota(jnp.int32, (NSA_H * n_q, 1), 0), n_q)

    lane = lax.broadcasted_iota(jnp.int32, (1, NSA_H * n_q), 1)
    qpos_l = q0 + lax.rem(lane, n_q)
    slope_l = jnp.full((1, NSA_H * n_q), _alibi_slope(NSA_H - 1), F32)
    for h in range(NSA_H - 2, -1, -1):
        slope_l = jnp.where(lane < (h + 1) * n_q, _alibi_slope(h), slope_l)
    cmp_end = lax.broadcasted_iota(jnp.int32, (n_cmp, 1), 0) * CMP_BLOCK + (CMP_BLOCK - 1)
    s_c = _dot_nt(kc_ref[...].astype(BF), q4) - slope_l * (qpos_l - cmp_end).astype(F32)
    s_c = jnp.where(cmp_end <= qpos_l, s_c, NEG_INF)
    m_c = jnp.max(s_c, axis=0, keepdims=True)
    m_c = jnp.where(m_c > NEG_INF, m_c, 0.0)
    e_c = jnp.exp(s_c - m_c)
    den_c = jnp.sum(e_c, axis=0, keepdims=True)
    p_c = e_c / jnp.where(den_c > 0, den_c, 1.0)
    o_c = _dot_tn(p_c.astype(BF), vc_ref[...].astype(BF))

    imp = p_c[:, 0:n_q]
    for h in range(1, NSA_H):
        imp = imp + p_c[:, h * n_q:(h + 1) * n_q]
    imp_ref[...] = imp
    imp = imp_ref[pl.ds(0, n_sel, stride=2), :] + imp_ref[pl.ds(1, n_sel, stride=2), :]
    blk = lax.broadcasted_iota(jnp.int32, (n_sel, 1), 0)
    cur = (q0 + lax.broadcasted_iota(jnp.int32, (1, n_q), 1)) // SEL_BLOCK
    forced = (blk == 0) | (blk == cur) | (blk == cur - 1)
    score = jnp.where(blk <= cur, jnp.where(forced, jnp.inf, imp), NEG_INF)
    sel = _select_blocks(score, n_sel)
    mask_ref[...] = _dot_tn(sel.astype(BF), e_ref[...])

    m_ref[...] = jnp.full(m_ref.shape, NEG_INF, F32)
    l_ref[...] = jnp.zeros(l_ref.shape, F32)
    acc_ref[...] = jnp.zeros(acc_ref.shape, F32)

    def sel_tile(t, carry):
        k0 = pl.multiple_of(t * kt, kt)
        s = _dot_nt(q4, ks_ref[pl.ds(k0, kt), :].astype(BF))
        d = qpos_r - (k0 + lax.broadcasted_iota(jnp.int32, (1, kt), 1))
        mk = mask_ref[:, pl.ds(k0, kt)]
        mk = jnp.concatenate([mk] * NSA_H, axis=0)
        s = jnp.where((d >= 0) & (mk > 0.5), s - slope_r * d.astype(F32), NEG_INF)
        m_old = m_ref[...]
        m_new = jnp.maximum(m_old, jnp.max(s, axis=1, keepdims=True))
        m_use = jnp.where(m_new > NEG_INF, m_new, 0.0)
        alpha = jnp.exp(m_old - m_use)
        p = jnp.exp(s - m_use)
        l_ref[...] = alpha * l_ref[...] + jnp.sum(p, axis=1, keepdims=True)
        acc_ref[...] = alpha * acc_ref[...] + _dot(p.astype(BF), vs_ref[pl.ds(k0, kt), :].astype(BF))
        m_ref[...] = m_new
        return carry

    lax.fori_loop(0, (q0 + n_q + kt - 1) // kt, sel_tile, 0)
    l_s = l_ref[...]
    o_s = acc_ref[...] / jnp.where(l_s > 0, l_s, 1.0)

    nw = WINDOW + n_q
    w0 = pl.multiple_of(jnp.maximum(q0 - WINDOW, 0), n_q)
    s_w = _dot_nt(q4, kw_ref[pl.ds(w0, nw), :].astype(BF))
    d_w = qpos_r - (w0 + lax.broadcasted_iota(jnp.int32, (1, nw), 1))
    s_w = jnp.where((d_w >= 0) & (d_w < WINDOW), s_w - slope_r * d_w.astype(F32), NEG_INF)
    o_w = _softmax_segments([(s_w, vw_ref[pl.ds(w0, nw), :].astype(BF))])

    _gated_heads(o_ref, _sigmoid(misc_ref[...]), o_c, o_s, o_w, n_q)


def nsa_prompt(proj, n_b, t_len, k_cmp, v_cmp, expand):
    n_qb = t_len // Q_BLOCK
    n_cmp = t_len // CMP_BLOCK
    n_sel = t_len // SEL_BLOCK
    seq = lambda cb: pl.BlockSpec((t_len, LANES), lambda b, i: (b, cb))
    return pl.pallas_call(
        functools.partial(_nsa_prompt_kernel, t_len=t_len),
        grid=(n_b, n_qb),
        in_specs=[pl.BlockSpec((Q_BLOCK, GROUP_W), lambda b, i: (b * n_qb + i, CB_QC // 4)),
                  pl.BlockSpec((Q_BLOCK, LANES), lambda b, i: (b * n_qb + i, CB_MISC)),
                  pl.BlockSpec((n_cmp, NSA_DH), lambda b, i: (b, 0)),
                  pl.BlockSpec((n_cmp, NSA_DH), lambda b, i: (b, 0)),
                  seq(CB_SELK), seq(CB_SELV), seq(CB_WINK), seq(CB_WINV),
                  pl.BlockSpec((n_sel, t_len), lambda b, i: (0, 0))],
        out_specs=pl.BlockSpec((Q_BLOCK, GROUP_W), lambda b, i: (b * n_qb + i, 0)),
        out_shape=jax.ShapeDtypeStruct((n_b * t_len, GROUP_W), BF),
        scratch_shapes=[pltpu.VMEM((n_cmp, Q_BLOCK), F32),
                        pltpu.VMEM((Q_BLOCK, t_len), F32),
                        pltpu.VMEM((NSA_H * Q_BLOCK, 1), F32),
                        pltpu.VMEM((NSA_H * Q_BLOCK, 1), F32),
                        pltpu.VMEM((NSA_H * Q_BLOCK, NSA_DH), F32)],
        compiler_params=_cparams(("parallel", "arbitrary")),
        name="nsa_prompt",
    )(proj, proj, k_cmp, v_cmp, proj, proj, proj, proj, expand)


def _nsa_sample_kernel(pt_ref, q_ref, misc_ref, ksn_ref, vsn_ref, kwn_ref, vwn_ref, winc_ref, pair_ref, e_ref,
                       *rest, n_pages, page, t_q):
    pages = rest[:n_pages]
    kcs = rest[n_pages:2 * n_pages]
    vcs = rest[2 * n_pages:3 * n_pages]
    o_ref, kc_ref, vc_ref, ks_ref, vs_ref = rest[3 * n_pages:]
    past_len = n_pages * page
    cpp = page // CMP_BLOCK
    n_cmp = past_len // CMP_BLOCK
    n_sel = past_len // SEL_BLOCK + 1
    n_r = NSA_H * t_q
    for j in range(n_pages):
        kc_ref[j * cpp:(j + 1) * cpp, :] = kcs[j][...]
        vc_ref[j * cpp:(j + 1) * cpp, :] = vcs[j][...]
        ks_ref[j * page:(j + 1) * page, :] = pages[j][:, 2, :].astype(BF)
        vs_ref[j * page:(j + 1) * page, :] = pages[j][:, 3, :].astype(BF)

    q4 = _stack_heads(q_ref[...] * (NSA_DH ** -0.5)).astype(BF)
    slope_r = _head_rows_const(t_q, _alibi_slope)
    qpos_r = past_len + lax.rem(lax.broadcasted_iota(jnp.int32, (n_r, 1), 0), t_q)

    cmp_end = lax.broadcasted_iota(jnp.int32, (1, n_cmp), 1) * CMP_BLOCK + (CMP_BLOCK - 1)
    s_c = _dot_nt(q4, kc_ref[...].astype(BF)) - slope_r * (qpos_r - cmp_end).astype(F32)
    s_c = jnp.where(cmp_end <= qpos_r, s_c, NEG_INF)
    m_c = jnp.max(s_c, axis=1, keepdims=True)
    m_c = jnp.where(m_c > NEG_INF, m_c, 0.0)
    e_c = jnp.exp(s_c - m_c)
    den_c = jnp.sum(e_c, axis=1, keepdims=True)
    p_c = e_c / jnp.where(den_c > 0, den_c, 1.0)
    o_c = _dot(p_c.astype(BF), vc_ref[...].astype(BF))

    imp = p_c[0:t_q, :]
    for h in range(1, NSA_H):
        imp = imp + p_c[h * t_q:(h + 1) * t_q, :]
    hi = imp.astype(BF)
    lo = (imp - hi.astype(F32)).astype(BF)
    imp = _dot(hi, pair_ref[...]) + _dot(lo, pair_ref[...])
    blk = lax.broadcasted_iota(jnp.int32, (1, LANES), 1)
    cur = (past_len + lax.broadcasted_iota(jnp.int32, (t_q, 1), 0)) // SEL_BLOCK
    forced = (blk == 0) | (blk == cur) | (blk == cur - 1)
    score = jnp.where(blk <= cur, jnp.where(forced, jnp.inf, imp), NEG_INF)
    cnt = jnp.zeros(score.shape, F32)
    for i in range(n_sel):
        col = score[:, i:i + 1]
        beats = (col > score) | ((col == score) & (blk > i))
        cnt = cnt + jnp.where(beats, 1.0, 0.0)
    sel = jnp.where((cnt < float(SEL_TOP)) & (score > NEG_INF), 1.0, 0.0)
    mask = _dot(sel.astype(BF), e_ref[...])
    mask = jnp.concatenate([mask] * NSA_H, axis=0)

    d_p = qpos_r - lax.broadcasted_iota(jnp.int32, (1, past_len), 1)
    s_p = jnp.where(mask[:, 0:past_len] > 0.5, _dot_nt(q4, ks_ref[...]) - slope_r * d_p.astype(F32), NEG_INF)
    d_n = qpos_r - (past_len + lax.broadcasted_iota(jnp.int32, (1, t_q), 1))
    s_n = _dot_nt(q4, ksn_ref[...].astype(BF)) - slope_r * d_n.astype(F32)
    s_n = jnp.where((d_n >= 0) & (mask[:, past_len:past_len + t_q] > 0.5), s_n, NEG_INF)
    o_s = _softmax_segments([(s_p, vs_ref[...]), (s_n, vsn_ref[...].astype(BF))])

    n_wc = winc_ref.shape[0]
    d_c = qpos_r - (past_len - n_wc + lax.broadcasted_iota(jnp.int32, (1, n_wc), 1))
    s_wc = _dot_nt(q4, winc_ref[:, 0, :].astype(BF)) - slope_r * d_c.astype(F32)
    s_wc = jnp.where(d_c < WINDOW, s_wc, NEG_INF)
    s_wn = _dot_nt(q4, kwn_ref[...].astype(BF)) - slope_r * d_n.astype(F32)
    s_wn = jnp.where(d_n >= 0, s_wn, NEG_INF)
    o_w = _softmax_segments([(s_wc, winc_ref[:, 1, :].astype(BF)), (s_wn, vwn_ref[...].astype(BF))])

    _gated_heads(o_ref, _sigmoid(misc_ref[...]), o_c, o_s, o_w, t_q)


def nsa_sample(proj, row0, n_b, t_q, cache_nsa, layer, page_table, kc_pool, vc_pool, cache_win, pair, expand):
    n_pages = page_table.shape[1]
    page = cache_nsa.shape[2]
    cpp = page // CMP_BLOCK
    past_len = n_pages * page
    n_wc = cache_win.shape[2]
    base = row0 // t_q
    new = lambda cb: pl.BlockSpec((t_q, LANES), lambda b, pt: (base + b, cb))

    def page_spec(j):
        return pl.BlockSpec((None, None, page, 4, NSA_DH), lambda b, pt: (layer, pt[b, j], 0, 0, 0))

    def cmp_spec(j):
        return pl.BlockSpec((None, cpp, NSA_DH), lambda b, pt: (pt[b, j], 0, 0))

    in_specs = [pl.BlockSpec((t_q, GROUP_W), lambda b, pt: (base + b, CB_QC // 4)),
                new(CB_MISC), new(CB_SELK), new(CB_SELV), new(CB_WINK), new(CB_WINV),
                pl.BlockSpec((None, None, n_wc, 2, NSA_DH), lambda b, pt: (layer, b, 0, 0, 0)),
                pl.BlockSpec(pair.shape, lambda b, pt: (0, 0)),
                pl.BlockSpec(expand.shape, lambda b, pt: (0, 0))]
    in_specs += [page_spec(j) for j in range(n_pages)]
    in_specs += [cmp_spec(j) for j in range(n_pages)] * 2
    return pl.pallas_call(
        functools.partial(_nsa_sample_kernel, n_pages=n_pages, page=page, t_q=t_q),
        grid_spec=pltpu.PrefetchScalarGridSpec(
            num_scalar_prefetch=1,
            grid=(n_b,),
            in_specs=in_specs,
            out_specs=pl.BlockSpec((t_q, GROUP_W), lambda b, pt: (b, 0)),
            scratch_shapes=[pltpu.VMEM((past_len // CMP_BLOCK, NSA_DH), F32),
                            pltpu.VMEM((past_len // CMP_BLOCK, NSA_DH), F32),
                            pltpu.VMEM((past_len, NSA_DH), BF),
                            pltpu.VMEM((past_len, NSA_DH), BF)]),
        out_shape=jax.ShapeDtypeStruct((n_b * t_q, GROUP_W), BF),
        compiler_params=_cparams(("arbitrary",)),
        name="nsa_sample",
    )(page_table, proj, proj, proj, proj, proj, proj, cache_win, pair, expand,
      *([cache_nsa] * n_pages), *([kc_pool] * n_pages), *([vc_pool] * n_pages))


def _log_sigmoid(z):
    return jnp.minimum(z, 0.0) - jnp.log(1.0 + jnp.exp(-jnp.abs(z)))


def _cumsum_rows(x, tri):
    hi = x.astype(BF)
    lo = (x - hi.astype(F32)).astype(BF)
    return _dot(tri, hi) + _dot(tri, lo)


def _rows_to_col(row):
    n = row.shape[1]
    eye = lax.broadcasted_iota(jnp.int32, (n, n), 0) == lax.broadcasted_iota(jnp.int32, (n, n), 1)
    return jnp.sum(jnp.where(eye, row, 0.0), axis=1, keepdims=True)


def _gla_out(o, god, ng):
    o = o * lax.rsqrt(jnp.mean(o * o, -1, keepdims=True) + LN_EPS) * ng
    return o * (god * _sigmoid(god))


def _gla_prompt_kernel(q_ref, k_ref, v_ref, god_ref, misc_ref, wa_ref, ba_ref, ng_ref, o_ref, s_ref, st_ref,
                       *, n_chunks):
    c = GLA_CHUNK
    st_ref[...] = jnp.zeros(st_ref.shape, F32)
    r_i = lax.broadcasted_iota(jnp.int32, (c, c), 0)
    c_i = lax.broadcasted_iota(jnp.int32, (c, c), 1)
    causal = r_i >= c_i
    tri = jnp.where(causal, 1.0, 0.0).astype(BF)

    def chunk(n, carry):
        rows = pl.ds(pl.multiple_of(n * c, c), c)
        misc = misc_ref[rows, :].astype(BF)
        for h in range(GLA_H):
            cs = slice(h * LANES, (h + 1) * LANES)
            z = _dot(misc, wa_ref[:, cs]) + ba_ref[:, cs]
            bc = _cumsum_rows(_log_sigmoid(z) / GLA_TAU, tri)
            bl = bc[c - 1:c, :]
            q = q_ref[rows, cs] * (GLA_DK ** -0.5)
            k = k_ref[rows, cs]
            v = v_ref[rows, cs].astype(BF)
            qe = (q * jnp.exp(bc)).astype(BF)
            att = jnp.where(causal, _dot_nt(qe, (k * jnp.exp(-bc)).astype(BF)), 0.0)
            st = st_ref[h]
            o = _dot(qe, st.astype(BF)) + _dot(att.astype(BF), v)
            st_ref[h] = _rows_to_col(jnp.exp(bl)) * st + _dot_tn((k * jnp.exp(bl - bc)).astype(BF), v)
            o_ref[rows, cs] = _gla_out(o, god_ref[rows, cs], ng_ref[...]).astype(o_ref.dtype)
        return carry

    lax.fori_loop(0, n_chunks, chunk, 0)
    for h in range(GLA_H):
        s_ref[h] = st_ref[h, 0:GLA_DK, :]


def gla_prompt(proj, n_b, t_len, wa_pad, ba_pad, norm_g):
    w = GLA_H * LANES
    seq = lambda cb0: pl.BlockSpec((t_len, w), lambda b: (b, cb0 // GLA_H))
    return pl.pallas_call(
        functools.partial(_gla_prompt_kernel, n_chunks=t_len // GLA_CHUNK),
        grid=(n_b,),
        in_specs=[seq(CB_QD), seq(CB_KD), seq(CB_VD), seq(CB_GOD),
                  pl.BlockSpec((t_len, LANES), lambda b: (b, CB_MISC)),
                  pl.BlockSpec((LANES, w), lambda b: (0, 0)),
                  pl.BlockSpec((1, w), lambda b: (0, 0)),
                  pl.BlockSpec((1, GLA_DV), lambda b: (0, 0))],
        out_specs=[pl.BlockSpec((t_len, w), lambda b: (b, 0)),
                   pl.BlockSpec((None, GLA_H, GLA_DK, GLA_DV), lambda b: (b, 0, 0, 0))],
        out_shape=[jax.ShapeDtypeStruct((n_b * t_len, GROUP_W), BF),
                   jax.ShapeDtypeStruct((n_b, GLA_H, GLA_DK, GLA_DV), F32)],
        scratch_shapes=[pltpu.VMEM((GLA_H, LANES, GLA_DV), F32)],
        compiler_params=_cparams(("parallel",)),
        name="gla_prompt",
    )(proj, proj, proj, proj, proj, wa_pad, ba_pad, norm_g.reshape(1, GLA_DV))


def _gla_sample_kernel(q_ref, k_ref, v_ref, god_ref, misc_ref, wa_ref, ba_ref, ng_ref, s0_ref, o_ref, s_ref,
                       *, bb, t_q):
    n_r = bb * t_q
    r_i = lax.broadcasted_iota(jnp.int32, (n_r, n_r), 0)
    c_i = lax.broadcasted_iota(jnp.int32, (n_r, n_r), 1)
    causal = (r_i >= c_i) & (r_i // t_q == c_i // t_q)
    tri = jnp.where(causal, 1.0, 0.0).astype(BF)
    row_b = lax.broadcasted_iota(jnp.int32, (n_r, 1), 0) // t_q

    z = _dot(misc_ref[...].astype(BF), wa_ref[...]) + ba_ref[...]
    bc = _cumsum_rows(_log_sigmoid(z) / GLA_TAU, tri)
    q = q_ref[...] * (GLA_DK ** -0.5)
    k = k_ref[...]
    v = v_ref[...].astype(BF)
    qe = q * jnp.exp(bc)
    att = jnp.where(causal, _dot_nt(qe.astype(BF), (k * jnp.exp(-bc)).astype(BF)), 0.0)
    o = _dot(att.astype(BF), v)
    for b in range(bb):
        mine = row_b == b
        bl = bc[(b + 1) * t_q - 1:(b + 1) * t_q, :]
        s0 = s0_ref[b, 0]
        o = o + _dot(jnp.where(mine, qe, 0.0)[:, 0:GLA_DK].astype(BF), s0.astype(BF))
        k2 = jnp.where(mine, k * jnp.exp(bl - bc), 0.0).astype(BF)
        upd = _dot_tn(k2, v)
        s_ref[b, 0] = _rows_to_col(jnp.exp(bl))[0:GLA_DK, :] * s0 + upd[0:GLA_DK, :]
    o_ref[...] = _gla_out(o, god_ref[...], ng_ref[...]).astype(o_ref.dtype)


def gla_sample(proj, row0, n_b, t_q, wa_pad, ba_pad, norm_g, s0, s0_b0, bb):
    rb = bb * t_q
    base = row0 // rb
    s_i0 = s0_b0 // bb
    seq = lambda cb0: pl.BlockSpec((rb, LANES), lambda i, h: (base + i, cb0 + h))
    return pl.pallas_call(
        functools.partial(_gla_sample_kernel, bb=bb, t_q=t_q),
        grid=(n_b // bb, GLA_H),
        in_specs=[seq(CB_QD), seq(CB_KD), seq(CB_VD), seq(CB_GOD),
                  pl.BlockSpec((rb, LANES), lambda i, h: (base + i, CB_MISC)),
                  pl.BlockSpec((LANES, LANES), lambda i, h: (0, h)),
                  pl.BlockSpec((1, LANES), lambda i, h: (0, h)),
                  pl.BlockSpec((1, GLA_DV), lambda i, h: (0, 0)),
                  pl.BlockSpec((bb, 1, GLA_DK, GLA_DV), lambda i, h: (s_i0 + i, h, 0, 0))],
        out_specs=[pl.BlockSpec((rb, GLA_DV), lambda i, h: (i, h)),
                   pl.BlockSpec((bb, 1, GLA_DK, GLA_DV), lambda i, h: (i, h, 0, 0))],
        out_shape=[jax.ShapeDtypeStruct((n_b * t_q, GROUP_W), BF),
                   jax.ShapeDtypeStruct((n_b, GLA_H, GLA_DK, GLA_DV), F32)],
        compiler_params=_cparams(("parallel", "parallel")),
        name="gla_sample",
    )(proj, proj, proj, proj, proj, wa_pad, ba_pad, norm_g.reshape(1, GLA_DV), s0)


def _mem_attn_kernel(q_ref, kv_ref, o_ref):
    q = q_ref[...] * (MEM_DH ** -0.5)
    for h in range(MEM_H):
        cs = slice(h * MEM_DH, (h + 1) * MEM_DH)
        s = _dot_nt(q[:, cs].astype(BF), kv_ref[:, 0, h, :].astype(BF))
        e = jnp.exp(s - jnp.max(s, axis=1, keepdims=True))
        p = e / jnp.sum(e, axis=1, keepdims=True)
        o_ref[:, cs] = _dot(p.astype(BF), kv_ref[:, 1, h, :].astype(BF)).astype(o_ref.dtype)


def mem_attention(qm, row0, n_b, t_len, kv, layer, tq):
    n_t = t_len // tq
    base = row0 // tq
    w = MEM_H * MEM_DH
    return pl.pallas_call(
        _mem_attn_kernel,
        grid=(n_b, n_t),
        in_specs=[pl.BlockSpec((tq, w), lambda b, i: (base + b * n_t + i, 0)),
                  pl.BlockSpec((None, None, MEM_LEN, 2, MEM_H, MEM_DH), lambda b, i: (layer, b, 0, 0, 0, 0))],
        out_specs=pl.BlockSpec((tq, w), lambda b, i: (b * n_t + i, 0)),
        out_shape=jax.ShapeDtypeStruct((n_b * t_len, w), BF),
        compiler_params=_cparams(("parallel", "parallel")),
        name="mem_attention",
    )(qm, kv)


def _router_kernel(x_ref, w_ref, b_ref, tri_ref, idx_ref, gate_ref, rank_ref, cnt_ref, run_ref):
    @pl.when(pl.program_id(0) == 0)
    def _():
        run_ref[...] = jnp.zeros(run_ref.shape, F32)

    logits = jnp.dot(x_ref[...], w_ref[...], precision=lax.Precision.HIGHEST,
                     preferred_element_type=F32) + b_ref[...]
    lane = lax.broadcasted_iota(jnp.int32, logits.shape, 1).astype(F32)
    vals = jnp.where(lane < N_EXPERTS, logits, NEG_INF)
    idx_out = jnp.zeros(logits.shape, F32)
    e_out = jnp.zeros(logits.shape, F32)
    den = None
    top0 = None
    onehots = []
    for k in range(TOP_K):
        m = jnp.max(vals, axis=1, keepdims=True)
        idx = jnp.min(jnp.where(vals == m, lane, float(LANES)), axis=1, keepdims=True)
        if k == 0:
            top0 = m
        e = jnp.exp(m - top0)
        den = e if den is None else den + e
        idx_out = jnp.where(lane == k, idx, idx_out)
        e_out = jnp.where(lane == k, e, e_out)
        onehots.append(jnp.where(lane == idx, 1.0, 0.0))
        vals = jnp.where(lane == idx, NEG_INF, vals)
    idx_ref[...] = idx_out.astype(jnp.int32)
    gate_ref[...] = e_out / den

    chosen = onehots[0] + onehots[1] + onehots[2] + onehots[3]
    before = _dot(tri_ref[...], chosen.astype(BF)) + run_ref[...]
    rank_out = jnp.zeros(logits.shape, F32)
    for k in range(TOP_K):
        r = jnp.sum(before * onehots[k], axis=1, keepdims=True)
        rank_out = jnp.where(lane == k, r, rank_out)
    rank_ref[...] = rank_out.astype(jnp.int32)
    run_ref[...] = run_ref[...] + jnp.sum(chosen, axis=0, keepdims=True)
    cnt_ref[...] = run_ref[...]


def moe_router(x, w_pad, b_pad, tm):
    m, d = x.shape
    assert m % tm == 0, (m, tm)
    tri = jnp.asarray(np.tril(np.ones((tm, tm), np.float32), -1), BF)
    row = lambda dt: jax.ShapeDtypeStruct((m, LANES), dt)
    return pl.pallas_call(
        _router_kernel,
        grid=(m // tm,),
        in_specs=[pl.BlockSpec((tm, d), lambda i: (i, 0)),
                  pl.BlockSpec((d, LANES), lambda i: (0, 0)),
                  pl.BlockSpec((1, LANES), lambda i: (0, 0)),
                  pl.BlockSpec((tm, tm), lambda i: (0, 0))],
        out_specs=[pl.BlockSpec((tm, LANES), lambda i: (i, 0)),
                   pl.BlockSpec((tm, LANES), lambda i: (i, 0)),
                   pl.BlockSpec((tm, LANES), lambda i: (i, 0)),
                   pl.BlockSpec((1, LANES), lambda i: (0, 0))],
        out_shape=[row(jnp.int32), row(F32), row(jnp.int32), jax.ShapeDtypeStruct((1, LANES), F32)],
        scratch_shapes=[pltpu.VMEM((1, LANES), F32)],
        compiler_params=_cparams(("arbitrary",)),
        name="moe_router",
    )(x, w_pad, b_pad, tri)


def _slot_ids_to_smem(slot_ref, ids_ref, sem):
    cp = pltpu.make_async_copy(slot_ref.at[0], ids_ref, sem)
    cp.start()
    cp.wait()


def _dispatch_kernel(slot_ref, x_ref, xg_in, xg_out, ids_ref, sem_ids, sem, *, tn):
    del xg_in
    _slot_ids_to_smem(slot_ref, ids_ref, sem_ids)

    def issue(t, carry):
        for k in range(TOP_K):
            pltpu.make_async_copy(x_ref.at[pl.ds(t, 1), :],
                                  xg_out.at[pl.ds(ids_ref[0, t * TOP_K + k], 1), :], sem).start()
        return carry

    lax.fori_loop(0, tn, issue, 0)
    for k in range(TOP_K):
        pltpu.make_async_copy(x_ref, xg_out.at[pl.ds(0, tn), :], sem).wait()


def moe_dispatch(x, slots, cap):
    n, d = x.shape
    tn = MOE_TOK
    assert n % tn == 0, (n, tn)
    return pl.pallas_call(
        functools.partial(_dispatch_kernel, tn=tn),
        grid=(n // tn,),
        in_specs=[pl.BlockSpec((1, 1, TOP_K * tn), lambda i: (i, 0, 0)),
                  pl.BlockSpec((tn, d), lambda i: (i, 0)),
                  pl.BlockSpec(memory_space=pl.ANY)],
        out_specs=pl.BlockSpec(memory_space=pl.ANY),
        out_shape=jax.ShapeDtypeStruct((cap, d), x.dtype),
        scratch_shapes=[pltpu.SMEM((1, TOP_K * tn), jnp.int32),
                        pltpu.SemaphoreType.DMA(()),
                        pltpu.SemaphoreType.DMA(())],
        input_output_aliases={2: 0},
        compiler_params=_cparams(("arbitrary",)),
        name="moe_dispatch",
    )(slots.reshape(n // tn, 1, TOP_K * tn), x, jnp.zeros((cap, d), x.dtype))


def _moe_up_kernel(be_ref, nu_ref, x_ref, wg_ref, wu_ref, bg_ref, bu_ref, o_ref, wgb_ref, wub_ref):
    i = pl.program_id(1)
    changed = jnp.logical_or(i == 0, be_ref[i] != be_ref[jnp.maximum(i - 1, 0)])

    @pl.when(changed)
    def _():
        wgb_ref[...] = wg_ref[...].astype(BF)
        wub_ref[...] = wu_ref[...].astype(BF)

    @pl.when(i < nu_ref[0])
    def _():
        x = x_ref[...].astype(BF)
        g = jnp.minimum(_dot(x, wgb_ref[...]) + bg_ref[...], SWIGLU_LIMIT)
        u = jnp.clip(_dot(x, wub_ref[...]) + bu_ref[...], -SWIGLU_LIMIT, SWIGLU_LIMIT)
        o_ref[...] = ((u + 1.0) * g * _sigmoid(SWIGLU_ALPHA * g)).astype(o_ref.dtype)

    @pl.when(i >= nu_ref[0])
    def _():
        o_ref[...] = jnp.zeros(o_ref.shape, o_ref.dtype)


def moe_up(xg, blk_expert, n_used, w1, b1):
    cap, d = xg.shape
    n_blk = cap // MOE_TM
    n_j = D_FF // MOE_TN
    return pl.pallas_call(
        _moe_up_kernel,
        grid_spec=pltpu.PrefetchScalarGridSpec(
            num_scalar_prefetch=2,
            grid=(n_j, n_blk),
            in_specs=[pl.BlockSpec((MOE_TM, d), lambda j, i, be, nu: (jnp.minimum(i, nu[0] - 1), 0)),
                      pl.BlockSpec((None, d, MOE_TN), lambda j, i, be, nu: (be[i], 0, j)),
                      pl.BlockSpec((None, d, MOE_TN), lambda j, i, be, nu: (be[i], 0, n_j + j)),
                      pl.BlockSpec((None, 1, MOE_TN), lambda j, i, be, nu: (be[i], 0, j)),
                      pl.BlockSpec((None, 1, MOE_TN), lambda j, i, be, nu: (be[i], 0, n_j + j))],
            out_specs=pl.BlockSpec((MOE_TM, MOE_TN), lambda j, i, be, nu: (i, j)),
            scratch_shapes=[pltpu.VMEM((d, MOE_TN), BF), pltpu.VMEM((d, MOE_TN), BF)]),
        out_shape=jax.ShapeDtypeStruct((cap, D_FF), BF),
        compiler_params=_cparams(("arbitrary", "arbitrary")),
        name="moe_up",
    )(blk_expert, n_used, xg, w1, w1, b1, b1)


def _moe_down_kernel(be_ref, nu_ref, a_ref, w_ref, b_ref, o_ref, wb_ref):
    i = pl.program_id(1)
    changed = jnp.logical_or(i == 0, be_ref[i] != be_ref[jnp.maximum(i - 1, 0)])

    @pl.when(changed)
    def _():
        wb_ref[...] = w_ref[...].astype(BF)

    @pl.when(i < nu_ref[0])
    def _():
        o_ref[...] = _dot(a_ref[...], wb_ref[...]) + b_ref[...]

    @pl.when(i >= nu_ref[0])
    def _():
        o_ref[...] = jnp.zeros(o_ref.shape, o_ref.dtype)


def moe_down(act, blk_expert, n_used, w2, b2):
    cap, f = act.shape
    d = w2.shape[2]
    n_blk = cap // MOE_TM
    return pl.pallas_call(
        _moe_down_kernel,
        grid_spec=pltpu.PrefetchScalarGridSpec(
            num_scalar_prefetch=2,
            grid=(d // MOE_TN_DOWN, n_blk),
            in_specs=[pl.BlockSpec((MOE_TM, f), lambda j, i, be, nu: (jnp.minimum(i, nu[0] - 1), 0)),
                      pl.BlockSpec((None, f, MOE_TN_DOWN), lambda j, i, be, nu: (be[i], 0, j)),
                      pl.BlockSpec((None, 1, MOE_TN_DOWN), lambda j, i, be, nu: (be[i], 0, j))],
            out_specs=pl.BlockSpec((MOE_TM, MOE_TN_DOWN), lambda j, i, be, nu: (i, j)),
            scratch_shapes=[pltpu.VMEM((f, MOE_TN_DOWN), BF)]),
        out_shape=jax.ShapeDtypeStruct((cap, d), F32),
        compiler_params=_cparams(("arbitrary", "arbitrary")),
        name="moe_down",
    )(blk_expert, n_used, act, w2, b2)


def _combine_ln_kernel(slot_ref, yb_hbm, gate_ref, x_ref, g_ref, b_ref, o_ref, ids_ref, buf_ref, sem_ids, sem, *, tn):
    _slot_ids_to_smem(slot_ref, ids_ref, sem_ids)

    def issue(t, carry):
        for k in range(TOP_K):
            pltpu.make_async_copy(yb_hbm.at[pl.ds(ids_ref[0, t * TOP_K + k], 1), :],
                                  buf_ref.at[k, pl.ds(t, 1), :], sem).start()
        return carry

    lax.fori_loop(0, tn, issue, 0)
    for k in range(TOP_K):
        pltpu.make_async_copy(yb_hbm.at[pl.ds(0, tn), :], buf_ref.at[k], sem).wait()
    gate = gate_ref[...]
    y = gate[:, 0:1] * buf_ref[0]
    for k in range(1, TOP_K):
        y = y + gate[:, k:k + 1] * buf_ref[k]
    o_ref[...] = _layer_norm(DN_ALPHA * x_ref[...] + y, g_ref[...], b_ref[...])


def moe_combine_ln(yb, slots, gate, x, g, b):
    n, d = x.shape
    tn = MOE_TOK
    assert n % tn == 0, (n, tn)
    return pl.pallas_call(
        functools.partial(_combine_ln_kernel, tn=tn),
        grid=(n // tn,),
        in_specs=[pl.BlockSpec((1, 1, TOP_K * tn), lambda i: (i, 0, 0)),
                  pl.BlockSpec(memory_space=pl.ANY),
                  pl.BlockSpec((tn, LANES), lambda i: (i, 0)),
                  pl.BlockSpec((tn, d), lambda i: (i, 0)),
                  pl.BlockSpec((1, d), lambda i: (0, 0)),
                  pl.BlockSpec((1, d), lambda i: (0, 0))],
        out_specs=pl.BlockSpec((tn, d), lambda i: (i, 0)),
        out_shape=jax.ShapeDtypeStruct((n, d), F32),
        scratch_shapes=[pltpu.SMEM((1, TOP_K * tn), jnp.int32),
                        pltpu.VMEM((TOP_K, tn, d), F32),
                        pltpu.SemaphoreType.DMA(()),
                        pltpu.SemaphoreType.DMA(())],
        compiler_params=_cparams(("arbitrary",)),
        name="moe_combine_ln",
    )(slots.reshape(n // tn, 1, TOP_K * tn), yb, gate, x, g.reshape(1, d), b.reshape(1, d))


def moe_ffn_ln(x, router_w, router_b, w1, b1, w2, b2, e0, ln_g, ln_b):
    n, d = x.shape
    w_pad = jnp.pad(router_w, ((0, 0), (0, LANES - N_EXPERTS)))
    b_pad = jnp.pad(router_b, (0, LANES - N_EXPERTS)).reshape(1, LANES)
    idx, gate, rank, cnt = moe_router(x, w_pad, b_pad, 512)

    counts = cnt[0, :N_EXPERTS].astype(jnp.int32)
    padded = (counts + MOE_TM - 1) // MOE_TM * MOE_TM
    pad_end = jnp.cumsum(padded)
    pad_start = pad_end - padded
    onehot = idx[:, :TOP_K, None] == jnp.arange(N_EXPERTS, dtype=jnp.int32)
    slots = (jnp.sum(jnp.where(onehot, pad_start, 0), axis=-1) + rank[:, :TOP_K]).reshape(-1)
    n_blk = -(-(n * TOP_K + N_EXPERTS * (MOE_TM - 1)) // MOE_TM)
    blk_row0 = jnp.arange(n_blk, dtype=jnp.int32) * MOE_TM
    blk_expert = e0 + jnp.minimum(jnp.sum(pad_end[None, :] <= blk_row0[:, None], axis=1), N_EXPERTS - 1)
    n_used = (pad_end[-1] // MOE_TM).astype(jnp.int32).reshape(1)

    xg = moe_dispatch(x, slots, n_blk * MOE_TM)
    act = moe_up(xg, blk_expert.astype(jnp.int32), n_used, w1, b1)
    yb = moe_down(act, blk_expert.astype(jnp.int32), n_used, w2, b2)
    return moe_combine_ln(yb, slots, gate, x, ln_g, ln_b)


def _pad_heads(w):
    lead = w.shape[:-1]
    w = w.reshape(lead + (GLA_H, GLA_DK))
    w = jnp.pad(w, [(0, 0)] * len(lead) + [(0, 0), (0, LANES - GLA_DK)])
    return w.reshape(lead + (GLA_H * LANES,))


def _relayout_w_in(w):
    c = np.cumsum((GROUP_W, 2 * GROUP_W, GROUP_W, 6 * NSA_DH, 3 * NSA_H, GLA_H * GLA_DK, GLA_H * GLA_DK, GROUP_W,
                   GLA_RANK, GROUP_W))
    gc, qd, kd, vd, lrd, god = (w[:, c[3]:c[4]], w[:, c[4]:c[5]], w[:, c[5]:c[6]], w[:, c[6]:c[7]],
                                w[:, c[7]:c[8]], w[:, c[8]:c[9]])
    misc = jnp.pad(jnp.concatenate([gc, lrd], axis=1), ((0, 0), (0, LANES - 3 * NSA_H - GLA_RANK)))
    out = jnp.concatenate([w[:, :c[3]], misc, jnp.zeros((w.shape[0], LANES), w.dtype),
                           _pad_heads(qd), _pad_heads(kd), vd, god], axis=1)
    assert out.shape[1] == D_PROJ
    return out.astype(BF)


def kernel(x_prompt, x_sample, cache_pool, cache_nsa_kv, cache_win_kv, state_gla, cache_mem_kv, page_table,
           mem_prompt, ln_in_g, ln_in_b, w_in, w_out, pool_w, pool_scale, gmlp_ln_g, gmlp_ln_b, gmlp_ws, gmlp_bs,
           nsa_phi, gla_wa, gla_ba, gla_norm_g, ln1_g, ln1_b, mem_wq, mem_wkv, mem_wo, ln2_g, ln2_b,
           router_w, router_b, moe_w1, moe_b1, moe_w2, moe_b2, ln3_g, ln3_b):
    n_bp, t_p, d = x_prompt.shape
    n_bs, t_s, _ = x_sample.shape
    n_p = n_bp * t_p
    n_s = n_bs * t_s
    n_pool = cache_nsa_kv.shape[1]
    page = cache_nsa_kv.shape[2]
    past_len = page_table.shape[1] * page
    n_wc = cache_win_kv.shape[2]
    assert t_p % 512 == 0 and n_s % 512 == 0 and t_s == 8 and past_len % SEL_BLOCK == 0 and n_wc == WINDOW

    x = jnp.concatenate([ln_rows(x_prompt.reshape(n_p, d), ln_in_g, ln_in_b, 512),
                         ln_rows(x_sample.reshape(n_s, d), ln_in_g, ln_in_b, 512)], axis=0)
    n_tok = n_p + n_s
    tm_big = 1024 if n_tok % 1024 == 0 else 512
    cpp = page // CMP_BLOCK
    cache_blocks = cache_nsa_kv.reshape(DEPTH * n_pool * cpp, CMP_BLOCK, 4, NSA_DH)
    state0 = state_gla.reshape(DEPTH * n_bs, GLA_H, GLA_DK, GLA_DV)
    w1_all = moe_w1.reshape(DEPTH * N_EXPERTS, d, 2 * D_FF)
    b1_all = moe_b1.reshape(DEPTH * N_EXPERTS, 1, 2 * D_FF)
    w2_all = moe_w2.reshape(DEPTH * N_EXPERTS, D_FF, d)
    b2_all = moe_b2.reshape(DEPTH * N_EXPERTS, 1, d)

    expand_p = (np.arange(t_p)[None, :] // SEL_BLOCK == np.arange(t_p // SEL_BLOCK)[:, None])
    expand_p = jnp.asarray(expand_p, BF)
    expand_s = (np.arange(past_len + LANES)[None, :] // SEL_BLOCK == np.arange(LANES)[:, None])
    expand_s = jnp.asarray(expand_s, BF)
    pair_s = jnp.asarray(np.arange(past_len // CMP_BLOCK)[:, None] // 2 == np.arange(LANES)[None, :], BF)
    bb_s = LANES // t_s
    eye_bb = jnp.eye(bb_s, dtype=F32)

    outs = {k: [] for k in ("pool_p", "nsa_p", "win_p", "gla_p", "mem_p", "pool_s", "chunk_s", "nsa_s", "win_s",
                            "gla_s")}
    for l in range(DEPTH):
        proj = matmul(x, _relayout_w_in(w_in[l]), tm_big, 512)

        pre_p = jnp.zeros((n_bp, POOL_PRE, GROUP_W), F32)
        pre_s = jnp.pad(cache_pool[l], ((0, 0), (1, 0), (0, 0)))
        ya = jnp.concatenate([
            pool_mixer(proj, 0, n_bp, t_p, pre_p, pool_w[l], pool_scale[l], 0, 1, 512),
            pool_mixer(proj, n_p, n_bs, t_s, pre_s, pool_w[l], pool_scale[l], past_len, bb_s, t_s)], axis=0)

        tril = jnp.tril(jnp.ones((GMLP_CHUNK, GMLP_CHUNK), F32))
        w_mix_p = (gmlp_ws[l] * tril).astype(BF)
        bias_p = jnp.repeat(gmlp_bs[l].T, LANES, axis=1)
        ws_s = gmlp_ws[l][:, :t_s, :t_s] * tril[:t_s, :t_s]
        w_mix_s = jnp.einsum('ab,gts->gatbs', eye_bb, ws_s).reshape(4, LANES, LANES).astype(BF)
        bias_s = jnp.tile(jnp.repeat(gmlp_bs[l][:, :t_s].T, LANES, axis=1), (bb_s, 1))
        yb_p, _ = gmlp_mixer(proj, 0, n_p, gmlp_ln_g[l], gmlp_ln_b[l], w_mix_p, bias_p, 4)
        yb_s, v_s = gmlp_mixer(proj, n_p, n_s, gmlp_ln_g[l], gmlp_ln_b[l], w_mix_s, bias_s, 4)
        yb = jnp.concatenate([yb_p, yb_s], axis=0)

        phi2 = nsa_phi[l].astype(BF).reshape(2, CMP_BLOCK // 2, 2 * NSA_DH, NSA_DH)
        kc_p, vc_p = cmp_project(proj, n_p // CMP_BLOCK, phi2[0], phi2[1], t_p // CMP_BLOCK)
        yc_p = nsa_prompt(proj, n_bp, t_p, kc_p, vc_p, expand_p)
        kc_pool, vc_pool = cmp_pool(cache_blocks, l * n_pool * cpp, n_pool * cpp, phi2[0], phi2[1], 128)
        yc_s = nsa_sample(proj, n_p, n_bs, t_s, cache_nsa_kv, l, page_table,
                          kc_pool.reshape(n_pool, cpp, NSA_DH), vc_pool.reshape(n_pool, cpp, NSA_DH),
                          cache_win_kv, pair_s, expand_s)
        yc = jnp.concatenate([yc_p, yc_s], axis=0)

        wa_pad = jnp.zeros((LANES, GLA_H * LANES), F32).at[MISC_LR0:MISC_LR0 + GLA_RANK].set(_pad_heads(gla_wa[l]))
        wa_pad = wa_pad.astype(BF)
        ba_pad = _pad_heads(gla_ba[l]).reshape(1, GLA_H * LANES)
        yd_p, st_p = gla_prompt(proj, n_bp, t_p, wa_pad, ba_pad, gla_norm_g[l])
        yd_s, st_s = gla_sample(proj, n_p, n_bs, t_s, wa_pad, ba_pad, gla_norm_g[l], state0, l * n_bs, bb_s)
        yd = jnp.concatenate([yd_p, yd_s], axis=0)

        x = matmul_res_ln([ya, yb, yc, yd], w_out[l].astype(BF), x, ln1_g[l], ln1_b[l], 256)

        qm = matmul(x, mem_wq[l].astype(BF), tm_big, 512)
        mem_kv_p = matmul(mem_prompt.reshape(n_bp * MEM_LEN, d), mem_wkv[l].astype(BF), 512, 512)
        mem_kv_p = mem_kv_p.reshape(1, n_bp, MEM_LEN, 2, MEM_H, MEM_DH)
        om = jnp.concatenate([
            mem_attention(qm, 0, n_bp, t_p, mem_kv_p, 0, 512),
            mem_attention(qm, n_p, n_bs, t_s, cache_mem_kv, l, t_s)], axis=0)
        x = matmul_res_ln([om], mem_wo[l].astype(BF), x, ln2_g[l], ln2_b[l], 256)

        x = moe_ffn_ln(x, router_w[l], router_b[l], w1_all, b1_all, w2_all, b2_all, l * N_EXPERTS,
                       ln3_g[l], ln3_b[l])

        seqs_p = lambda c0, c1: proj[:n_p, c0 * LANES:c1 * LANES].reshape(n_bp, t_p, (c1 - c0) * LANES)
        seqs_s = lambda c0, c1: proj[n_p:, c0 * LANES:c1 * LANES].reshape(n_bs, t_s, (c1 - c0) * LANES)
        outs["pool_p"].append(seqs_p(CB_XA, CB_U)[:, t_p - POOL_BUF:])
        outs["nsa_p"].append(seqs_p(CB_CMPK, CB_WINK).reshape(n_bp, t_p, 4, NSA_DH))
        outs["win_p"].append(seqs_p(CB_WINK, CB_MISC)[:, t_p - min(WINDOW, t_p):]
                             .reshape(n_bp, min(WINDOW, t_p), 2, NSA_DH))
        outs["gla_p"].append(st_p)
        outs["mem_p"].append(mem_kv_p[0])
        outs["pool_s"].append(seqs_s(CB_XA, CB_U))
        outs["chunk_s"].append(v_s.reshape(n_bs, t_s, GROUP_W))
        outs["nsa_s"].append(seqs_s(CB_CMPK, CB_WINK).reshape(n_bs, t_s, 4, NSA_DH))
        outs["win_s"].append(seqs_s(CB_WINK, CB_MISC).reshape(n_bs, t_s, 2, NSA_DH))
        outs["gla_s"].append(st_s)

    st = lambda k: jnp.stack(outs[k])
    return (x[:n_p].reshape(n_bp, t_p, d), x[n_p:].reshape(n_bs, t_s, d), st("pool_p"), st("nsa_p"), st("win_p"),
            st("gla_p"), st("mem_p"), st("pool_s"), st("chunk_s"), st("nsa_s"), st("win_s"), st("gla_s"))
```

```python
import functools

import jax
import jax.numpy as jnp
import numpy as np
from jax import lax
from jax.experimental import pallas as pl
from jax.experimental.pallas import tpu as pltpu

BF = jnp.bfloat16
F32 = jnp.float32

D_MODEL = 2048
DEPTH = 2
GROUP_W = 512
LANES = 128
POOL_WINDOWS = (2, 4, 8, 16)
POOL_BUF = 15
POOL_PRE = 16
GMLP_CHUNK = 128
NSA_DH = 128
NSA_H = 4
CMP_BLOCK = 32
SEL_BLOCK = 64
SEL_TOP = 16
WINDOW = 512
Q_BLOCK = 128
GLA_H = 4
GLA_DK = 64
GLA_DV = 128
GLA_RANK = 16
GLA_TAU = 16.0
GLA_CHUNK = 64
MEM_LEN = 256
MEM_H = 4
MEM_DH = 128
N_EXPERTS = 32
TOP_K = 4
D_FF = 2048
SWIGLU_LIMIT = 7.0
SWIGLU_ALPHA = 1.702
DN_ALPHA = (2.0 * DEPTH) ** 0.25
LN_EPS = 1e-5
NEG_INF = float("-inf")

CB_XA, CB_U, CB_V, CB_QC = 0, 4, 8, 12
CB_CMPK, CB_CMPV, CB_SELK, CB_SELV, CB_WINK, CB_WINV = 16, 17, 18, 19, 20, 21
CB_MISC, CB_QD, CB_KD, CB_VD, CB_GOD = 22, 24, 28, 32, 36
N_CB = 40
D_PROJ = N_CB * LANES
MISC_GATE0 = 0
MISC_LR0 = 12

MOE_TM = 512
MOE_TN = 512
MOE_TN_DOWN = 1024
MOE_TOK = 256
VMEM_LIMIT = 56 * 1024 * 1024


def _cparams(sem):
    return pltpu.CompilerParams(dimension_semantics=sem, vmem_limit_bytes=VMEM_LIMIT)


def _layer_norm(x, g, b):
    mu = jnp.mean(x, -1, keepdims=True)
    xc = x - mu
    var = jnp.mean(xc * xc, -1, keepdims=True)
    return xc * lax.rsqrt(var + LN_EPS) * g + b


def _dot(a, b):
    return jnp.dot(a, b, preferred_element_type=F32)


def _dot_nt(a, b):
    return lax.dot_general(a, b, (((1,), (1,)), ((), ())), preferred_element_type=F32)


def _dot_tn(a, b):
    return lax.dot_general(a, b, (((0,), (0,)), ((), ())), preferred_element_type=F32)


def _sigmoid(x):
    return 1.0 / (1.0 + jnp.exp(-x))


def _ln_kernel(x_ref, g_ref, b_ref, o_ref):
    o_ref[...] = _layer_norm(x_ref[...], g_ref[...], b_ref[...])


def ln_rows(x, g, b, tm):
    m, d = x.shape
    assert m % tm == 0, (m, tm)
    return pl.pallas_call(
        _ln_kernel,
        grid=(m // tm,),
        in_specs=[pl.BlockSpec((tm, d), lambda i: (i, 0)),
                  pl.BlockSpec((1, d), lambda i: (0, 0)),
                  pl.BlockSpec((1, d), lambda i: (0, 0))],
        out_specs=pl.BlockSpec((tm, d), lambda i: (i, 0)),
        out_shape=jax.ShapeDtypeStruct((m, d), F32),
        compiler_params=_cparams(("parallel",)),
        name="ln_rows",
    )(x, g.reshape(1, d), b.reshape(1, d))


def _mm_kernel(x_ref, w_ref, o_ref, xb_ref):
    @pl.when(pl.program_id(1) == 0)
    def _():
        xb_ref[...] = x_ref[...].astype(BF)

    o_ref[...] = _dot(xb_ref[...], w_ref[...]).astype(o_ref.dtype)


def matmul(x, w, tm, tn, out_dtype=F32):
    m, k = x.shape
    n = w.shape[1]
    assert m % tm == 0 and n % tn == 0, (m, tm, n, tn)
    return pl.pallas_call(
        _mm_kernel,
        grid=(m // tm, n // tn),
        in_specs=[pl.BlockSpec((tm, k), lambda i, j: (i, 0)),
                  pl.BlockSpec((k, tn), lambda i, j: (0, j))],
        out_specs=pl.BlockSpec((tm, tn), lambda i, j: (i, j)),
        out_shape=jax.ShapeDtypeStruct((m, n), out_dtype),
        scratch_shapes=[pltpu.VMEM((tm, k), BF)],
        compiler_params=_cparams(("parallel", "arbitrary")),
        name="matmul",
    )(x, w)


def _mm_res_ln_kernel(*refs, n_in):
    xs = refs[:n_in]
    w_ref, r_ref, g_ref, b_ref, o_ref = refs[n_in:]
    acc = None
    k0 = 0
    for x_ref in xs:
        kk = x_ref.shape[1]
        part = _dot(x_ref[...].astype(BF), w_ref[k0:k0 + kk, :])
        acc = part if acc is None else acc + part
        k0 += kk
    o_ref[...] = _layer_norm(DN_ALPHA * r_ref[...] + acc, g_ref[...], b_ref[...])


def matmul_res_ln(xs, w, resid, g, b, tm):
    m, d = resid.shape
    assert m % tm == 0, (m, tm)
    in_specs = [pl.BlockSpec((tm, x.shape[1]), lambda i: (i, 0)) for x in xs]
    in_specs += [pl.BlockSpec(w.shape, lambda i: (0, 0)),
                 pl.BlockSpec((tm, d), lambda i: (i, 0)),
                 pl.BlockSpec((1, d), lambda i: (0, 0)),
                 pl.BlockSpec((1, d), lambda i: (0, 0))]
    return pl.pallas_call(
        functools.partial(_mm_res_ln_kernel, n_in=len(xs)),
        grid=(m // tm,),
        in_specs=in_specs,
        out_specs=pl.BlockSpec((tm, d), lambda i: (i, 0)),
        out_shape=jax.ShapeDtypeStruct((m, d), F32),
        compiler_params=_cparams(("parallel",)),
        name="matmul_res_ln",
    )(*xs, w, resid, g.reshape(1, d), b.reshape(1, d))


def _pool_kernel(x_ref, pre_ref, w_ref, sc_ref, o_ref, ext_ref, *, bb, tt, n_tiles, t0):
    j = pl.program_id(1)

    @pl.when(j == 0)
    def _():
        ext_ref[:, 0:POOL_PRE, :] = pre_ref[...]

    x = x_ref[...].reshape(bb, tt, GROUP_W)
    ext_ref[:, POOL_PRE:, :] = x
    t_idx = j * tt + lax.broadcasted_iota(jnp.int32, (1, tt, LANES), 1)
    for g, win in enumerate(POOL_WINDOWS):
        cs = slice(g * LANES, (g + 1) * LANES)
        xg = ext_ref[:, POOL_PRE:POOL_PRE + tt, cs]
        s = xg
        for k in range(1, win):
            s = s + ext_ref[:, POOL_PRE - k:POOL_PRE - k + tt, cs]
        cnt = jnp.minimum(win, t0 + 1 + t_idx).astype(F32)
        d = s / cnt - xg
        y = _dot(d.reshape(bb * tt, LANES).astype(BF), w_ref[g])
        o_ref[:, cs] = (y * sc_ref[:, cs]).astype(o_ref.dtype)
    if n_tiles > 1:
        ext_ref[:, 0:POOL_PRE, :] = ext_ref[:, tt:tt + POOL_PRE, :]


def pool_mixer(proj, row0, n_b, t_len, prefix, pool_w, pool_scale, t0, bb, tt):
    n_tiles = t_len // tt
    rb = bb * tt
    base = row0 // rb
    return pl.pallas_call(
        functools.partial(_pool_kernel, bb=bb, tt=tt, n_tiles=n_tiles, t0=t0),
        grid=(n_b // bb, n_tiles),
        in_specs=[pl.BlockSpec((rb, GROUP_W), lambda i, j: (base + i * n_tiles + j, CB_XA // 4)),
                  pl.BlockSpec((bb, POOL_PRE, GROUP_W), lambda i, j: (i, 0, 0)),
                  pl.BlockSpec((4, LANES, LANES), lambda i, j: (0, 0, 0)),
                  pl.BlockSpec((1, GROUP_W), lambda i, j: (0, 0))],
        out_specs=pl.BlockSpec((rb, GROUP_W), lambda i, j: (i * n_tiles + j, 0)),
        out_shape=jax.ShapeDtypeStruct((n_b * t_len, GROUP_W), BF),
        scratch_shapes=[pltpu.VMEM((bb, POOL_PRE + tt, GROUP_W), F32)],
        compiler_params=_cparams(("parallel", "arbitrary")),
        name="pool_mixer",
    )(proj, prefix, pool_w.astype(BF), pool_scale.reshape(1, GROUP_W))


def _gelu_tanh(x):
    return 0.5 * x * (1.0 + jnp.tanh(np.sqrt(2.0 / np.pi) * (x + 0.044715 * (x * x * x))))


def _gmlp_kernel(u_ref, v_ref, g_ref, b_ref, w_ref, bias_ref, y_ref, vo_ref, *, n_ch):
    for c in range(n_ch):
        rows = slice(c * GMLP_CHUNK, (c + 1) * GMLP_CHUNK)
        u = _gelu_tanh(u_ref[rows, :])
        v = _layer_norm(_gelu_tanh(v_ref[rows, :]), g_ref[...], b_ref[...])
        vo_ref[rows, :] = v
        vb = v.astype(BF)
        for g in range(4):
            cs = slice(g * LANES, (g + 1) * LANES)
            mixed = _dot(w_ref[g], vb[:, cs]) + bias_ref[:, cs]
            y_ref[rows, cs] = (u[:, cs] * mixed).astype(y_ref.dtype)


def gmlp_mixer(proj, row0, n_rows, ln_g, ln_b, w_mix, bias_full, n_ch):
    rb = n_ch * GMLP_CHUNK
    base = row0 // rb
    return pl.pallas_call(
        functools.partial(_gmlp_kernel, n_ch=n_ch),
        grid=(n_rows // rb,),
        in_specs=[pl.BlockSpec((rb, GROUP_W), lambda i: (base + i, CB_U // 4)),
                  pl.BlockSpec((rb, GROUP_W), lambda i: (base + i, CB_V // 4)),
                  pl.BlockSpec((1, GROUP_W), lambda i: (0, 0)),
                  pl.BlockSpec((1, GROUP_W), lambda i: (0, 0)),
                  pl.BlockSpec((4, GMLP_CHUNK, GMLP_CHUNK), lambda i: (0, 0, 0)),
                  pl.BlockSpec((GMLP_CHUNK, GROUP_W), lambda i: (0, 0))],
        out_specs=[pl.BlockSpec((rb, GROUP_W), lambda i: (i, 0)),
                   pl.BlockSpec((rb, GROUP_W), lambda i: (i, 0))],
        out_shape=[jax.ShapeDtypeStruct((n_rows, GROUP_W), BF),
                   jax.ShapeDtypeStruct((n_rows, GROUP_W), F32)],
        compiler_params=_cparams(("parallel",)),
        name="gmlp_mixer",
    )(proj, proj, ln_g.reshape(1, GROUP_W), ln_b.reshape(1, GROUP_W), w_mix, bias_full)


def _cmp_accumulate(row_pair, phi_ref):
    acc = None
    for l2 in range(CMP_BLOCK // 2):
        xa, xb = row_pair(l2)
        part = _dot(jnp.concatenate([xa, xb], axis=1).astype(BF), phi_ref[l2])
        acc = part if acc is None else acc + part
    return acc


def _cmp_kernel(xk_ref, xv_ref, phik_ref, phiv_ref, ok_ref, ov_ref, *, tb):
    for x_ref, phi_ref, o_ref in ((xk_ref, phik_ref, ok_ref), (xv_ref, phiv_ref, ov_ref)):
        o_ref[...] = _cmp_accumulate(
            lambda l2: (x_ref[pl.ds(2 * l2, tb, stride=CMP_BLOCK), :],
                        x_ref[pl.ds(2 * l2 + 1, tb, stride=CMP_BLOCK), :]), phi_ref)


def cmp_project(proj, n_blk, phik2, phiv2, tb):
    assert n_blk % tb == 0, (n_blk, tb)
    rows = tb * CMP_BLOCK
    return pl.pallas_call(
        functools.partial(_cmp_kernel, tb=tb),
        grid=(n_blk // tb,),
        in_specs=[pl.BlockSpec((rows, LANES), lambda i: (i, CB_CMPK)),
                  pl.BlockSpec((rows, LANES), lambda i: (i, CB_CMPV)),
                  pl.BlockSpec(phik2.shape, lambda i: (0, 0, 0)),
                  pl.BlockSpec(phiv2.shape, lambda i: (0, 0, 0))],
        out_specs=[pl.BlockSpec((tb, NSA_DH), lambda i: (i, 0)),
                   pl.BlockSpec((tb, NSA_DH), lambda i: (i, 0))],
        out_shape=[jax.ShapeDtypeStruct((n_blk, NSA_DH), F32),
                   jax.ShapeDtypeStruct((n_blk, NSA_DH), F32)],
        compiler_params=_cparams(("parallel",)),
        name="cmp_project",
    )(proj, proj, phik2, phiv2)


def _cmp_pool_kernel(x_ref, phik_ref, phiv_ref, o_ref):
    for s, phi_ref in ((0, phik_ref), (1, phiv_ref)):
        o_ref[:, s * NSA_DH:(s + 1) * NSA_DH] = _cmp_accumulate(
            lambda l2: (x_ref[:, 2 * l2, s, :], x_ref[:, 2 * l2 + 1, s, :]), phi_ref)


def cmp_pool(cache_blocks, blk0, n_blk, phik2, phiv2, tb):
    assert n_blk % tb == 0 and blk0 % tb == 0, (n_blk, blk0, tb)
    base = blk0 // tb
    return pl.pallas_call(
        _cmp_pool_kernel,
        grid=(n_blk // tb,),
        in_specs=[pl.BlockSpec((tb, CMP_BLOCK, 4, NSA_DH), lambda i: (base + i, 0, 0, 0)),
                  pl.BlockSpec(phik2.shape, lambda i: (0, 0, 0)),
                  pl.BlockSpec(phiv2.shape, lambda i: (0, 0, 0))],
        out_specs=pl.BlockSpec((tb, 2 * NSA_DH), lambda i: (i, 0)),
        out_shape=jax.ShapeDtypeStruct((n_blk, 2 * NSA_DH), F32),
        compiler_params=_cparams(("parallel",)),
        name="cmp_pool",
    )(cache_blocks, phik2, phiv2)


def _alibi_slope(h):
    return 2.0 ** (-8.0 * (h + 1) / NSA_H)


def _head_rows_const(n_q, fn):
    r = lax.broadcasted_iota(jnp.int32, (NSA_H * n_q, 1), 0)
    out = jnp.full((NSA_H * n_q, 1), fn(NSA_H - 1), F32)
    for h in range(NSA_H - 2, -1, -1):
        out = jnp.where(r < (h + 1) * n_q, fn(h), out)
    return out


def _stack_heads(q):
    return jnp.concatenate([q[:, h * NSA_DH:(h + 1) * NSA_DH] for h in range(NSA_H)], axis=0)


def _select_blocks(score, n_blk):
    blk = lax.broadcasted_iota(jnp.int32, score.shape, 0)
    cnt = jnp.zeros(score.shape, F32)
    for i in range(n_blk):
        row = score[i:i + 1, :]
        beats = (row > score) | ((row == score) & (blk > i))
        cnt = cnt + jnp.where(beats, 1.0, 0.0)
    return jnp.where((cnt < float(SEL_TOP)) & (score > NEG_INF), 1.0, 0.0)


def _softmax_segments(segs):
    m = None
    for s, _ in segs:
        mi = jnp.max(s, axis=1, keepdims=True)
        m = mi if m is None else jnp.maximum(m, mi)
    m = jnp.where(m > NEG_INF, m, 0.0)
    den = None
    acc = None
    for s, v in segs:
        e = jnp.exp(s - m)
        d = jnp.sum(e, axis=1, keepdims=True)
        a = _dot(e.astype(BF), v)
        den = d if den is None else den + d
        acc = a if acc is None else acc + a
    return acc / jnp.where(den > 0, den, 1.0)


def _gated_heads(o_ref, gates, o_c, o_s, o_w, n_q):
    for h in range(NSA_H):
        rows = slice(h * n_q, (h + 1) * n_q)
        c = MISC_GATE0 + 3 * h
        out = (gates[:, c:c + 1] * o_c[rows, :] + gates[:, c + 1:c + 2] * o_s[rows, :]
               + gates[:, c + 2:c + 3] * o_w[rows, :])
        o_ref[:, h * NSA_DH:(h + 1) * NSA_DH] = out.astype(o_ref.dtype)


def _nsa_prompt_kernel(q_ref, misc_ref, kc_ref, vc_ref, ks_ref, vs_ref, kw_ref, vw_ref, e_ref, o_ref,
                       imp_ref, mask_ref, m_ref, l_ref, acc_ref, *, t_len):
    n_q = Q_BLOCK
    n_cmp = t_len // CMP_BLOCK
    n_sel = t_len // SEL_BLOCK
    kt = 512
    q0 = pl.program_id(1) * n_q
    q4 = _stack_heads(q_ref[...] * (NSA_DH ** -0.5)).astype(BF)
    slope_r = _head_rows_const(n_q, _alibi_slope)
    qpos_r = q0 + lax.rem(lax.broadcasted_iota(jnp.int32, (NSA_H * n_q, 1), 0), n_q)

    lane = lax.broadcasted_iota(jnp.int32, (1, NSA_H * n_q), 1)
    qpos_l = q0 + lax.rem(lane, n_q)
    slope_l = jnp.full((1, NSA_H * n_q), _alibi_slope(NSA_H - 1), F32)
    for h in range(NSA_H - 2, -1, -1):
        slope_l = jnp.where(lane < (h + 1) * n_q, _alibi_slope(h), slope_l)
    cmp_end = lax.broadcasted_iota(jnp.int32, (n_cmp, 1), 0) * CMP_BLOCK + (CMP_BLOCK - 1)
    s_c = _dot_nt(kc_ref[...].astype(BF), q4) - slope_l * (qpos_l - cmp_end).astype(F32)
    s_c = jnp.where(cmp_end <= qpos_l, s_c, NEG_INF)
    m_c = jnp.max(s_c, axis=0, keepdims=True)
    m_c = jnp.where(m_c > NEG_INF, m_c, 0.0)
    e_c = jnp.exp(s_c - m_c)
    den_c = jnp.sum(e_c, axis=0, keepdims=True)
    p_c = e_c / jnp.where(den_c > 0, den_c, 1.0)
    o_c = _dot_tn(p_c.astype(BF), vc_ref[...].astype(BF))

    imp = p_c[:, 0:n_q]
    for h in range(1, NSA_H):
        imp = imp + p_c[:, h * n_q:(h + 1) * n_q]
    imp_ref[...] = imp
    imp = imp_ref[pl.ds(0, n_sel, stride=2), :] + imp_ref[pl.ds(1, n_sel, stride=2), :]
    blk = lax.broadcasted_iota(jnp.int32, (n_sel, 1), 0)
    cur = (q0 + lax.broadcasted_iota(jnp.int32, (1, n_q), 1)) // SEL_BLOCK
    forced = (blk == 0) | (blk == cur) | (blk == cur - 1)
    score = jnp.where(blk <= cur, jnp.where(forced, jnp.inf, imp), NEG_INF)
    sel = _select_blocks(score, n_sel)
    mask_ref[...] = _dot_tn(sel.astype(BF), e_ref[...])

    m_ref[...] = jnp.full(m_ref.shape, NEG_INF, F32)
    l_ref[...] = jnp.zeros(l_ref.shape, F32)
    acc_ref[...] = jnp.zeros(acc_ref.shape, F32)

    def sel_tile(t, carry):
        k0 = pl.multiple_of(t * kt, kt)
        s = _dot_nt(q4, ks_ref[pl.ds(k0, kt), :].astype(BF))
        d = qpos_r - (k0 + lax.broadcasted_iota(jnp.int32, (1, kt), 1))
        mk = mask_ref[:, pl.ds(k0, kt)]
        mk = jnp.concatenate([mk] * NSA_H, axis=0)
        s = jnp.where((d >= 0) & (mk > 0.5), s - slope_r * d.astype(F32), NEG_INF)
        m_old = m_ref[...]
        m_new = jnp.maximum(m_old, jnp.max(s, axis=1, keepdims=True))
        m_use = jnp.where(m_new > NEG_INF, m_new, 0.0)
        alpha = jnp.exp(m_old - m_use)
        p = jnp.exp(s - m_use)
        l_ref[...] = alpha * l_ref[...] + jnp.sum(p, axis=1, keepdims=True)
        acc_ref[...] = alpha * acc_ref[...] + _dot(p.astype(BF), vs_ref[pl.ds(k0, kt), :].astype(BF))
        m_ref[...] = m_new
        return carry

    lax.fori_loop(0, (q0 + n_q + kt - 1) // kt, sel_tile, 0)
    l_s = l_ref[...]
    o_s = acc_ref[...] / jnp.where(l_s > 0, l_s, 1.0)

    nw = WINDOW + n_q
    w0 = pl.multiple_of(jnp.maximum(q0 - WINDOW, 0), n_q)
    s_w = _dot_nt(q4, kw_ref[pl.ds(w0, nw), :].astype(BF))
    d_w = qpos_r - (w0 + lax.broadcasted_iota(jnp.int32, (1, nw), 1))
    s_w = jnp.where((d_w >= 0) & (d_w < WINDOW), s_w - slope_r * d_w.astype(F32), NEG_INF)
    o_w = _softmax_segments([(s_w, vw_ref[pl.ds(w0, nw), :].astype(BF))])

    _gated_heads(o_ref, _sigmoid(misc_ref[...]), o_c, o_s, o_w, n_q)


def nsa_prompt(proj, n_b, t_len, k_cmp, v_cmp, expand):
    n_qb = t_len // Q_BLOCK
    n_cmp = t_len // CMP_BLOCK
    n_sel = t_len // SEL_BLOCK
    seq = lambda cb: pl.BlockSpec((t_len, LANES), lambda b, i: (b, cb))
    return pl.pallas_call(
        functools.partial(_nsa_prompt_kernel, t_len=t_len),
        grid=(n_b, n_qb),
        in_specs=[pl.BlockSpec((Q_BLOCK, GROUP_W), lambda b, i: (b * n_qb + i, CB_QC // 4)),
                  pl.BlockSpec((Q_BLOCK, LANES), lambda b, i: (b * n_qb + i, CB_MISC)),
                  pl.BlockSpec((n_cmp, NSA_DH), lambda b, i: (b, 0)),
                  pl.BlockSpec((n_cmp, NSA_DH), lambda b, i: (b, 0)),
                  seq(CB_SELK), seq(CB_SELV), seq(CB_WINK), seq(CB_WINV),
                  pl.BlockSpec((n_sel, t_len), lambda b, i: (0, 0))],
        out_specs=pl.BlockSpec((Q_BLOCK, GROUP_W), lambda b, i: (b * n_qb + i, 0)),
        out_shape=jax.ShapeDtypeStruct((n_b * t_len, GROUP_W), BF),
        scratch_shapes=[pltpu.VMEM((n_cmp, Q_BLOCK), F32),
                        pltpu.VMEM((Q_BLOCK, t_len), F32),
                        pltpu.VMEM((NSA_H * Q_BLOCK, 1), F32),
                        pltpu.VMEM((NSA_H * Q_BLOCK, 1), F32),
                        pltpu.VMEM((NSA_H * Q_BLOCK, NSA_DH), F32)],
        compiler_params=_cparams(("parallel", "arbitrary")),
        name="nsa_prompt",
    )(proj, proj, k_cmp, v_cmp, proj, proj, proj, proj, expand)


def _nsa_sample_kernel(pt_ref, q_ref, misc_ref, new4_ref, new2_ref, winc_ref, pair_ref, e_ref,
                       *rest, n_pages, page, t_q):
    pages = rest[:n_pages]
    kvcs = rest[n_pages:2 * n_pages]
    o_ref, kvc_ref, ks_ref, vs_ref = rest[2 * n_pages:]
    past_len = n_pages * page
    cpp = page // CMP_BLOCK
    n_cmp = past_len // CMP_BLOCK
    n_sel = past_len // SEL_BLOCK + 1
    n_r = NSA_H * t_q
    for j in range(n_pages):
        kvc_ref[j * cpp:(j + 1) * cpp, :] = kvcs[j][...]
        ks_ref[j * page:(j + 1) * page, :] = pages[j][:, 2, :].astype(BF)
        vs_ref[j * page:(j + 1) * page, :] = pages[j][:, 3, :].astype(BF)
    kc_ref = kvc_ref.at[:, 0:NSA_DH]
    vc_ref = kvc_ref.at[:, NSA_DH:2 * NSA_DH]
    ksn_ref = new4_ref.at[:, 2 * NSA_DH:3 * NSA_DH]
    vsn_ref = new4_ref.at[:, 3 * NSA_DH:4 * NSA_DH]
    kwn_ref = new2_ref.at[:, 0:NSA_DH]
    vwn_ref = new2_ref.at[:, NSA_DH:2 * NSA_DH]

    q4 = _stack_heads(q_ref[...] * (NSA_DH ** -0.5)).astype(BF)
    slope_r = _head_rows_const(t_q, _alibi_slope)
    qpos_r = past_len + lax.rem(lax.broadcasted_iota(jnp.int32, (n_r, 1), 0), t_q)

    cmp_end = lax.broadcasted_iota(jnp.int32, (1, n_cmp), 1) * CMP_BLOCK + (CMP_BLOCK - 1)
    s_c = _dot_nt(q4, kc_ref[...].astype(BF)) - slope_r * (qpos_r - cmp_end).astype(F32)
    s_c = jnp.where(cmp_end <= qpos_r, s_c, NEG_INF)
    m_c = jnp.max(s_c, axis=1, keepdims=True)
    m_c = jnp.where(m_c > NEG_INF, m_c, 0.0)
    e_c = jnp.exp(s_c - m_c)
    den_c = jnp.sum(e_c, axis=1, keepdims=True)
    p_c = e_c / jnp.where(den_c > 0, den_c, 1.0)
    o_c = _dot(p_c.astype(BF), vc_ref[...].astype(BF))

    imp = p_c[0:t_q, :]
    for h in range(1, NSA_H):
        imp = imp + p_c[h * t_q:(h + 1) * t_q, :]
    hi = imp.astype(BF)
    lo = (imp - hi.astype(F32)).astype(BF)
    imp = _dot(hi, pair_ref[...]) + _dot(lo, pair_ref[...])
    blk = lax.broadcasted_iota(jnp.int32, (1, LANES), 1)
    cur = (past_len + lax.broadcasted_iota(jnp.int32, (t_q, 1), 0)) // SEL_BLOCK
    forced = (blk == 0) | (blk == cur) | (blk == cur - 1)
    score = jnp.where(blk <= cur, jnp.where(forced, jnp.inf, imp), NEG_INF)
    cnt = jnp.zeros(score.shape, F32)
    for i in range(n_sel):
        col = score[:, i:i + 1]
        beats = (col > score) | ((col == score) & (blk > i))
        cnt = cnt + jnp.where(beats, 1.0, 0.0)
    sel = jnp.where((cnt < float(SEL_TOP)) & (score > NEG_INF), 1.0, 0.0)
    mask = _dot(sel.astype(BF), e_ref[...])
    mask = jnp.concatenate([mask] * NSA_H, axis=0)

    d_p = qpos_r - lax.broadcasted_iota(jnp.int32, (1, past_len), 1)
    s_p = jnp.where(mask[:, 0:past_len] > 0.5, _dot_nt(q4, ks_ref[...]) - slope_r * d_p.astype(F32), NEG_INF)
    d_n = qpos_r - (past_len + lax.broadcasted_iota(jnp.int32, (1, t_q), 1))
    s_n = _dot_nt(q4, ksn_ref[...].astype(BF)) - slope_r * d_n.astype(F32)
    s_n = jnp.where((d_n >= 0) & (mask[:, past_len:past_len + t_q] > 0.5), s_n, NEG_INF)
    o_s = _softmax_segments([(s_p, vs_ref[...]), (s_n, vsn_ref[...].astype(BF))])

    n_wc = winc_ref.shape[0]
    d_c = qpos_r - (past_len - n_wc + lax.broadcasted_iota(jnp.int32, (1, n_wc), 1))
    s_wc = _dot_nt(q4, winc_ref[:, 0, :].astype(BF)) - slope_r * d_c.astype(F32)
    s_wc = jnp.where(d_c < WINDOW, s_wc, NEG_INF)
    s_wn = _dot_nt(q4, kwn_ref[...].astype(BF)) - slope_r * d_n.astype(F32)
    s_wn = jnp.where(d_n >= 0, s_wn, NEG_INF)
    o_w = _softmax_segments([(s_wc, winc_ref[:, 1, :].astype(BF)), (s_wn, vwn_ref[...].astype(BF))])

    _gated_heads(o_ref, _sigmoid(misc_ref[...]), o_c, o_s, o_w, t_q)


def nsa_sample(proj, row0, n_b, t_q, cache_nsa, layer, page_table, kvc_pool, cache_win, pair, expand):
    n_pages = page_table.shape[1]
    page = cache_nsa.shape[2]
    cpp = page // CMP_BLOCK
    past_len = n_pages * page
    n_wc = cache_win.shape[2]
    base = row0 // t_q
    assert CB_SELV == CB_CMPK + 3 and CB_CMPK % 4 == 0 and CB_WINV == CB_WINK + 1 and CB_WINK % 2 == 0

    def page_spec(j):
        return pl.BlockSpec((None, None, page, 4, NSA_DH), lambda b, pt: (layer, pt[b, j], 0, 0, 0))

    def cmp_spec(j):
        return pl.BlockSpec((None, cpp, 2 * NSA_DH), lambda b, pt: (pt[b, j], 0, 0))

    in_specs = [pl.BlockSpec((t_q, GROUP_W), lambda b, pt: (base + b, CB_QC // 4)),
                pl.BlockSpec((t_q, LANES), lambda b, pt: (base + b, CB_MISC)),
                pl.BlockSpec((t_q, 4 * NSA_DH), lambda b, pt: (base + b, CB_CMPK // 4)),
                pl.BlockSpec((t_q, 2 * NSA_DH), lambda b, pt: (base + b, CB_WINK // 2)),
                pl.BlockSpec((None, None, n_wc, 2, NSA_DH), lambda b, pt: (layer, b, 0, 0, 0)),
                pl.BlockSpec(pair.shape, lambda b, pt: (0, 0)),
                pl.BlockSpec(expand.shape, lambda b, pt: (0, 0))]
    in_specs += [page_spec(j) for j in range(n_pages)]
    in_specs += [cmp_spec(j) for j in range(n_pages)]
    return pl.pallas_call(
        functools.partial(_nsa_sample_kernel, n_pages=n_pages, page=page, t_q=t_q),
        grid_spec=pltpu.PrefetchScalarGridSpec(
            num_scalar_prefetch=1,
            grid=(n_b,),
            in_specs=in_specs,
            out_specs=pl.BlockSpec((t_q, GROUP_W), lambda b, pt: (b, 0)),
            scratch_shapes=[pltpu.VMEM((past_len // CMP_BLOCK, 2 * NSA_DH), F32),
                            pltpu.VMEM((past_len, NSA_DH), BF),
                            pltpu.VMEM((past_len, NSA_DH), BF)]),
        out_shape=jax.ShapeDtypeStruct((n_b * t_q, GROUP_W), BF),
        compiler_params=_cparams(("arbitrary",)),
        name="nsa_sample",
    )(page_table, proj, proj, proj, proj, cache_win, pair, expand,
      *([cache_nsa] * n_pages), *([kvc_pool] * n_pages))


def _log_sigmoid(z):
    return jnp.minimum(z, 0.0) - jnp.log(1.0 + jnp.exp(-jnp.abs(z)))


def _cumsum_rows(x, tri):
    hi = x.astype(BF)
    lo = (x - hi.astype(F32)).astype(BF)
    return _dot(tri, hi) + _dot(tri, lo)


def _rows_to_col(row):
    n = row.shape[1]
    eye = lax.broadcasted_iota(jnp.int32, (n, n), 0) == lax.broadcasted_iota(jnp.int32, (n, n), 1)
    return jnp.sum(jnp.where(eye, row, 0.0), axis=1, keepdims=True)


def _gla_out(o, god, ng):
    o = o * lax.rsqrt(jnp.mean(o * o, -1, keepdims=True) + LN_EPS) * ng
    return o * (god * _sigmoid(god))


def _gla_prompt_kernel(q_ref, k_ref, v_ref, god_ref, misc_ref, wa_ref, ba_ref, ng_ref, o_ref, s_ref, st_ref,
                       *, n_chunks):
    c = GLA_CHUNK
    st_ref[...] = jnp.zeros(st_ref.shape, F32)
    r_i = lax.broadcasted_iota(jnp.int32, (c, c), 0)
    c_i = lax.broadcasted_iota(jnp.int32, (c, c), 1)
    causal = r_i >= c_i
    tri = jnp.where(causal, 1.0, 0.0).astype(BF)

    def chunk(n, carry):
        rows = pl.ds(pl.multiple_of(n * c, c), c)
        misc = misc_ref[rows, :].astype(BF)
        for h in range(GLA_H):
            cs = slice(h * LANES, (h + 1) * LANES)
            z = _dot(misc, wa_ref[:, cs]) + ba_ref[:, cs]
            bc = _cumsum_rows(_log_sigmoid(z) / GLA_TAU, tri)
            bl = bc[c - 1:c, :]
            q = q_ref[rows, cs] * (GLA_DK ** -0.5)
            k = k_ref[rows, cs]
            v = v_ref[rows, cs].astype(BF)
            qe = (q * jnp.exp(bc)).astype(BF)
            att = jnp.where(causal, _dot_nt(qe, (k * jnp.exp(-bc)).astype(BF)), 0.0)
            st = st_ref[h]
            o = _dot(qe, st.astype(BF)) + _dot(att.astype(BF), v)
            st_ref[h] = _rows_to_col(jnp.exp(bl)) * st + _dot_tn((k * jnp.exp(bl - bc)).astype(BF), v)
            o_ref[rows, cs] = _gla_out(o, god_ref[rows, cs], ng_ref[...]).astype(o_ref.dtype)
        return carry

    lax.fori_loop(0, n_chunks, chunk, 0)
    for h in range(GLA_H):
        s_ref[h] = st_ref[h, 0:GLA_DK, :]


def gla_prompt(proj, n_b, t_len, wa_pad, ba_pad, norm_g):
    w = GLA_H * LANES
    seq = lambda cb0: pl.BlockSpec((t_len, w), lambda b: (b, cb0 // GLA_H))
    return pl.pallas_call(
        functools.partial(_gla_prompt_kernel, n_chunks=t_len // GLA_CHUNK),
        grid=(n_b,),
        in_specs=[seq(CB_QD), seq(CB_KD), seq(CB_VD), seq(CB_GOD),
                  pl.BlockSpec((t_len, LANES), lambda b: (b, CB_MISC)),
                  pl.BlockSpec((LANES, w), lambda b: (0, 0)),
                  pl.BlockSpec((1, w), lambda b: (0, 0)),
                  pl.BlockSpec((1, GLA_DV), lambda b: (0, 0))],
        out_specs=[pl.BlockSpec((t_len, w), lambda b: (b, 0)),
                   pl.BlockSpec((None, GLA_H, GLA_DK, GLA_DV), lambda b: (b, 0, 0, 0))],
        out_shape=[jax.ShapeDtypeStruct((n_b * t_len, GROUP_W), BF),
                   jax.ShapeDtypeStruct((n_b, GLA_H, GLA_DK, GLA_DV), F32)],
        scratch_shapes=[pltpu.VMEM((GLA_H, LANES, GLA_DV), F32)],
        compiler_params=_cparams(("parallel",)),
        name="gla_prompt",
    )(proj, proj, proj, proj, proj, wa_pad, ba_pad, norm_g.reshape(1, GLA_DV))


def _gla_sample_kernel(q_ref, k_ref, v_ref, god_ref, misc_ref, wa_ref, ba_ref, ng_ref, s0_ref, o_ref, s_ref,
                       *, bb, t_q):
    n_r = bb * t_q
    r_i = lax.broadcasted_iota(jnp.int32, (n_r, n_r), 0)
    c_i = lax.broadcasted_iota(jnp.int32, (n_r, n_r), 1)
    causal = (r_i >= c_i) & (r_i // t_q == c_i // t_q)
    tri = jnp.where(causal, 1.0, 0.0).astype(BF)
    row_b = lax.broadcasted_iota(jnp.int32, (n_r, 1), 0) // t_q

    z = _dot(misc_ref[...].astype(BF), wa_ref[...]) + ba_ref[...]
    bc = _cumsum_rows(_log_sigmoid(z) / GLA_TAU, tri)
    q = q_ref[...] * (GLA_DK ** -0.5)
    k = k_ref[...]
    v = v_ref[...].astype(BF)
    qe = q * jnp.exp(bc)
    att = jnp.where(causal, _dot_nt(qe.astype(BF), (k * jnp.exp(-bc)).astype(BF)), 0.0)
    o = _dot(att.astype(BF), v)
    for b in range(bb):
        mine = row_b == b
        bl = bc[(b + 1) * t_q - 1:(b + 1) * t_q, :]
        s0 = s0_ref[b, 0]
        o = o + _dot(jnp.where(mine, qe, 0.0)[:, 0:GLA_DK].astype(BF), s0.astype(BF))
        k2 = jnp.where(mine, k * jnp.exp(bl - bc), 0.0).astype(BF)
        upd = _dot_tn(k2, v)
        s_ref[b, 0] = _rows_to_col(jnp.exp(bl))[0:GLA_DK, :] * s0 + upd[0:GLA_DK, :]
    o_ref[...] = _gla_out(o, god_ref[...], ng_ref[...]).astype(o_ref.dtype)


def gla_sample(proj, row0, n_b, t_q, wa_pad, ba_pad, norm_g, s0, s0_b0, bb):
    rb = bb * t_q
    base = row0 // rb
    s_i0 = s0_b0 // bb
    seq = lambda cb0: pl.BlockSpec((rb, LANES), lambda i, h: (base + i, cb0 + h))
    return pl.pallas_call(
        functools.partial(_gla_sample_kernel, bb=bb, t_q=t_q),
        grid=(n_b // bb, GLA_H),
        in_specs=[seq(CB_QD), seq(CB_KD), seq(CB_VD), seq(CB_GOD),
                  pl.BlockSpec((rb, LANES), lambda i, h: (base + i, CB_MISC)),
                  pl.BlockSpec((LANES, LANES), lambda i, h: (0, h)),
                  pl.BlockSpec((1, LANES), lambda i, h: (0, h)),
                  pl.BlockSpec((1, GLA_DV), lambda i, h: (0, 0)),
                  pl.BlockSpec((bb, 1, GLA_DK, GLA_DV), lambda i, h: (s_i0 + i, h, 0, 0))],
        out_specs=[pl.BlockSpec((rb, GLA_DV), lambda i, h: (i, h)),
                   pl.BlockSpec((bb, 1, GLA_DK, GLA_DV), lambda i, h: (i, h, 0, 0))],
        out_shape=[jax.ShapeDtypeStruct((n_b * t_q, GROUP_W), BF),
                   jax.ShapeDtypeStruct((n_b, GLA_H, GLA_DK, GLA_DV), F32)],
        compiler_params=_cparams(("parallel", "parallel")),
        name="gla_sample",
    )(proj, proj, proj, proj, proj, wa_pad, ba_pad, norm_g.reshape(1, GLA_DV), s0)


def _mem_attn_kernel(q_ref, kv_ref, o_ref):
    q = q_ref[...] * (MEM_DH ** -0.5)
    for h in range(MEM_H):
        cs = slice(h * MEM_DH, (h + 1) * MEM_DH)
        s = _dot_nt(q[:, cs].astype(BF), kv_ref[:, 0, h, :].astype(BF))
        e = jnp.exp(s - jnp.max(s, axis=1, keepdims=True))
        p = e / jnp.sum(e, axis=1, keepdims=True)
        o_ref[:, cs] = _dot(p.astype(BF), kv_ref[:, 1, h, :].astype(BF)).astype(o_ref.dtype)


def mem_attention(qm, row0, n_b, t_len, kv, layer, tq):
    n_t = t_len // tq
    base = row0 // tq
    w = MEM_H * MEM_DH
    return pl.pallas_call(
        _mem_attn_kernel,
        grid=(n_b, n_t),
        in_specs=[pl.BlockSpec((tq, w), lambda b, i: (base + b * n_t + i, 0)),
                  pl.BlockSpec((None, None, MEM_LEN, 2, MEM_H, MEM_DH), lambda b, i: (layer, b, 0, 0, 0, 0))],
        out_specs=pl.BlockSpec((tq, w), lambda b, i: (b * n_t + i, 0)),
        out_shape=jax.ShapeDtypeStruct((n_b * t_len, w), BF),
        compiler_params=_cparams(("parallel", "parallel")),
        name="mem_attention",
    )(qm, kv)


def _router_kernel(x_ref, w_ref, b_ref, tri_ref, idx_ref, gate_ref, rank_ref, cnt_ref, run_ref):
    @pl.when(pl.program_id(0) == 0)
    def _():
        run_ref[...] = jnp.zeros(run_ref.shape, F32)

    logits = jnp.dot(x_ref[...], w_ref[...], precision=lax.Precision.HIGHEST,
                     preferred_element_type=F32) + b_ref[...]
    lane = lax.broadcasted_iota(jnp.int32, logits.shape, 1).astype(F32)
    vals = jnp.where(lane < N_EXPERTS, logits, NEG_INF)
    idx_out = jnp.zeros(logits.shape, F32)
    e_out = jnp.zeros(logits.shape, F32)
    den = None
    top0 = None
    onehots = []
    for k in range(TOP_K):
        m = jnp.max(vals, axis=1, keepdims=True)
        idx = jnp.min(jnp.where(vals == m, lane, float(LANES)), axis=1, keepdims=True)
        if k == 0:
            top0 = m
        e = jnp.exp(m - top0)
        den = e if den is None else den + e
        idx_out = jnp.where(lane == k, idx, idx_out)
        e_out = jnp.where(lane == k, e, e_out)
        onehots.append(jnp.where(lane == idx, 1.0, 0.0))
        vals = jnp.where(lane == idx, NEG_INF, vals)
    idx_ref[...] = idx_out.astype(jnp.int32)
    gate_ref[...] = e_out / den

    chosen = onehots[0] + onehots[1] + onehots[2] + onehots[3]
    before = _dot(tri_ref[...], chosen.astype(BF)) + run_ref[...]
    rank_out = jnp.zeros(logits.shape, F32)
    for k in range(TOP_K):
        r = jnp.sum(before * onehots[k], axis=1, keepdims=True)
        rank_out = jnp.where(lane == k, r, rank_out)
    rank_ref[...] = rank_out.astype(jnp.int32)
    run_ref[...] = run_ref[...] + jnp.sum(chosen, axis=0, keepdims=True)
    cnt_ref[...] = run_ref[...]


def moe_router(x, w_pad, b_pad, tm):
    m, d = x.shape
    assert m % tm == 0, (m, tm)
    tri = jnp.asarray(np.tril(np.ones((tm, tm), np.float32), -1), BF)
    row = lambda dt: jax.ShapeDtypeStruct((m, LANES), dt)
    return pl.pallas_call(
        _router_kernel,
        grid=(m // tm,),
        in_specs=[pl.BlockSpec((tm, d), lambda i: (i, 0)),
                  pl.BlockSpec((d, LANES), lambda i: (0, 0)),
                  pl.BlockSpec((1, LANES), lambda i: (0, 0)),
                  pl.BlockSpec((tm, tm), lambda i: (0, 0))],
        out_specs=[pl.BlockSpec((tm, LANES), lambda i: (i, 0)),
                   pl.BlockSpec((tm, LANES), lambda i: (i, 0)),
                   pl.BlockSpec((tm, LANES), lambda i: (i, 0)),
                   pl.BlockSpec((1, LANES), lambda i: (0, 0))],
        out_shape=[row(jnp.int32), row(F32), row(jnp.int32), jax.ShapeDtypeStruct((1, LANES), F32)],
        scratch_shapes=[pltpu.VMEM((1, LANES), F32)],
        compiler_params=_cparams(("arbitrary",)),
        name="moe_router",
    )(x, w_pad, b_pad, tri)


def _slot_ids_to_smem(slot_ref, ids_ref, sem):
    cp = pltpu.make_async_copy(slot_ref.at[0], ids_ref, sem)
    cp.start()
    cp.wait()


def _dispatch_kernel(pad0_ref, npad_ref, slot_ref, x_ref, xg_ref, ids_ref, zero_ref, sem_ids, sem, sem_z, *, tn):
    @pl.when(pl.program_id(0) == 0)
    def _():
        zero_ref[...] = jnp.zeros(zero_ref.shape, zero_ref.dtype)

        def per_expert(e, carry):
            def start(r, c):
                pltpu.make_async_copy(zero_ref.at[pl.ds(0, 1), :], xg_ref.at[pl.ds(pad0_ref[e] + r, 1), :],
                                      sem_z).start()
                return c

            def wait(r, c):
                pltpu.make_async_copy(zero_ref.at[pl.ds(0, 1), :], xg_ref.at[pl.ds(0, 1), :], sem_z).wait()
                return c

            lax.fori_loop(0, npad_ref[e], start, 0)
            lax.fori_loop(0, npad_ref[e], wait, 0)
            return carry

        lax.fori_loop(0, pad0_ref.shape[0], per_expert, 0)

    _slot_ids_to_smem(slot_ref, ids_ref, sem_ids)

    def issue(t, carry):
        for k in range(TOP_K):
            pltpu.make_async_copy(x_ref.at[pl.ds(t, 1), :],
                                  xg_ref.at[pl.ds(ids_ref[0, t * TOP_K + k], 1), :], sem).start()
        return carry

    lax.fori_loop(0, tn, issue, 0, unroll=8)
    for k in range(TOP_K):
        pltpu.make_async_copy(x_ref, xg_ref.at[pl.ds(0, tn), :], sem).wait()


def moe_dispatch(x, slots, pad0, npad, cap):
    n, d = x.shape
    tn = MOE_TOK
    assert n % tn == 0, (n, tn)
    return pl.pallas_call(
        functools.partial(_dispatch_kernel, tn=tn),
        grid_spec=pltpu.PrefetchScalarGridSpec(
            num_scalar_prefetch=2,
            grid=(n // tn,),
            in_specs=[pl.BlockSpec((1, 1, TOP_K * tn), lambda i, p0, np_: (i, 0, 0)),
                      pl.BlockSpec((tn, d), lambda i, p0, np_: (i, 0))],
            out_specs=pl.BlockSpec(memory_space=pl.ANY),
            scratch_shapes=[pltpu.SMEM((1, TOP_K * tn), jnp.int32),
                            pltpu.VMEM((8, d), x.dtype),
                            pltpu.SemaphoreType.DMA(()),
                            pltpu.SemaphoreType.DMA(()),
                            pltpu.SemaphoreType.DMA(())]),
        out_shape=jax.ShapeDtypeStruct((cap, d), x.dtype),
        compiler_params=_cparams(("arbitrary",)),
        name="moe_dispatch",
    )(pad0, npad, slots.reshape(n // tn, 1, TOP_K * tn), x)


def _moe_up_kernel(be_ref, nu_ref, x_ref, wg_ref, wu_ref, bg_ref, bu_ref, o_ref, wgb_ref, wub_ref):
    i = pl.program_id(1)
    changed = jnp.logical_or(i == 0, be_ref[i] != be_ref[jnp.maximum(i - 1, 0)])

    @pl.when(changed)
    def _():
        wgb_ref[...] = wg_ref[...].astype(BF)
        wub_ref[...] = wu_ref[...].astype(BF)

    @pl.when(i < nu_ref[0])
    def _():
        x = x_ref[...].astype(BF)
        g = jnp.minimum(_dot(x, wgb_ref[...]) + bg_ref[...], SWIGLU_LIMIT)
        u = jnp.clip(_dot(x, wub_ref[...]) + bu_ref[...], -SWIGLU_LIMIT, SWIGLU_LIMIT)
        o_ref[...] = ((u + 1.0) * g * _sigmoid(SWIGLU_ALPHA * g)).astype(o_ref.dtype)

    @pl.when(i >= nu_ref[0])
    def _():
        o_ref[...] = jnp.zeros(o_ref.shape, o_ref.dtype)


def moe_up(xg, blk_expert, n_used, w1, b1):
    cap, d = xg.shape
    n_blk = cap // MOE_TM
    n_j = D_FF // MOE_TN
    return pl.pallas_call(
        _moe_up_kernel,
        grid_spec=pltpu.PrefetchScalarGridSpec(
            num_scalar_prefetch=2,
            grid=(n_j, n_blk),
            in_specs=[pl.BlockSpec((MOE_TM, d), lambda j, i, be, nu: (jnp.minimum(i, nu[0] - 1), 0)),
                      pl.BlockSpec((None, d, MOE_TN), lambda j, i, be, nu: (be[i], 0, j)),
                      pl.BlockSpec((None, d, MOE_TN), lambda j, i, be, nu: (be[i], 0, n_j + j)),
                      pl.BlockSpec((None, 1, MOE_TN), lambda j, i, be, nu: (be[i], 0, j)),
                      pl.BlockSpec((None, 1, MOE_TN), lambda j, i, be, nu: (be[i], 0, n_j + j))],
            out_specs=pl.BlockSpec((MOE_TM, MOE_TN), lambda j, i, be, nu: (i, j)),
            scratch_shapes=[pltpu.VMEM((d, MOE_TN), BF), pltpu.VMEM((d, MOE_TN), BF)]),
        out_shape=jax.ShapeDtypeStruct((cap, D_FF), BF),
        compiler_params=_cparams(("arbitrary", "arbitrary")),
        name="moe_up",
    )(blk_expert, n_used, xg, w1, w1, b1, b1)


def _moe_down_kernel(be_ref, nu_ref, a_ref, w_ref, b_ref, o_ref, wb_ref):
    i = pl.program_id(1)
    changed = jnp.logical_or(i == 0, be_ref[i] != be_ref[jnp.maximum(i - 1, 0)])

    @pl.when(changed)
    def _():
        wb_ref[...] = w_ref[...].astype(BF)

    @pl.when(i < nu_ref[0])
    def _():
        o_ref[...] = _dot(a_ref[...], wb_ref[...]) + b_ref[...]

    @pl.when(i >= nu_ref[0])
    def _():
        o_ref[...] = jnp.zeros(o_ref.shape, o_ref.dtype)


def moe_down(act, blk_expert, n_used, w2, b2):
    cap, f = act.shape
    d = w2.shape[2]
    n_blk = cap // MOE_TM
    return pl.pallas_call(
        _moe_down_kernel,
        grid_spec=pltpu.PrefetchScalarGridSpec(
            num_scalar_prefetch=2,
            grid=(d // MOE_TN_DOWN, n_blk),
            in_specs=[pl.BlockSpec((MOE_TM, f), lambda j, i, be, nu: (jnp.minimum(i, nu[0] - 1), 0)),
                      pl.BlockSpec((None, f, MOE_TN_DOWN), lambda j, i, be, nu: (be[i], 0, j)),
                      pl.BlockSpec((None, 1, MOE_TN_DOWN), lambda j, i, be, nu: (be[i], 0, j))],
            out_specs=pl.BlockSpec((MOE_TM, MOE_TN_DOWN), lambda j, i, be, nu: (i, j)),
            scratch_shapes=[pltpu.VMEM((f, MOE_TN_DOWN), BF)]),
        out_shape=jax.ShapeDtypeStruct((cap, d), F32),
        compiler_params=_cparams(("arbitrary", "arbitrary")),
        name="moe_down",
    )(blk_expert, n_used, act, w2, b2)


def _combine_ln_kernel(slot_ref, yb_hbm, gate_ref, x_ref, g_ref, b_ref, o_ref, ids_ref, buf_ref, sem_ids, sem, *, tn):
    _slot_ids_to_smem(slot_ref, ids_ref, sem_ids)

    def issue(t, carry):
        for k in range(TOP_K):
            pltpu.make_async_copy(yb_hbm.at[pl.ds(ids_ref[0, t * TOP_K + k], 1), :],
                                  buf_ref.at[k, pl.ds(t, 1), :], sem).start()
        return carry

    lax.fori_loop(0, tn, issue, 0, unroll=8)
    for k in range(TOP_K):
        pltpu.make_async_copy(yb_hbm.at[pl.ds(0, tn), :], buf_ref.at[k], sem).wait()
    gate = gate_ref[...]
    y = gate[:, 0:1] * buf_ref[0]
    for k in range(1, TOP_K):
        y = y + gate[:, k:k + 1] * buf_ref[k]
    o_ref[...] = _layer_norm(DN_ALPHA * x_ref[...] + y, g_ref[...], b_ref[...])


def moe_combine_ln(yb, slots, gate, x, g, b):
    n, d = x.shape
    tn = MOE_TOK
    assert n % tn == 0, (n, tn)
    return pl.pallas_call(
        functools.partial(_combine_ln_kernel, tn=tn),
        grid=(n // tn,),
        in_specs=[pl.BlockSpec((1, 1, TOP_K * tn), lambda i: (i, 0, 0)),
                  pl.BlockSpec(memory_space=pl.ANY),
                  pl.BlockSpec((tn, LANES), lambda i: (i, 0)),
                  pl.BlockSpec((tn, d), lambda i: (i, 0)),
                  pl.BlockSpec((1, d), lambda i: (0, 0)),
                  pl.BlockSpec((1, d), lambda i: (0, 0))],
        out_specs=pl.BlockSpec((tn, d), lambda i: (i, 0)),
        out_shape=jax.ShapeDtypeStruct((n, d), F32),
        scratch_shapes=[pltpu.SMEM((1, TOP_K * tn), jnp.int32),
                        pltpu.VMEM((TOP_K, tn, d), F32),
                        pltpu.SemaphoreType.DMA(()),
                        pltpu.SemaphoreType.DMA(())],
        compiler_params=_cparams(("arbitrary",)),
        name="moe_combine_ln",
    )(slots.reshape(n // tn, 1, TOP_K * tn), yb, gate, x, g.reshape(1, d), b.reshape(1, d))


def moe_ffn_ln(x, router_w, router_b, w1, b1, w2, b2, e0, ln_g, ln_b):
    n, d = x.shape
    w_pad = jnp.pad(router_w, ((0, 0), (0, LANES - N_EXPERTS)))
    b_pad = jnp.pad(router_b, (0, LANES - N_EXPERTS)).reshape(1, LANES)
    idx, gate, rank, cnt = moe_router(x, w_pad, b_pad, 512)

    counts = cnt[0, :N_EXPERTS].astype(jnp.int32)
    padded = (counts + MOE_TM - 1) // MOE_TM * MOE_TM
    pad_end = jnp.cumsum(padded)
    pad_start = pad_end - padded
    onehot = idx[:, :TOP_K, None] == jnp.arange(N_EXPERTS, dtype=jnp.int32)
    slots = (jnp.sum(jnp.where(onehot, pad_start, 0), axis=-1) + rank[:, :TOP_K]).reshape(-1)
    n_blk = -(-(n * TOP_K + N_EXPERTS * (MOE_TM - 1)) // MOE_TM)
    blk_row0 = jnp.arange(n_blk, dtype=jnp.int32) * MOE_TM
    blk_expert = e0 + jnp.minimum(jnp.sum(pad_end[None, :] <= blk_row0[:, None], axis=1), N_EXPERTS - 1)
    n_used = (pad_end[-1] // MOE_TM).astype(jnp.int32).reshape(1)

    cap = n_blk * MOE_TM
    pad0 = jnp.concatenate([pad_start + counts, pad_end[-1:]]).astype(jnp.int32)
    npad = jnp.concatenate([padded - counts, cap - pad_end[-1:]]).astype(jnp.int32)
    xg = moe_dispatch(x, slots, pad0, npad, cap)
    act = moe_up(xg, blk_expert.astype(jnp.int32), n_used, w1, b1)
    yb = moe_down(act, blk_expert.astype(jnp.int32), n_used, w2, b2)
    return moe_combine_ln(yb, slots, gate, x, ln_g, ln_b)


def _pad_heads(w):
    lead = w.shape[:-1]
    w = w.reshape(lead + (GLA_H, GLA_DK))
    w = jnp.pad(w, [(0, 0)] * len(lead) + [(0, 0), (0, LANES - GLA_DK)])
    return w.reshape(lead + (GLA_H * LANES,))


def _tail_column_map():
    c = np.cumsum((3 * NSA_H, GLA_H * GLA_DK, GLA_H * GLA_DK, GROUP_W, GLA_RANK, GROUP_W))
    dst = np.zeros(c[-1], np.int64)
    src = np.arange(c[-1])
    rel = lambda cb: (cb - CB_MISC) * LANES
    dst[:c[0]] = rel(CB_MISC) + MISC_GATE0 + src[:c[0]]
    for lo, hi, cb in ((c[0], c[1], CB_QD), (c[1], c[2], CB_KD)):
        k = src[lo:hi] - lo
        dst[lo:hi] = rel(cb) + (k // GLA_DK) * LANES + k % GLA_DK
    dst[c[2]:c[3]] = rel(CB_VD) + src[c[2]:c[3]] - c[2]
    dst[c[3]:c[4]] = rel(CB_MISC) + MISC_LR0 + src[c[3]:c[4]] - c[3]
    dst[c[4]:c[5]] = rel(CB_GOD) + src[c[4]:c[5]] - c[4]
    return dst


def _relayout_w_in(w):
    wb = w.astype(BF)
    n_keep = CB_MISC * LANES
    dst = _tail_column_map()
    sel = np.zeros((dst.shape[0], D_PROJ - n_keep), np.float32)
    sel[np.arange(dst.shape[0]), dst] = 1.0
    tail = jnp.dot(wb[:, n_keep:], jnp.asarray(sel, BF), preferred_element_type=F32).astype(BF)
    return jnp.concatenate([wb[:, :n_keep], tail], axis=1)


def kernel(x_prompt, x_sample, cache_pool, cache_nsa_kv, cache_win_kv, state_gla, cache_mem_kv, page_table,
           mem_prompt, ln_in_g, ln_in_b, w_in, w_out, pool_w, pool_scale, gmlp_ln_g, gmlp_ln_b, gmlp_ws, gmlp_bs,
           nsa_phi, gla_wa, gla_ba, gla_norm_g, ln1_g, ln1_b, mem_wq, mem_wkv, mem_wo, ln2_g, ln2_b,
           router_w, router_b, moe_w1, moe_b1, moe_w2, moe_b2, ln3_g, ln3_b):
    n_bp, t_p, d = x_prompt.shape
    n_bs, t_s, _ = x_sample.shape
    n_p = n_bp * t_p
    n_s = n_bs * t_s
    n_pool = cache_nsa_kv.shape[1]
    page = cache_nsa_kv.shape[2]
    past_len = page_table.shape[1] * page
    n_wc = cache_win_kv.shape[2]
    assert t_p % 512 == 0 and n_s % 512 == 0 and t_s == 8 and past_len % SEL_BLOCK == 0 and n_wc == WINDOW

    x = jnp.concatenate([ln_rows(x_prompt.reshape(n_p, d), ln_in_g, ln_in_b, 512),
                         ln_rows(x_sample.reshape(n_s, d), ln_in_g, ln_in_b, 512)], axis=0)
    n_tok = n_p + n_s
    tm_big = 1024 if n_tok % 1024 == 0 else 512
    cpp = page // CMP_BLOCK
    cache_blocks = cache_nsa_kv.reshape(DEPTH * n_pool * cpp, CMP_BLOCK, 4, NSA_DH)
    state0 = state_gla.reshape(DEPTH * n_bs, GLA_H, GLA_DK, GLA_DV)
    w1_all = moe_w1.reshape(DEPTH * N_EXPERTS, d, 2 * D_FF)
    b1_all = moe_b1.reshape(DEPTH * N_EXPERTS, 1, 2 * D_FF)
    w2_all = moe_w2.reshape(DEPTH * N_EXPERTS, D_FF, d)
    b2_all = moe_b2.reshape(DEPTH * N_EXPERTS, 1, d)

    expand_p = (np.arange(t_p)[None, :] // SEL_BLOCK == np.arange(t_p // SEL_BLOCK)[:, None])
    expand_p = jnp.asarray(expand_p, BF)
    expand_s = (np.arange(past_len + LANES)[None, :] // SEL_BLOCK == np.arange(LANES)[:, None])
    expand_s = jnp.asarray(expand_s, BF)
    pair_s = jnp.asarray(np.arange(past_len // CMP_BLOCK)[:, None] // 2 == np.arange(LANES)[None, :], BF)
    bb_s = LANES // t_s
    eye_bb = jnp.eye(bb_s, dtype=F32)

    outs = {k: [] for k in ("pool_p", "nsa_p", "win_p", "gla_p", "mem_p", "pool_s", "chunk_s", "nsa_s", "win_s",
                            "gla_s")}
    for l in range(DEPTH):
        proj = matmul(x, _relayout_w_in(w_in[l]), tm_big, 512)

        pre_p = jnp.zeros((n_bp, POOL_PRE, GROUP_W), F32)
        pre_s = jnp.pad(cache_pool[l], ((0, 0), (1, 0), (0, 0)))
        ya = jnp.concatenate([
            pool_mixer(proj, 0, n_bp, t_p, pre_p, pool_w[l], pool_scale[l], 0, 1, 512),
            pool_mixer(proj, n_p, n_bs, t_s, pre_s, pool_w[l], pool_scale[l], past_len, bb_s, t_s)], axis=0)

        tril = jnp.tril(jnp.ones((GMLP_CHUNK, GMLP_CHUNK), F32))
        w_mix_p = (gmlp_ws[l] * tril).astype(BF)
        bias_p = jnp.repeat(gmlp_bs[l].T, LANES, axis=1)
        ws_s = gmlp_ws[l][:, :t_s, :t_s] * tril[:t_s, :t_s]
        w_mix_s = jnp.einsum('ab,gts->gatbs', eye_bb, ws_s).reshape(4, LANES, LANES).astype(BF)
        bias_s = jnp.tile(jnp.repeat(gmlp_bs[l][:, :t_s].T, LANES, axis=1), (bb_s, 1))
        yb_p, _ = gmlp_mixer(proj, 0, n_p, gmlp_ln_g[l], gmlp_ln_b[l], w_mix_p, bias_p, 4)
        yb_s, v_s = gmlp_mixer(proj, n_p, n_s, gmlp_ln_g[l], gmlp_ln_b[l], w_mix_s, bias_s, 4)
        yb = jnp.concatenate([yb_p, yb_s], axis=0)

        phi2 = nsa_phi[l].astype(BF).reshape(2, CMP_BLOCK // 2, 2 * NSA_DH, NSA_DH)
        kc_p, vc_p = cmp_project(proj, n_p // CMP_BLOCK, phi2[0], phi2[1], t_p // CMP_BLOCK)
        yc_p = nsa_prompt(proj, n_bp, t_p, kc_p, vc_p, expand_p)
        kvc_pool = cmp_pool(cache_blocks, l * n_pool * cpp, n_pool * cpp, phi2[0], phi2[1], 128)
        yc_s = nsa_sample(proj, n_p, n_bs, t_s, cache_nsa_kv, l, page_table,
                          kvc_pool.reshape(n_pool, cpp, 2 * NSA_DH), cache_win_kv, pair_s, expand_s)
        yc = jnp.concatenate([yc_p, yc_s], axis=0)

        wa_pad = jnp.zeros((LANES, GLA_H * LANES), F32).at[MISC_LR0:MISC_LR0 + GLA_RANK].set(_pad_heads(gla_wa[l]))
        wa_pad = wa_pad.astype(BF)
        ba_pad = _pad_heads(gla_ba[l]).reshape(1, GLA_H * LANES)
        yd_p, st_p = gla_prompt(proj, n_bp, t_p, wa_pad, ba_pad, gla_norm_g[l])
        yd_s, st_s = gla_sample(proj, n_p, n_bs, t_s, wa_pad, ba_pad, gla_norm_g[l], state0, l * n_bs, bb_s)
        yd = jnp.concatenate([yd_p, yd_s], axis=0)

        x = matmul_res_ln([ya, yb, yc, yd], w_out[l].astype(BF), x, ln1_g[l], ln1_b[l], 256)

        qm = matmul(x, mem_wq[l].astype(BF), tm_big, 512)
        mem_kv_p = matmul(mem_prompt.reshape(n_bp * MEM_LEN, d), mem_wkv[l].astype(BF), 512, 512)
        mem_kv_p = mem_kv_p.reshape(1, n_bp, MEM_LEN, 2, MEM_H, MEM_DH)
        om = jnp.concatenate([
            mem_attention(qm, 0, n_bp, t_p, mem_kv_p, 0, 512),
            mem_attention(qm, n_p, n_bs, t_s, cache_mem_kv, l, t_s)], axis=0)
        x = matmul_res_ln([om], mem_wo[l].astype(BF), x, ln2_g[l], ln2_b[l], 256)

        x = moe_ffn_ln(x, router_w[l], router_b[l], w1_all, b1_all, w2_all, b2_all, l * N_EXPERTS,
                       ln3_g[l], ln3_b[l])

        seqs_p = lambda c0, c1: proj[:n_p, c0 * LANES:c1 * LANES].reshape(n_bp, t_p, (c1 - c0) * LANES)
        seqs_s = lambda c0, c1: proj[n_p:, c0 * LANES:c1 * LANES].reshape(n_bs, t_s, (c1 - c0) * LANES)
        outs["pool_p"].append(seqs_p(CB_XA, CB_U)[:, t_p - POOL_BUF:])
        outs["nsa_p"].append(seqs_p(CB_CMPK, CB_WINK).reshape(n_bp, t_p, 4, NSA_DH))
        outs["win_p"].append(seqs_p(CB_WINK, CB_MISC)[:, t_p - min(WINDOW, t_p):]
                             .reshape(n_bp, min(WINDOW, t_p), 2, NSA_DH))
        outs["gla_p"].append(st_p)
        outs["mem_p"].append(mem_kv_p[0])
        outs["pool_s"].append(seqs_s(CB_XA, CB_U))
        outs["chunk_s"].append(v_s.reshape(n_bs, t_s, GROUP_W))
        outs["nsa_s"].append(seqs_s(CB_CMPK, CB_WINK).reshape(n_bs, t_s, 4, NSA_DH))
        outs["win_s"].append(seqs_s(CB_WINK, CB_MISC).reshape(n_bs, t_s, 2, NSA_DH))
        outs["gla_s"].append(st_s)

    st = lambda k: jnp.stack(outs[k])
    return (x[:n_p].reshape(n_bp, t_p, d), x[n_p:].reshape(n_bs, t_s, d), st("pool_p"), st("nsa_p"), st("win_p"),
            st("gla_p"), st("mem_p"), st("pool_s"), st("chunk_s"), st("nsa_s"), st("win_s"), st("gla_s"))
```

```python
import functools

import jax
import jax.numpy as jnp
import numpy as np
from jax import lax
from jax.experimental import pallas as pl
from jax.experimental.pallas import tpu as pltpu

BF = jnp.bfloat16
F32 = jnp.float32

D_MODEL = 2048
DEPTH = 2
GROUP_W = 512
LANES = 128
POOL_WINDOWS = (2, 4, 8, 16)
POOL_BUF = 15
POOL_PRE = 16
GMLP_CHUNK = 128
NSA_DH = 128
NSA_H = 4
CMP_BLOCK = 32
SEL_BLOCK = 64
SEL_TOP = 16
WINDOW = 512
Q_BLOCK = 128
GLA_H = 4
GLA_DK = 64
GLA_DV = 128
GLA_RANK = 16
GLA_TAU = 16.0
GLA_CHUNK = 64
MEM_LEN = 256
MEM_H = 4
MEM_DH = 128
N_EXPERTS = 32
TOP_K = 4
D_FF = 2048
SWIGLU_LIMIT = 7.0
SWIGLU_ALPHA = 1.702
DN_ALPHA = (2.0 * DEPTH) ** 0.25
LN_EPS = 1e-5
NEG_INF = float("-inf")

CB_XA, CB_U, CB_V, CB_QC = 0, 4, 8, 12
CB_CMPK, CB_CMPV, CB_SELK, CB_SELV, CB_WINK, CB_WINV = 16, 17, 18, 19, 20, 21
CB_MISC, CB_QD, CB_KD, CB_VD, CB_GOD = 22, 24, 28, 32, 36
N_CB = 40
D_PROJ = N_CB * LANES
MISC_GATE0 = 0
MISC_LR0 = 12

MOE_TM = 512
MOE_TN = 512
MOE_TN_DOWN = 1024
MOE_TOK = 256
MOE_ZERO_ROWS = 32
VMEM_LIMIT = 56 * 1024 * 1024


def _cparams(sem):
    return pltpu.CompilerParams(dimension_semantics=sem, vmem_limit_bytes=VMEM_LIMIT)


def _layer_norm(x, g, b):
    mu = jnp.mean(x, -1, keepdims=True)
    xc = x - mu
    var = jnp.mean(xc * xc, -1, keepdims=True)
    return xc * lax.rsqrt(var + LN_EPS) * g + b


def _dot(a, b):
    return jnp.dot(a, b, preferred_element_type=F32)


def _dot_nt(a, b):
    return lax.dot_general(a, b, (((1,), (1,)), ((), ())), preferred_element_type=F32)


def _dot_tn(a, b):
    return lax.dot_general(a, b, (((0,), (0,)), ((), ())), preferred_element_type=F32)


def _sigmoid(x):
    return 1.0 / (1.0 + jnp.exp(-x))


def _ln_kernel(x_ref, g_ref, b_ref, o_ref):
    o_ref[...] = _layer_norm(x_ref[...], g_ref[...], b_ref[...])


def ln_rows(x, g, b, tm):
    m, d = x.shape
    assert m % tm == 0, (m, tm)
    return pl.pallas_call(
        _ln_kernel,
        grid=(m // tm,),
        in_specs=[pl.BlockSpec((tm, d), lambda i: (i, 0)),
                  pl.BlockSpec((1, d), lambda i: (0, 0)),
                  pl.BlockSpec((1, d), lambda i: (0, 0))],
        out_specs=pl.BlockSpec((tm, d), lambda i: (i, 0)),
        out_shape=jax.ShapeDtypeStruct((m, d), F32),
        compiler_params=_cparams(("parallel",)),
        name="ln_rows",
    )(x, g.reshape(1, d), b.reshape(1, d))


def _mm_kernel(x_ref, w_ref, o_ref, xb_ref):
    @pl.when(pl.program_id(1) == 0)
    def _():
        xb_ref[...] = x_ref[...].astype(BF)

    o_ref[...] = _dot(xb_ref[...], w_ref[...]).astype(o_ref.dtype)


def matmul(x, w, tm, tn, out_dtype=F32):
    m, k = x.shape
    n = w.shape[1]
    assert m % tm == 0 and n % tn == 0, (m, tm, n, tn)
    return pl.pallas_call(
        _mm_kernel,
        grid=(m // tm, n // tn),
        in_specs=[pl.BlockSpec((tm, k), lambda i, j: (i, 0)),
                  pl.BlockSpec((k, tn), lambda i, j: (0, j))],
        out_specs=pl.BlockSpec((tm, tn), lambda i, j: (i, j)),
        out_shape=jax.ShapeDtypeStruct((m, n), out_dtype),
        scratch_shapes=[pltpu.VMEM((tm, k), BF)],
        compiler_params=_cparams(("parallel", "arbitrary")),
        name="matmul",
    )(x, w)


def _mm_res_ln_kernel(*refs, n_in):
    xs = refs[:n_in]
    w_ref, r_ref, g_ref, b_ref, o_ref = refs[n_in:]
    acc = None
    k0 = 0
    for x_ref in xs:
        kk = x_ref.shape[1]
        part = _dot(x_ref[...].astype(BF), w_ref[k0:k0 + kk, :])
        acc = part if acc is None else acc + part
        k0 += kk
    o_ref[...] = _layer_norm(DN_ALPHA * r_ref[...] + acc, g_ref[...], b_ref[...])


def matmul_res_ln(xs, w, resid, g, b, tm):
    m, d = resid.shape
    assert m % tm == 0, (m, tm)
    in_specs = [pl.BlockSpec((tm, x.shape[1]), lambda i: (i, 0)) for x in xs]
    in_specs += [pl.BlockSpec(w.shape, lambda i: (0, 0)),
                 pl.BlockSpec((tm, d), lambda i: (i, 0)),
                 pl.BlockSpec((1, d), lambda i: (0, 0)),
                 pl.BlockSpec((1, d), lambda i: (0, 0))]
    return pl.pallas_call(
        functools.partial(_mm_res_ln_kernel, n_in=len(xs)),
        grid=(m // tm,),
        in_specs=in_specs,
        out_specs=pl.BlockSpec((tm, d), lambda i: (i, 0)),
        out_shape=jax.ShapeDtypeStruct((m, d), F32),
        compiler_params=_cparams(("parallel",)),
        name="matmul_res_ln",
    )(*xs, w, resid, g.reshape(1, d), b.reshape(1, d))


def _pool_kernel(x_ref, pre_ref, w_ref, sc_ref, o_ref, ext_ref, *, bb, tt, n_tiles, t0):
    j = pl.program_id(1)

    @pl.when(j == 0)
    def _():
        ext_ref[:, 0:POOL_PRE, :] = pre_ref[...]

    x = x_ref[...].reshape(bb, tt, GROUP_W)
    ext_ref[:, POOL_PRE:, :] = x
    t_idx = j * tt + lax.broadcasted_iota(jnp.int32, (1, tt, LANES), 1)
    for g, win in enumerate(POOL_WINDOWS):
        cs = slice(g * LANES, (g + 1) * LANES)
        xg = ext_ref[:, POOL_PRE:POOL_PRE + tt, cs]
        s = xg
        for k in range(1, win):
            s = s + ext_ref[:, POOL_PRE - k:POOL_PRE - k + tt, cs]
        cnt = jnp.minimum(win, t0 + 1 + t_idx).astype(F32)
        d = s / cnt - xg
        y = _dot(d.reshape(bb * tt, LANES).astype(BF), w_ref[g])
        o_ref[:, cs] = (y * sc_ref[:, cs]).astype(o_ref.dtype)
    if n_tiles > 1:
        ext_ref[:, 0:POOL_PRE, :] = ext_ref[:, tt:tt + POOL_PRE, :]


def pool_mixer(proj, row0, n_b, t_len, prefix, pool_w, pool_scale, t0, bb, tt):
    n_tiles = t_len // tt
    rb = bb * tt
    base = row0 // rb
    return pl.pallas_call(
        functools.partial(_pool_kernel, bb=bb, tt=tt, n_tiles=n_tiles, t0=t0),
        grid=(n_b // bb, n_tiles),
        in_specs=[pl.BlockSpec((rb, GROUP_W), lambda i, j: (base + i * n_tiles + j, CB_XA // 4)),
                  pl.BlockSpec((bb, POOL_PRE, GROUP_W), lambda i, j: (i, 0, 0)),
                  pl.BlockSpec((4, LANES, LANES), lambda i, j: (0, 0, 0)),
                  pl.BlockSpec((1, GROUP_W), lambda i, j: (0, 0))],
        out_specs=pl.BlockSpec((rb, GROUP_W), lambda i, j: (i * n_tiles + j, 0)),
        out_shape=jax.ShapeDtypeStruct((n_b * t_len, GROUP_W), BF),
        scratch_shapes=[pltpu.VMEM((bb, POOL_PRE + tt, GROUP_W), F32)],
        compiler_params=_cparams(("parallel", "arbitrary")),
        name="pool_mixer",
    )(proj, prefix, pool_w.astype(BF), pool_scale.reshape(1, GROUP_W))


def _gelu_tanh(x):
    return 0.5 * x * (1.0 + jnp.tanh(np.sqrt(2.0 / np.pi) * (x + 0.044715 * (x * x * x))))


def _gmlp_kernel(u_ref, v_ref, g_ref, b_ref, w_ref, bias_ref, y_ref, vo_ref, *, n_ch):
    for c in range(n_ch):
        rows = slice(c * GMLP_CHUNK, (c + 1) * GMLP_CHUNK)
        u = _gelu_tanh(u_ref[rows, :])
        v = _layer_norm(_gelu_tanh(v_ref[rows, :]), g_ref[...], b_ref[...])
        vo_ref[rows, :] = v
        vb = v.astype(BF)
        for g in range(4):
            cs = slice(g * LANES, (g + 1) * LANES)
            mixed = _dot(w_ref[g], vb[:, cs]) + bias_ref[:, cs]
            y_ref[rows, cs] = (u[:, cs] * mixed).astype(y_ref.dtype)


def gmlp_mixer(proj, row0, n_rows, ln_g, ln_b, w_mix, bias_full, n_ch):
    rb = n_ch * GMLP_CHUNK
    base = row0 // rb
    return pl.pallas_call(
        functools.partial(_gmlp_kernel, n_ch=n_ch),
        grid=(n_rows // rb,),
        in_specs=[pl.BlockSpec((rb, GROUP_W), lambda i: (base + i, CB_U // 4)),
                  pl.BlockSpec((rb, GROUP_W), lambda i: (base + i, CB_V // 4)),
                  pl.BlockSpec((1, GROUP_W), lambda i: (0, 0)),
                  pl.BlockSpec((1, GROUP_W), lambda i: (0, 0)),
                  pl.BlockSpec((4, GMLP_CHUNK, GMLP_CHUNK), lambda i: (0, 0, 0)),
                  pl.BlockSpec((GMLP_CHUNK, GROUP_W), lambda i: (0, 0))],
        out_specs=[pl.BlockSpec((rb, GROUP_W), lambda i: (i, 0)),
                   pl.BlockSpec((rb, GROUP_W), lambda i: (i, 0))],
        out_shape=[jax.ShapeDtypeStruct((n_rows, GROUP_W), BF),
                   jax.ShapeDtypeStruct((n_rows, GROUP_W), F32)],
        compiler_params=_cparams(("parallel",)),
        name="gmlp_mixer",
    )(proj, proj, ln_g.reshape(1, GROUP_W), ln_b.reshape(1, GROUP_W), w_mix, bias_full)


def _cmp_accumulate(row_pair, phi_ref):
    acc = None
    for l2 in range(CMP_BLOCK // 2):
        xa, xb = row_pair(l2)
        part = _dot(jnp.concatenate([xa, xb], axis=1).astype(BF), phi_ref[l2])
        acc = part if acc is None else acc + part
    return acc


def _cmp_kernel(xk_ref, xv_ref, phik_ref, phiv_ref, ok_ref, ov_ref, *, tb):
    for x_ref, phi_ref, o_ref in ((xk_ref, phik_ref, ok_ref), (xv_ref, phiv_ref, ov_ref)):
        o_ref[...] = _cmp_accumulate(
            lambda l2: (x_ref[pl.ds(2 * l2, tb, stride=CMP_BLOCK), :],
                        x_ref[pl.ds(2 * l2 + 1, tb, stride=CMP_BLOCK), :]), phi_ref)


def cmp_project(proj, n_blk, phik2, phiv2, tb):
    assert n_blk % tb == 0, (n_blk, tb)
    rows = tb * CMP_BLOCK
    return pl.pallas_call(
        functools.partial(_cmp_kernel, tb=tb),
        grid=(n_blk // tb,),
        in_specs=[pl.BlockSpec((rows, LANES), lambda i: (i, CB_CMPK)),
                  pl.BlockSpec((rows, LANES), lambda i: (i, CB_CMPV)),
                  pl.BlockSpec(phik2.shape, lambda i: (0, 0, 0)),
                  pl.BlockSpec(phiv2.shape, lambda i: (0, 0, 0))],
        out_specs=[pl.BlockSpec((tb, NSA_DH), lambda i: (i, 0)),
                   pl.BlockSpec((tb, NSA_DH), lambda i: (i, 0))],
        out_shape=[jax.ShapeDtypeStruct((n_blk, NSA_DH), F32),
                   jax.ShapeDtypeStruct((n_blk, NSA_DH), F32)],
        compiler_params=_cparams(("parallel",)),
        name="cmp_project",
    )(proj, proj, phik2, phiv2)


def _cmp_pool_kernel(x_ref, phik_ref, phiv_ref, o_ref):
    for s, phi_ref in ((0, phik_ref), (1, phiv_ref)):
        o_ref[:, s * NSA_DH:(s + 1) * NSA_DH] = _cmp_accumulate(
            lambda l2: (x_ref[:, 2 * l2, s, :], x_ref[:, 2 * l2 + 1, s, :]), phi_ref)


def cmp_pool(cache_blocks, blk0, n_blk, phik2, phiv2, tb):
    assert n_blk % tb == 0 and blk0 % tb == 0, (n_blk, blk0, tb)
    base = blk0 // tb
    return pl.pallas_call(
        _cmp_pool_kernel,
        grid=(n_blk // tb,),
        in_specs=[pl.BlockSpec((tb, CMP_BLOCK, 4, NSA_DH), lambda i: (base + i, 0, 0, 0)),
                  pl.BlockSpec(phik2.shape, lambda i: (0, 0, 0)),
                  pl.BlockSpec(phiv2.shape, lambda i: (0, 0, 0))],
        out_specs=pl.BlockSpec((tb, 2 * NSA_DH), lambda i: (i, 0)),
        out_shape=jax.ShapeDtypeStruct((n_blk, 2 * NSA_DH), F32),
        compiler_params=_cparams(("parallel",)),
        name="cmp_pool",
    )(cache_blocks, phik2, phiv2)


def _alibi_slope(h):
    return 2.0 ** (-8.0 * (h + 1) / NSA_H)


def _head_rows_const(n_q, fn):
    r = lax.broadcasted_iota(jnp.int32, (NSA_H * n_q, 1), 0)
    out = jnp.full((NSA_H * n_q, 1), fn(NSA_H - 1), F32)
    for h in range(NSA_H - 2, -1, -1):
        out = jnp.where(r < (h + 1) * n_q, fn(h), out)
    return out


def _stack_heads(q):
    return jnp.concatenate([q[:, h * NSA_DH:(h + 1) * NSA_DH] for h in range(NSA_H)], axis=0)


def _select_blocks(score, n_blk):
    blk = lax.broadcasted_iota(jnp.int32, score.shape, 0)
    cnt = jnp.zeros(score.shape, F32)
    for i in range(n_blk):
        row = score[i:i + 1, :]
        beats = (row > score) | ((row == score) & (blk > i))
        cnt = cnt + jnp.where(beats, 1.0, 0.0)
    return jnp.where((cnt < float(SEL_TOP)) & (score > NEG_INF), 1.0, 0.0)


def _softmax_segments(segs):
    m = None
    for s, _ in segs:
        mi = jnp.max(s, axis=1, keepdims=True)
        m = mi if m is None else jnp.maximum(m, mi)
    m = jnp.where(m > NEG_INF, m, 0.0)
    den = None
    acc = None
    for s, v in segs:
        e = jnp.exp(s - m)
        d = jnp.sum(e, axis=1, keepdims=True)
        a = _dot(e.astype(BF), v)
        den = d if den is None else den + d
        acc = a if acc is None else acc + a
    return acc / jnp.where(den > 0, den, 1.0)


def _gated_heads(o_ref, gates, o_c, o_s, o_w, n_q):
    for h in range(NSA_H):
        rows = slice(h * n_q, (h + 1) * n_q)
        c = MISC_GATE0 + 3 * h
        out = (gates[:, c:c + 1] * o_c[rows, :] + gates[:, c + 1:c + 2] * o_s[rows, :]
               + gates[:, c + 2:c + 3] * o_w[rows, :])
        o_ref[:, h * NSA_DH:(h + 1) * NSA_DH] = out.astype(o_ref.dtype)


def _nsa_prompt_kernel(q_ref, misc_ref, kc_ref, vc_ref, ks_ref, vs_ref, kw_ref, vw_ref, e_ref, o_ref,
                       imp_ref, mask_ref, m_ref, l_ref, acc_ref, *, t_len):
    n_q = Q_BLOCK
    n_cmp = t_len // CMP_BLOCK
    n_sel = t_len // SEL_BLOCK
    kt = 512
    q0 = pl.program_id(1) * n_q
    q4 = _stack_heads(q_ref[...] * (NSA_DH ** -0.5)).astype(BF)
    slope_r = _head_rows_const(n_q, _alibi_slope)
    qpos_r = q0 + lax.rem(lax.broadcasted_iota(jnp.int32, (NSA_H * n_q, 1), 0), n_q)

    lane = lax.broadcasted_iota(jnp.int32, (1, NSA_H * n_q), 1)
    qpos_l = q0 + lax.rem(lane, n_q)
    slope_l = jnp.full((1, NSA_H * n_q), _alibi_slope(NSA_H - 1), F32)
    for h in range(NSA_H - 2, -1, -1):
        slope_l = jnp.where(lane < (h + 1) * n_q, _alibi_slope(h), slope_l)
    cmp_end = lax.broadcasted_iota(jnp.int32, (n_cmp, 1), 0) * CMP_BLOCK + (CMP_BLOCK - 1)
    s_c = _dot_nt(kc_ref[...].astype(BF), q4) - slope_l * (qpos_l - cmp_end).astype(F32)
    s_c = jnp.where(cmp_end <= qpos_l, s_c, NEG_INF)
    m_c = jnp.max(s_c, axis=0, keepdims=True)
    m_c = jnp.where(m_c > NEG_INF, m_c, 0.0)
    e_c = jnp.exp(s_c - m_c)
    den_c = jnp.sum(e_c, axis=0, keepdims=True)
    p_c = e_c / jnp.where(den_c > 0, den_c, 1.0)
    o_c = _dot_tn(p_c.astype(BF), vc_ref[...].astype(BF))

    imp = p_c[:, 0:n_q]
    for h in range(1, NSA_H):
        imp = imp + p_c[:, h * n_q:(h + 1) * n_q]
    imp_ref[...] = imp
    imp = imp_ref[pl.ds(0, n_sel, stride=2), :] + imp_ref[pl.ds(1, n_sel, stride=2), :]
    blk = lax.broadcasted_iota(jnp.int32, (n_sel, 1), 0)
    cur = (q0 + lax.broadcasted_iota(jnp.int32, (1, n_q), 1)) // SEL_BLOCK
    forced = (blk == 0) | (blk == cur) | (blk == cur - 1)
    score = jnp.where(blk <= cur, jnp.where(forced, jnp.inf, imp), NEG_INF)
    sel = _select_blocks(score, n_sel)
    mask_ref[...] = _dot_tn(sel.astype(BF), e_ref[...])

    m_ref[...] = jnp.full(m_ref.shape, NEG_INF, F32)
    l_ref[...] = jnp.zeros(l_ref.shape, F32)
    acc_ref[...] = jnp.zeros(acc_ref.shape, F32)

    def sel_tile(t, carry):
        k0 = pl.multiple_of(t * kt, kt)
        s = _dot_nt(q4, ks_ref[pl.ds(k0, kt), :].astype(BF))
        d = qpos_r - (k0 + lax.broadcasted_iota(jnp.int32, (1, kt), 1))
        mk = mask_ref[:, pl.ds(k0, kt)]
        mk = jnp.concatenate([mk] * NSA_H, axis=0)
        s = jnp.where((d >= 0) & (mk > 0.5), s - slope_r * d.astype(F32), NEG_INF)
        m_old = m_ref[...]
        m_new = jnp.maximum(m_old, jnp.max(s, axis=1, keepdims=True))
        m_use = jnp.where(m_new > NEG_INF, m_new, 0.0)
        alpha = jnp.exp(m_old - m_use)
        p = jnp.exp(s - m_use)
        l_ref[...] = alpha * l_ref[...] + jnp.sum(p, axis=1, keepdims=True)
        acc_ref[...] = alpha * acc_ref[...] + _dot(p.astype(BF), vs_ref[pl.ds(k0, kt), :].astype(BF))
        m_ref[...] = m_new
        return carry

    lax.fori_loop(0, (q0 + n_q + kt - 1) // kt, sel_tile, 0)
    l_s = l_ref[...]
    o_s = acc_ref[...] / jnp.where(l_s > 0, l_s, 1.0)

    nw = WINDOW + n_q
    w0 = pl.multiple_of(jnp.maximum(q0 - WINDOW, 0), n_q)
    s_w = _dot_nt(q4, kw_ref[pl.ds(w0, nw), :].astype(BF))
    d_w = qpos_r - (w0 + lax.broadcasted_iota(jnp.int32, (1, nw), 1))
    s_w = jnp.where((d_w >= 0) & (d_w < WINDOW), s_w - slope_r * d_w.astype(F32), NEG_INF)
    o_w = _softmax_segments([(s_w, vw_ref[pl.ds(w0, nw), :].astype(BF))])

    _gated_heads(o_ref, _sigmoid(misc_ref[...]), o_c, o_s, o_w, n_q)


def nsa_prompt(proj, n_b, t_len, k_cmp, v_cmp, expand):
    n_qb = t_len // Q_BLOCK
    n_cmp = t_len // CMP_BLOCK
    n_sel = t_len // SEL_BLOCK
    seq = lambda cb: pl.BlockSpec((t_len, LANES), lambda b, i: (b, cb))
    return pl.pallas_call(
        functools.partial(_nsa_prompt_kernel, t_len=t_len),
        grid=(n_b, n_qb),
        in_specs=[pl.BlockSpec((Q_BLOCK, GROUP_W), lambda b, i: (b * n_qb + i, CB_QC // 4)),
                  pl.BlockSpec((Q_BLOCK, LANES), lambda b, i: (b * n_qb + i, CB_MISC)),
                  pl.BlockSpec((n_cmp, NSA_DH), lambda b, i: (b, 0)),
                  pl.BlockSpec((n_cmp, NSA_DH), lambda b, i: (b, 0)),
                  seq(CB_SELK), seq(CB_SELV), seq(CB_WINK), seq(CB_WINV),
                  pl.BlockSpec((n_sel, t_len), lambda b, i: (0, 0))],
        out_specs=pl.BlockSpec((Q_BLOCK, GROUP_W), lambda b, i: (b * n_qb + i, 0)),
        out_shape=jax.ShapeDtypeStruct((n_b * t_len, GROUP_W), BF),
        scratch_shapes=[pltpu.VMEM((n_cmp, Q_BLOCK), F32),
                        pltpu.VMEM((Q_BLOCK, t_len), F32),
                        pltpu.VMEM((NSA_H * Q_BLOCK, 1), F32),
                        pltpu.VMEM((NSA_H * Q_BLOCK, 1), F32),
                        pltpu.VMEM((NSA_H * Q_BLOCK, NSA_DH), F32)],
        compiler_params=_cparams(("parallel", "arbitrary")),
        name="nsa_prompt",
    )(proj, proj, k_cmp, v_cmp, proj, proj, proj, proj, expand)


def _nsa_sample_kernel(pt_ref, q_ref, misc_ref, new4_ref, new2_ref, winc_ref, pair_ref, e_ref,
                       *rest, n_pages, page, t_q):
    pages = rest[:n_pages]
    kvcs = rest[n_pages:2 * n_pages]
    o_ref, kvc_ref, ks_ref, vs_ref = rest[2 * n_pages:]
    past_len = n_pages * page
    cpp = page // CMP_BLOCK
    n_cmp = past_len // CMP_BLOCK
    n_sel = past_len // SEL_BLOCK + 1
    n_r = NSA_H * t_q
    for j in range(n_pages):
        kvc_ref[j * cpp:(j + 1) * cpp, :] = kvcs[j][...]
        ks_ref[j * page:(j + 1) * page, :] = pages[j][:, 2, :].astype(BF)
        vs_ref[j * page:(j + 1) * page, :] = pages[j][:, 3, :].astype(BF)
    kc_ref = kvc_ref.at[:, 0:NSA_DH]
    vc_ref = kvc_ref.at[:, NSA_DH:2 * NSA_DH]
    ksn_ref = new4_ref.at[:, 2 * NSA_DH:3 * NSA_DH]
    vsn_ref = new4_ref.at[:, 3 * NSA_DH:4 * NSA_DH]
    kwn_ref = new2_ref.at[:, 0:NSA_DH]
    vwn_ref = new2_ref.at[:, NSA_DH:2 * NSA_DH]

    q4 = _stack_heads(q_ref[...] * (NSA_DH ** -0.5)).astype(BF)
    slope_r = _head_rows_const(t_q, _alibi_slope)
    qpos_r = past_len + lax.rem(lax.broadcasted_iota(jnp.int32, (n_r, 1), 0), t_q)

    cmp_end = lax.broadcasted_iota(jnp.int32, (1, n_cmp), 1) * CMP_BLOCK + (CMP_BLOCK - 1)
    s_c = _dot_nt(q4, kc_ref[...].astype(BF)) - slope_r * (qpos_r - cmp_end).astype(F32)
    s_c = jnp.where(cmp_end <= qpos_r, s_c, NEG_INF)
    m_c = jnp.max(s_c, axis=1, keepdims=True)
    m_c = jnp.where(m_c > NEG_INF, m_c, 0.0)
    e_c = jnp.exp(s_c - m_c)
    den_c = jnp.sum(e_c, axis=1, keepdims=True)
    p_c = e_c / jnp.where(den_c > 0, den_c, 1.0)
    o_c = _dot(p_c.astype(BF), vc_ref[...].astype(BF))

    imp = p_c[0:t_q, :]
    for h in range(1, NSA_H):
        imp = imp + p_c[h * t_q:(h + 1) * t_q, :]
    hi = imp.astype(BF)
    lo = (imp - hi.astype(F32)).astype(BF)
    imp = _dot(hi, pair_ref[...]) + _dot(lo, pair_ref[...])
    blk = lax.broadcasted_iota(jnp.int32, (1, LANES), 1)
    cur = (past_len + lax.broadcasted_iota(jnp.int32, (t_q, 1), 0)) // SEL_BLOCK
    forced = (blk == 0) | (blk == cur) | (blk == cur - 1)
    score = jnp.where(blk <= cur, jnp.where(forced, jnp.inf, imp), NEG_INF)
    cnt = jnp.zeros(score.shape, F32)
    for i in range(n_sel):
        col = score[:, i:i + 1]
        beats = (col > score) | ((col == score) & (blk > i))
        cnt = cnt + jnp.where(beats, 1.0, 0.0)
    sel = jnp.where((cnt < float(SEL_TOP)) & (score > NEG_INF), 1.0, 0.0)
    mask = _dot(sel.astype(BF), e_ref[...])
    mask = jnp.concatenate([mask] * NSA_H, axis=0)

    d_p = qpos_r - lax.broadcasted_iota(jnp.int32, (1, past_len), 1)
    s_p = jnp.where(mask[:, 0:past_len] > 0.5, _dot_nt(q4, ks_ref[...]) - slope_r * d_p.astype(F32), NEG_INF)
    d_n = qpos_r - (past_len + lax.broadcasted_iota(jnp.int32, (1, t_q), 1))
    s_n = _dot_nt(q4, ksn_ref[...].astype(BF)) - slope_r * d_n.astype(F32)
    s_n = jnp.where((d_n >= 0) & (mask[:, past_len:past_len + t_q] > 0.5), s_n, NEG_INF)
    o_s = _softmax_segments([(s_p, vs_ref[...]), (s_n, vsn_ref[...].astype(BF))])

    n_wc = winc_ref.shape[0]
    d_c = qpos_r - (past_len - n_wc + lax.broadcasted_iota(jnp.int32, (1, n_wc), 1))
    s_wc = _dot_nt(q4, winc_ref[:, 0, :].astype(BF)) - slope_r * d_c.astype(F32)
    s_wc = jnp.where(d_c < WINDOW, s_wc, NEG_INF)
    s_wn = _dot_nt(q4, kwn_ref[...].astype(BF)) - slope_r * d_n.astype(F32)
    s_wn = jnp.where(d_n >= 0, s_wn, NEG_INF)
    o_w = _softmax_segments([(s_wc, winc_ref[:, 1, :].astype(BF)), (s_wn, vwn_ref[...].astype(BF))])

    _gated_heads(o_ref, _sigmoid(misc_ref[...]), o_c, o_s, o_w, t_q)


def nsa_sample(proj, row0, n_b, t_q, cache_nsa, layer, page_table, kvc_pool, cache_win, pair, expand):
    n_pages = page_table.shape[1]
    page = cache_nsa.shape[2]
    cpp = page // CMP_BLOCK
    past_len = n_pages * page
    n_wc = cache_win.shape[2]
    base = row0 // t_q
    assert CB_SELV == CB_CMPK + 3 and CB_CMPK % 4 == 0 and CB_WINV == CB_WINK + 1 and CB_WINK % 2 == 0

    def page_spec(j):
        return pl.BlockSpec((None, None, page, 4, NSA_DH), lambda b, pt: (layer, pt[b, j], 0, 0, 0))

    def cmp_spec(j):
        return pl.BlockSpec((None, cpp, 2 * NSA_DH), lambda b, pt: (pt[b, j], 0, 0))

    in_specs = [pl.BlockSpec((t_q, GROUP_W), lambda b, pt: (base + b, CB_QC // 4)),
                pl.BlockSpec((t_q, LANES), lambda b, pt: (base + b, CB_MISC)),
                pl.BlockSpec((t_q, 4 * NSA_DH), lambda b, pt: (base + b, CB_CMPK // 4)),
                pl.BlockSpec((t_q, 2 * NSA_DH), lambda b, pt: (base + b, CB_WINK // 2)),
                pl.BlockSpec((None, None, n_wc, 2, NSA_DH), lambda b, pt: (layer, b, 0, 0, 0)),
                pl.BlockSpec(pair.shape, lambda b, pt: (0, 0)),
                pl.BlockSpec(expand.shape, lambda b, pt: (0, 0))]
    in_specs += [page_spec(j) for j in range(n_pages)]
    in_specs += [cmp_spec(j) for j in range(n_pages)]
    return pl.pallas_call(
        functools.partial(_nsa_sample_kernel, n_pages=n_pages, page=page, t_q=t_q),
        grid_spec=pltpu.PrefetchScalarGridSpec(
            num_scalar_prefetch=1,
            grid=(n_b,),
            in_specs=in_specs,
            out_specs=pl.BlockSpec((t_q, GROUP_W), lambda b, pt: (b, 0)),
            scratch_shapes=[pltpu.VMEM((past_len // CMP_BLOCK, 2 * NSA_DH), F32),
                            pltpu.VMEM((past_len, NSA_DH), BF),
                            pltpu.VMEM((past_len, NSA_DH), BF)]),
        out_shape=jax.ShapeDtypeStruct((n_b * t_q, GROUP_W), BF),
        compiler_params=_cparams(("arbitrary",)),
        name="nsa_sample",
    )(page_table, proj, proj, proj, proj, cache_win, pair, expand,
      *([cache_nsa] * n_pages), *([kvc_pool] * n_pages))


def _log_sigmoid(z):
    return jnp.minimum(z, 0.0) - jnp.log(1.0 + jnp.exp(-jnp.abs(z)))


def _cumsum_rows(x, tri):
    hi = x.astype(BF)
    lo = (x - hi.astype(F32)).astype(BF)
    return _dot(tri, hi) + _dot(tri, lo)


def _rows_to_col(row):
    n = row.shape[1]
    eye = lax.broadcasted_iota(jnp.int32, (n, n), 0) == lax.broadcasted_iota(jnp.int32, (n, n), 1)
    return jnp.sum(jnp.where(eye, row, 0.0), axis=1, keepdims=True)


def _gla_out(o, god, ng):
    o = o * lax.rsqrt(jnp.mean(o * o, -1, keepdims=True) + LN_EPS) * ng
    return o * (god * _sigmoid(god))


def _gla_prompt_kernel(q_ref, k_ref, v_ref, god_ref, misc_ref, wa_ref, ba_ref, ng_ref, o_ref, s_ref, st_ref,
                       *, n_chunks):
    c = GLA_CHUNK
    st_ref[...] = jnp.zeros(st_ref.shape, F32)
    r_i = lax.broadcasted_iota(jnp.int32, (c, c), 0)
    c_i = lax.broadcasted_iota(jnp.int32, (c, c), 1)
    causal = r_i >= c_i
    tri = jnp.where(causal, 1.0, 0.0).astype(BF)

    def chunk(n, carry):
        rows = pl.ds(pl.multiple_of(n * c, c), c)
        misc = misc_ref[rows, :].astype(BF)
        for h in range(GLA_H):
            cs = slice(h * LANES, (h + 1) * LANES)
            z = _dot(misc, wa_ref[:, cs]) + ba_ref[:, cs]
            bc = _cumsum_rows(_log_sigmoid(z) / GLA_TAU, tri)
            bl = bc[c - 1:c, :]
            q = q_ref[rows, cs] * (GLA_DK ** -0.5)
            k = k_ref[rows, cs]
            v = v_ref[rows, cs].astype(BF)
            qe = (q * jnp.exp(bc)).astype(BF)
            att = jnp.where(causal, _dot_nt(qe, (k * jnp.exp(-bc)).astype(BF)), 0.0)
            st = st_ref[h]
            o = _dot(qe, st.astype(BF)) + _dot(att.astype(BF), v)
            st_ref[h] = _rows_to_col(jnp.exp(bl)) * st + _dot_tn((k * jnp.exp(bl - bc)).astype(BF), v)
            o_ref[rows, cs] = _gla_out(o, god_ref[rows, cs], ng_ref[...]).astype(o_ref.dtype)
        return carry

    lax.fori_loop(0, n_chunks, chunk, 0)
    for h in range(GLA_H):
        s_ref[h] = st_ref[h, 0:GLA_DK, :]


def gla_prompt(proj, n_b, t_len, wa_pad, ba_pad, norm_g):
    w = GLA_H * LANES
    seq = lambda cb0: pl.BlockSpec((t_len, w), lambda b: (b, cb0 // GLA_H))
    return pl.pallas_call(
        functools.partial(_gla_prompt_kernel, n_chunks=t_len // GLA_CHUNK),
        grid=(n_b,),
        in_specs=[seq(CB_QD), seq(CB_KD), seq(CB_VD), seq(CB_GOD),
                  pl.BlockSpec((t_len, LANES), lambda b: (b, CB_MISC)),
                  pl.BlockSpec((LANES, w), lambda b: (0, 0)),
                  pl.BlockSpec((1, w), lambda b: (0, 0)),
                  pl.BlockSpec((1, GLA_DV), lambda b: (0, 0))],
        out_specs=[pl.BlockSpec((t_len, w), lambda b: (b, 0)),
                   pl.BlockSpec((None, GLA_H, GLA_DK, GLA_DV), lambda b: (b, 0, 0, 0))],
        out_shape=[jax.ShapeDtypeStruct((n_b * t_len, GROUP_W), BF),
                   jax.ShapeDtypeStruct((n_b, GLA_H, GLA_DK, GLA_DV), F32)],
        scratch_shapes=[pltpu.VMEM((GLA_H, LANES, GLA_DV), F32)],
        compiler_params=_cparams(("parallel",)),
        name="gla_prompt",
    )(proj, proj, proj, proj, proj, wa_pad, ba_pad, norm_g.reshape(1, GLA_DV))


def _gla_sample_kernel(q_ref, k_ref, v_ref, god_ref, misc_ref, wa_ref, ba_ref, ng_ref, s0_ref, o_ref, s_ref,
                       *, bb, t_q):
    n_r = bb * t_q
    r_i = lax.broadcasted_iota(jnp.int32, (n_r, n_r), 0)
    c_i = lax.broadcasted_iota(jnp.int32, (n_r, n_r), 1)
    causal = (r_i >= c_i) & (r_i // t_q == c_i // t_q)
    tri = jnp.where(causal, 1.0, 0.0).astype(BF)
    row_b = lax.broadcasted_iota(jnp.int32, (n_r, 1), 0) // t_q

    z = _dot(misc_ref[...].astype(BF), wa_ref[...]) + ba_ref[...]
    bc = _cumsum_rows(_log_sigmoid(z) / GLA_TAU, tri)
    q = q_ref[...] * (GLA_DK ** -0.5)
    k = k_ref[...]
    v = v_ref[...].astype(BF)
    qe = q * jnp.exp(bc)
    att = jnp.where(causal, _dot_nt(qe.astype(BF), (k * jnp.exp(-bc)).astype(BF)), 0.0)
    o = _dot(att.astype(BF), v)
    for b in range(bb):
        mine = row_b == b
        bl = bc[(b + 1) * t_q - 1:(b + 1) * t_q, :]
        s0 = s0_ref[b, 0]
        o = o + _dot(jnp.where(mine, qe, 0.0)[:, 0:GLA_DK].astype(BF), s0.astype(BF))
        k2 = jnp.where(mine, k * jnp.exp(bl - bc), 0.0).astype(BF)
        upd = _dot_tn(k2, v)
        s_ref[b, 0] = _rows_to_col(jnp.exp(bl))[0:GLA_DK, :] * s0 + upd[0:GLA_DK, :]
    o_ref[...] = _gla_out(o, god_ref[...], ng_ref[...]).astype(o_ref.dtype)


def gla_sample(proj, row0, n_b, t_q, wa_pad, ba_pad, norm_g, s0, s0_b0, bb):
    rb = bb * t_q
    base = row0 // rb
    s_i0 = s0_b0 // bb
    seq = lambda cb0: pl.BlockSpec((rb, LANES), lambda i, h: (base + i, cb0 + h))
    return pl.pallas_call(
        functools.partial(_gla_sample_kernel, bb=bb, t_q=t_q),
        grid=(n_b // bb, GLA_H),
        in_specs=[seq(CB_QD), seq(CB_KD), seq(CB_VD), seq(CB_GOD),
                  pl.BlockSpec((rb, LANES), lambda i, h: (base + i, CB_MISC)),
                  pl.BlockSpec((LANES, LANES), lambda i, h: (0, h)),
                  pl.BlockSpec((1, LANES), lambda i, h: (0, h)),
                  pl.BlockSpec((1, GLA_DV), lambda i, h: (0, 0)),
                  pl.BlockSpec((bb, 1, GLA_DK, GLA_DV), lambda i, h: (s_i0 + i, h, 0, 0))],
        out_specs=[pl.BlockSpec((rb, GLA_DV), lambda i, h: (i, h)),
                   pl.BlockSpec((bb, 1, GLA_DK, GLA_DV), lambda i, h: (i, h, 0, 0))],
        out_shape=[jax.ShapeDtypeStruct((n_b * t_q, GROUP_W), BF),
                   jax.ShapeDtypeStruct((n_b, GLA_H, GLA_DK, GLA_DV), F32)],
        compiler_params=_cparams(("parallel", "parallel")),
        name="gla_sample",
    )(proj, proj, proj, proj, proj, wa_pad, ba_pad, norm_g.reshape(1, GLA_DV), s0)


def _mem_attn_kernel(q_ref, kv_ref, o_ref):
    q = q_ref[...] * (MEM_DH ** -0.5)
    for h in range(MEM_H):
        cs = slice(h * MEM_DH, (h + 1) * MEM_DH)
        s = _dot_nt(q[:, cs].astype(BF), kv_ref[:, 0, h, :].astype(BF))
        e = jnp.exp(s - jnp.max(s, axis=1, keepdims=True))
        p = e / jnp.sum(e, axis=1, keepdims=True)
        o_ref[:, cs] = _dot(p.astype(BF), kv_ref[:, 1, h, :].astype(BF)).astype(o_ref.dtype)


def mem_attention(qm, row0, n_b, t_len, kv, layer, tq):
    n_t = t_len // tq
    base = row0 // tq
    w = MEM_H * MEM_DH
    return pl.pallas_call(
        _mem_attn_kernel,
        grid=(n_b, n_t),
        in_specs=[pl.BlockSpec((tq, w), lambda b, i: (base + b * n_t + i, 0)),
                  pl.BlockSpec((None, None, MEM_LEN, 2, MEM_H, MEM_DH), lambda b, i: (layer, b, 0, 0, 0, 0))],
        out_specs=pl.BlockSpec((tq, w), lambda b, i: (b * n_t + i, 0)),
        out_shape=jax.ShapeDtypeStruct((n_b * t_len, w), BF),
        compiler_params=_cparams(("parallel", "parallel")),
        name="mem_attention",
    )(qm, kv)


def _router_kernel(x_ref, w_ref, b_ref, tri_ref, idx_ref, gate_ref, rank_ref, cnt_ref, run_ref):
    @pl.when(pl.program_id(0) == 0)
    def _():
        run_ref[...] = jnp.zeros(run_ref.shape, F32)

    logits = jnp.dot(x_ref[...], w_ref[...], precision=lax.Precision.HIGHEST,
                     preferred_element_type=F32) + b_ref[...]
    lane = lax.broadcasted_iota(jnp.int32, logits.shape, 1).astype(F32)
    vals = jnp.where(lane < N_EXPERTS, logits, NEG_INF)
    idx_out = jnp.zeros(logits.shape, F32)
    e_out = jnp.zeros(logits.shape, F32)
    den = None
    top0 = None
    onehots = []
    for k in range(TOP_K):
        m = jnp.max(vals, axis=1, keepdims=True)
        idx = jnp.min(jnp.where(vals == m, lane, float(LANES)), axis=1, keepdims=True)
        if k == 0:
            top0 = m
        e = jnp.exp(m - top0)
        den = e if den is None else den + e
        idx_out = jnp.where(lane == k, idx, idx_out)
        e_out = jnp.where(lane == k, e, e_out)
        onehots.append(jnp.where(lane == idx, 1.0, 0.0))
        vals = jnp.where(lane == idx, NEG_INF, vals)
    idx_ref[...] = idx_out.astype(jnp.int32)
    gate_ref[...] = e_out / den

    chosen = onehots[0] + onehots[1] + onehots[2] + onehots[3]
    before = _dot(tri_ref[...], chosen.astype(BF)) + run_ref[...]
    rank_out = jnp.zeros(logits.shape, F32)
    for k in range(TOP_K):
        r = jnp.sum(before * onehots[k], axis=1, keepdims=True)
        rank_out = jnp.where(lane == k, r, rank_out)
    rank_ref[...] = rank_out.astype(jnp.int32)
    run_ref[...] = run_ref[...] + jnp.sum(chosen, axis=0, keepdims=True)
    cnt_ref[...] = run_ref[...]


def moe_router(x, w_pad, b_pad, tm):
    m, d = x.shape
    assert m % tm == 0, (m, tm)
    tri = jnp.asarray(np.tril(np.ones((tm, tm), np.float32), -1), BF)
    row = lambda dt: jax.ShapeDtypeStruct((m, LANES), dt)
    return pl.pallas_call(
        _router_kernel,
        grid=(m // tm,),
        in_specs=[pl.BlockSpec((tm, d), lambda i: (i, 0)),
                  pl.BlockSpec((d, LANES), lambda i: (0, 0)),
                  pl.BlockSpec((1, LANES), lambda i: (0, 0)),
                  pl.BlockSpec((tm, tm), lambda i: (0, 0))],
        out_specs=[pl.BlockSpec((tm, LANES), lambda i: (i, 0)),
                   pl.BlockSpec((tm, LANES), lambda i: (i, 0)),
                   pl.BlockSpec((tm, LANES), lambda i: (i, 0)),
                   pl.BlockSpec((1, LANES), lambda i: (0, 0))],
        out_shape=[row(jnp.int32), row(F32), row(jnp.int32), jax.ShapeDtypeStruct((1, LANES), F32)],
        scratch_shapes=[pltpu.VMEM((1, LANES), F32)],
        compiler_params=_cparams(("arbitrary",)),
        name="moe_router",
    )(x, w_pad, b_pad, tri)


def _slot_ids_to_smem(slot_ref, ids_ref, sem):
    cp = pltpu.make_async_copy(slot_ref.at[0], ids_ref, sem)
    cp.start()
    cp.wait()


def _dispatch_kernel(pad0_ref, npad_ref, slot_ref, x_ref, xg_ref, ids_ref, zero_ref, sem_ids, sem, sem_z, *, tn):
    @pl.when(pl.program_id(0) == 0)
    def _():
        zero_ref[...] = jnp.zeros(zero_ref.shape, zero_ref.dtype)

        def per_expert(e, carry):
            zr = zero_ref.shape[0]
            row0 = pad0_ref[e]
            n_one = jnp.minimum((-row0) & (zr - 1), npad_ref[e])
            row_al = row0 + n_one
            n_full = (npad_ref[e] - n_one) // zr

            def start_full(r, c):
                dst = xg_ref.at[pl.ds(pl.multiple_of(row_al + r * zr, zr), zr), :]
                pltpu.make_async_copy(zero_ref, dst, sem_z).start()
                return c

            def wait_full(r, c):
                pltpu.make_async_copy(zero_ref, xg_ref.at[pl.ds(0, zr), :], sem_z).wait()
                return c

            def start_one(r, c):
                pltpu.make_async_copy(zero_ref.at[pl.ds(0, 1), :], xg_ref.at[pl.ds(row0 + r, 1), :], sem_z).start()
                return c

            def wait_one(r, c):
                pltpu.make_async_copy(zero_ref.at[pl.ds(0, 1), :], xg_ref.at[pl.ds(0, 1), :], sem_z).wait()
                return c

            lax.fori_loop(0, n_full, start_full, 0)
            lax.fori_loop(0, n_one, start_one, 0)
            lax.fori_loop(0, n_full, wait_full, 0)
            lax.fori_loop(0, n_one, wait_one, 0)
            return carry

        lax.fori_loop(0, pad0_ref.shape[0], per_expert, 0)

    _slot_ids_to_smem(slot_ref, ids_ref, sem_ids)

    def issue(t, carry):
        for k in range(TOP_K):
            pltpu.make_async_copy(x_ref.at[pl.ds(t, 1), :],
                                  xg_ref.at[pl.ds(ids_ref[0, t * TOP_K + k], 1), :], sem).start()
        return carry

    lax.fori_loop(0, tn, issue, 0, unroll=8)
    for k in range(TOP_K):
        pltpu.make_async_copy(x_ref, xg_ref.at[pl.ds(0, tn), :], sem).wait()


def moe_dispatch(x, slots, pad0, npad, cap):
    n, d = x.shape
    tn = MOE_TOK
    assert n % tn == 0, (n, tn)
    assert MOE_ZERO_ROWS & (MOE_ZERO_ROWS - 1) == 0 and MOE_TM % MOE_ZERO_ROWS == 0 and MOE_ZERO_ROWS % 8 == 0
    return pl.pallas_call(
        functools.partial(_dispatch_kernel, tn=tn),
        grid_spec=pltpu.PrefetchScalarGridSpec(
            num_scalar_prefetch=2,
            grid=(n // tn,),
            in_specs=[pl.BlockSpec((1, 1, TOP_K * tn), lambda i, p0, np_: (i, 0, 0)),
                      pl.BlockSpec((tn, d), lambda i, p0, np_: (i, 0))],
            out_specs=pl.BlockSpec(memory_space=pl.ANY),
            scratch_shapes=[pltpu.SMEM((1, TOP_K * tn), jnp.int32),
                            pltpu.VMEM((MOE_ZERO_ROWS, d), x.dtype),
                            pltpu.SemaphoreType.DMA(()),
                            pltpu.SemaphoreType.DMA(()),
                            pltpu.SemaphoreType.DMA(())]),
        out_shape=jax.ShapeDtypeStruct((cap, d), x.dtype),
        compiler_params=_cparams(("arbitrary",)),
        name="moe_dispatch",
    )(pad0, npad, slots.reshape(n // tn, 1, TOP_K * tn), x)


def _moe_up_kernel(be_ref, nu_ref, x_ref, wg_ref, wu_ref, bg_ref, bu_ref, o_ref, wgb_ref, wub_ref):
    i = pl.program_id(1)
    changed = jnp.logical_or(i == 0, be_ref[i] != be_ref[jnp.maximum(i - 1, 0)])

    @pl.when(changed)
    def _():
        wgb_ref[...] = wg_ref[...].astype(BF)
        wub_ref[...] = wu_ref[...].astype(BF)

    @pl.when(i < nu_ref[0])
    def _():
        x = x_ref[...].astype(BF)
        g = jnp.minimum(_dot(x, wgb_ref[...]) + bg_ref[...], SWIGLU_LIMIT)
        u = jnp.clip(_dot(x, wub_ref[...]) + bu_ref[...], -SWIGLU_LIMIT, SWIGLU_LIMIT)
        o_ref[...] = ((u + 1.0) * g * _sigmoid(SWIGLU_ALPHA * g)).astype(o_ref.dtype)

    @pl.when(i >= nu_ref[0])
    def _():
        o_ref[...] = jnp.zeros(o_ref.shape, o_ref.dtype)


def moe_up(xg, blk_expert, n_used, w1, b1):
    cap, d = xg.shape
    n_blk = cap // MOE_TM
    n_j = D_FF // MOE_TN
    return pl.pallas_call(
        _moe_up_kernel,
        grid_spec=pltpu.PrefetchScalarGridSpec(
            num_scalar_prefetch=2,
            grid=(n_j, n_blk),
            in_specs=[pl.BlockSpec((MOE_TM, d), lambda j, i, be, nu: (jnp.minimum(i, nu[0] - 1), 0)),
                      pl.BlockSpec((None, d, MOE_TN), lambda j, i, be, nu: (be[i], 0, j)),
                      pl.BlockSpec((None, d, MOE_TN), lambda j, i, be, nu: (be[i], 0, n_j + j)),
                      pl.BlockSpec((None, 1, MOE_TN), lambda j, i, be, nu: (be[i], 0, j)),
                      pl.BlockSpec((None, 1, MOE_TN), lambda j, i, be, nu: (be[i], 0, n_j + j))],
            out_specs=pl.BlockSpec((MOE_TM, MOE_TN), lambda j, i, be, nu: (i, j)),
            scratch_shapes=[pltpu.VMEM((d, MOE_TN), BF), pltpu.VMEM((d, MOE_TN), BF)]),
        out_shape=jax.ShapeDtypeStruct((cap, D_FF), BF),
        compiler_params=_cparams(("arbitrary", "arbitrary")),
        name="moe_up",
    )(blk_expert, n_used, xg, w1, w1, b1, b1)


def _moe_down_kernel(be_ref, nu_ref, a_ref, w_ref, b_ref, o_ref, wb_ref):
    i = pl.program_id(1)
    changed = jnp.logical_or(i == 0, be_ref[i] != be_ref[jnp.maximum(i - 1, 0)])

    @pl.when(changed)
    def _():
        wb_ref[...] = w_ref[...].astype(BF)

    @pl.when(i < nu_ref[0])
    def _():
        o_ref[...] = _dot(a_ref[...], wb_ref[...]) + b_ref[...]

    @pl.when(i >= nu_ref[0])
    def _():
        o_ref[...] = jnp.zeros(o_ref.shape, o_ref.dtype)


def moe_down(act, blk_expert, n_used, w2, b2):
    cap, f = act.shape
    d = w2.shape[2]
    n_blk = cap // MOE_TM
    return pl.pallas_call(
        _moe_down_kernel,
        grid_spec=pltpu.PrefetchScalarGridSpec(
            num_scalar_prefetch=2,
            grid=(d // MOE_TN_DOWN, n_blk),
            in_specs=[pl.BlockSpec((MOE_TM, f), lambda j, i, be, nu: (jnp.minimum(i, nu[0] - 1), 0)),
                      pl.BlockSpec((None, f, MOE_TN_DOWN), lambda j, i, be, nu: (be[i], 0, j)),
                      pl.BlockSpec((None, 1, MOE_TN_DOWN), lambda j, i, be, nu: (be[i], 0, j))],
            out_specs=pl.BlockSpec((MOE_TM, MOE_TN_DOWN), lambda j, i, be, nu: (i, j)),
            scratch_shapes=[pltpu.VMEM((f, MOE_TN_DOWN), BF)]),
        out_shape=jax.ShapeDtypeStruct((cap, d), F32),
        compiler_params=_cparams(("arbitrary", "arbitrary")),
        name="moe_down",
    )(blk_expert, n_used, act, w2, b2)


def _combine_ln_kernel(slot_ref, yb_hbm, gate_ref, x_ref, g_ref, b_ref, o_ref, ids_ref, buf_ref, sem_ids, sem, *, tn):
    _slot_ids_to_smem(slot_ref, ids_ref, sem_ids)

    def issue(t, carry):
        for k in range(TOP_K):
            pltpu.make_async_copy(yb_hbm.at[pl.ds(ids_ref[0, t * TOP_K + k], 1), :],
                                  buf_ref.at[k, pl.ds(t, 1), :], sem).start()
        return carry

    lax.fori_loop(0, tn, issue, 0, unroll=8)
    for k in range(TOP_K):
        pltpu.make_async_copy(yb_hbm.at[pl.ds(0, tn), :], buf_ref.at[k], sem).wait()
    gate = gate_ref[...]
    y = gate[:, 0:1] * buf_ref[0]
    for k in range(1, TOP_K):
        y = y + gate[:, k:k + 1] * buf_ref[k]
    o_ref[...] = _layer_norm(DN_ALPHA * x_ref[...] + y, g_ref[...], b_ref[...])


def moe_combine_ln(yb, slots, gate, x, g, b):
    n, d = x.shape
    tn = MOE_TOK
    assert n % tn == 0, (n, tn)
    return pl.pallas_call(
        functools.partial(_combine_ln_kernel, tn=tn),
        grid=(n // tn,),
        in_specs=[pl.BlockSpec((1, 1, TOP_K * tn), lambda i: (i, 0, 0)),
                  pl.BlockSpec(memory_space=pl.ANY),
                  pl.BlockSpec((tn, LANES), lambda i: (i, 0)),
                  pl.BlockSpec((tn, d), lambda i: (i, 0)),
                  pl.BlockSpec((1, d), lambda i: (0, 0)),
                  pl.BlockSpec((1, d), lambda i: (0, 0))],
        out_specs=pl.BlockSpec((tn, d), lambda i: (i, 0)),
        out_shape=jax.ShapeDtypeStruct((n, d), F32),
        scratch_shapes=[pltpu.SMEM((1, TOP_K * tn), jnp.int32),
                        pltpu.VMEM((TOP_K, tn, d), F32),
                        pltpu.SemaphoreType.DMA(()),
                        pltpu.SemaphoreType.DMA(())],
        compiler_params=_cparams(("arbitrary",)),
        name="moe_combine_ln",
    )(slots.reshape(n // tn, 1, TOP_K * tn), yb, gate, x, g.reshape(1, d), b.reshape(1, d))


def moe_ffn_ln(x, router_w, router_b, w1, b1, w2, b2, e0, ln_g, ln_b):
    n, d = x.shape
    w_pad = jnp.pad(router_w, ((0, 0), (0, LANES - N_EXPERTS)))
    b_pad = jnp.pad(router_b, (0, LANES - N_EXPERTS)).reshape(1, LANES)
    idx, gate, rank, cnt = moe_router(x, w_pad, b_pad, 512)

    counts = cnt[0, :N_EXPERTS].astype(jnp.int32)
    padded = (counts + MOE_TM - 1) // MOE_TM * MOE_TM
    pad_end = jnp.cumsum(padded)
    pad_start = pad_end - padded
    onehot = idx[:, :TOP_K, None] == jnp.arange(N_EXPERTS, dtype=jnp.int32)
    slots = (jnp.sum(jnp.where(onehot, pad_start, 0), axis=-1) + rank[:, :TOP_K]).reshape(-1)
    n_blk = -(-(n * TOP_K + N_EXPERTS * (MOE_TM - 1)) // MOE_TM)
    blk_row0 = jnp.arange(n_blk, dtype=jnp.int32) * MOE_TM
    blk_expert = e0 + jnp.minimum(jnp.sum(pad_end[None, :] <= blk_row0[:, None], axis=1), N_EXPERTS - 1)
    n_used = (pad_end[-1] // MOE_TM).astype(jnp.int32).reshape(1)

    cap = n_blk * MOE_TM
    pad0 = jnp.concatenate([pad_start + counts, pad_end[-1:]]).astype(jnp.int32)
    npad = jnp.concatenate([padded - counts, cap - pad_end[-1:]]).astype(jnp.int32)
    xg = moe_dispatch(x, slots, pad0, npad, cap)
    act = moe_up(xg, blk_expert.astype(jnp.int32), n_used, w1, b1)
    yb = moe_down(act, blk_expert.astype(jnp.int32), n_used, w2, b2)
    return moe_combine_ln(yb, slots, gate, x, ln_g, ln_b)


def _pad_heads(w):
    lead = w.shape[:-1]
    w = w.reshape(lead + (GLA_H, GLA_DK))
    w = jnp.pad(w, [(0, 0)] * len(lead) + [(0, 0), (0, LANES - GLA_DK)])
    return w.reshape(lead + (GLA_H * LANES,))


def _tail_column_map():
    c = np.cumsum((3 * NSA_H, GLA_H * GLA_DK, GLA_H * GLA_DK, GROUP_W, GLA_RANK, GROUP_W))
    dst = np.zeros(c[-1], np.int64)
    src = np.arange(c[-1])
    rel = lambda cb: (cb - CB_MISC) * LANES
    dst[:c[0]] = rel(CB_MISC) + MISC_GATE0 + src[:c[0]]
    for lo, hi, cb in ((c[0], c[1], CB_QD), (c[1], c[2], CB_KD)):
        k = src[lo:hi] - lo
        dst[lo:hi] = rel(cb) + (k // GLA_DK) * LANES + k % GLA_DK
    dst[c[2]:c[3]] = rel(CB_VD) + src[c[2]:c[3]] - c[2]
    dst[c[3]:c[4]] = rel(CB_MISC) + MISC_LR0 + src[c[3]:c[4]] - c[3]
    dst[c[4]:c[5]] = rel(CB_GOD) + src[c[4]:c[5]] - c[4]
    return dst


def _relayout_w_in(w):
    wb = w.astype(BF)
    n_keep = CB_MISC * LANES
    dst = _tail_column_map()
    sel = np.zeros((dst.shape[0], D_PROJ - n_keep), np.float32)
    sel[np.arange(dst.shape[0]), dst] = 1.0
    tail = jnp.dot(wb[:, n_keep:], jnp.asarray(sel, BF), preferred_element_type=F32).astype(BF)
    return jnp.concatenate([wb[:, :n_keep], tail], axis=1)


def kernel(x_prompt, x_sample, cache_pool, cache_nsa_kv, cache_win_kv, state_gla, cache_mem_kv, page_table,
           mem_prompt, ln_in_g, ln_in_b, w_in, w_out, pool_w, pool_scale, gmlp_ln_g, gmlp_ln_b, gmlp_ws, gmlp_bs,
           nsa_phi, gla_wa, gla_ba, gla_norm_g, ln1_g, ln1_b, mem_wq, mem_wkv, mem_wo, ln2_g, ln2_b,
           router_w, router_b, moe_w1, moe_b1, moe_w2, moe_b2, ln3_g, ln3_b):
    n_bp, t_p, d = x_prompt.shape
    n_bs, t_s, _ = x_sample.shape
    n_p = n_bp * t_p
    n_s = n_bs * t_s
    n_pool = cache_nsa_kv.shape[1]
    page = cache_nsa_kv.shape[2]
    past_len = page_table.shape[1] * page
    n_wc = cache_win_kv.shape[2]
    assert t_p % 512 == 0 and n_s % 512 == 0 and t_s == 8 and past_len % SEL_BLOCK == 0 and n_wc == WINDOW

    x = jnp.concatenate([ln_rows(x_prompt.reshape(n_p, d), ln_in_g, ln_in_b, 512),
                         ln_rows(x_sample.reshape(n_s, d), ln_in_g, ln_in_b, 512)], axis=0)
    n_tok = n_p + n_s
    tm_big = 1024 if n_tok % 1024 == 0 else 512
    cpp = page // CMP_BLOCK
    cache_blocks = cache_nsa_kv.reshape(DEPTH * n_pool * cpp, CMP_BLOCK, 4, NSA_DH)
    state0 = state_gla.reshape(DEPTH * n_bs, GLA_H, GLA_DK, GLA_DV)
    w1_all = moe_w1.reshape(DEPTH * N_EXPERTS, d, 2 * D_FF)
    b1_all = moe_b1.reshape(DEPTH * N_EXPERTS, 1, 2 * D_FF)
    w2_all = moe_w2.reshape(DEPTH * N_EXPERTS, D_FF, d)
    b2_all = moe_b2.reshape(DEPTH * N_EXPERTS, 1, d)

    expand_p = (np.arange(t_p)[None, :] // SEL_BLOCK == np.arange(t_p // SEL_BLOCK)[:, None])
    expand_p = jnp.asarray(expand_p, BF)
    expand_s = (np.arange(past_len + LANES)[None, :] // SEL_BLOCK == np.arange(LANES)[:, None])
    expand_s = jnp.asarray(expand_s, BF)
    pair_s = jnp.asarray(np.arange(past_len // CMP_BLOCK)[:, None] // 2 == np.arange(LANES)[None, :], BF)
    bb_s = LANES // t_s
    eye_bb = jnp.eye(bb_s, dtype=F32)

    outs = {k: [] for k in ("pool_p", "nsa_p", "win_p", "gla_p", "mem_p", "pool_s", "chunk_s", "nsa_s", "win_s",
                            "gla_s")}
    for l in range(DEPTH):
        proj = matmul(x, _relayout_w_in(w_in[l]), tm_big, 1024)

        pre_p = jnp.zeros((n_bp, POOL_PRE, GROUP_W), F32)
        pre_s = jnp.pad(cache_pool[l], ((0, 0), (1, 0), (0, 0)))
        ya = jnp.concatenate([
            pool_mixer(proj, 0, n_bp, t_p, pre_p, pool_w[l], pool_scale[l], 0, 1, 512),
            pool_mixer(proj, n_p, n_bs, t_s, pre_s, pool_w[l], pool_scale[l], past_len, bb_s, t_s)], axis=0)

        tril = jnp.tril(jnp.ones((GMLP_CHUNK, GMLP_CHUNK), F32))
        w_mix_p = (gmlp_ws[l] * tril).astype(BF)
        bias_p = jnp.repeat(gmlp_bs[l].T, LANES, axis=1)
        ws_s = gmlp_ws[l][:, :t_s, :t_s] * tril[:t_s, :t_s]
        w_mix_s = jnp.einsum('ab,gts->gatbs', eye_bb, ws_s).reshape(4, LANES, LANES).astype(BF)
        bias_s = jnp.tile(jnp.repeat(gmlp_bs[l][:, :t_s].T, LANES, axis=1), (bb_s, 1))
        yb_p, _ = gmlp_mixer(proj, 0, n_p, gmlp_ln_g[l], gmlp_ln_b[l], w_mix_p, bias_p, 4)
        yb_s, v_s = gmlp_mixer(proj, n_p, n_s, gmlp_ln_g[l], gmlp_ln_b[l], w_mix_s, bias_s, 4)
        yb = jnp.concatenate([yb_p, yb_s], axis=0)

        phi2 = nsa_phi[l].astype(BF).reshape(2, CMP_BLOCK // 2, 2 * NSA_DH, NSA_DH)
        kc_p, vc_p = cmp_project(proj, n_p // CMP_BLOCK, phi2[0], phi2[1], t_p // CMP_BLOCK)
        yc_p = nsa_prompt(proj, n_bp, t_p, kc_p, vc_p, expand_p)
        kvc_pool = cmp_pool(cache_blocks, l * n_pool * cpp, n_pool * cpp, phi2[0], phi2[1], 128)
        yc_s = nsa_sample(proj, n_p, n_bs, t_s, cache_nsa_kv, l, page_table,
                          kvc_pool.reshape(n_pool, cpp, 2 * NSA_DH), cache_win_kv, pair_s, expand_s)
        yc = jnp.concatenate([yc_p, yc_s], axis=0)

        wa_pad = jnp.zeros((LANES, GLA_H * LANES), F32).at[MISC_LR0:MISC_LR0 + GLA_RANK].set(_pad_heads(gla_wa[l]))
        wa_pad = wa_pad.astype(BF)
        ba_pad = _pad_heads(gla_ba[l]).reshape(1, GLA_H * LANES)
        yd_p, st_p = gla_prompt(proj, n_bp, t_p, wa_pad, ba_pad, gla_norm_g[l])
        yd_s, st_s = gla_sample(proj, n_p, n_bs, t_s, wa_pad, ba_pad, gla_norm_g[l], state0, l * n_bs, bb_s)
        yd = jnp.concatenate([yd_p, yd_s], axis=0)

        x = matmul_res_ln([ya, yb, yc, yd], w_out[l].astype(BF), x, ln1_g[l], ln1_b[l], 256)

        qm = matmul(x, mem_wq[l].astype(BF), tm_big, 512)
        mem_kv_p = matmul(mem_prompt.reshape(n_bp * MEM_LEN, d), mem_wkv[l].astype(BF), 512, 512)
        mem_kv_p = mem_kv_p.reshape(1, n_bp, MEM_LEN, 2, MEM_H, MEM_DH)
        om = jnp.concatenate([
            mem_attention(qm, 0, n_bp, t_p, mem_kv_p, 0, 512),
            mem_attention(qm, n_p, n_bs, t_s, cache_mem_kv, l, t_s)], axis=0)
        x = matmul_res_ln([om], mem_wo[l].astype(BF), x, ln2_g[l], ln2_b[l], 256)

        x = moe_ffn_ln(x, router_w[l], router_b[l], w1_all, b1_all, w2_all, b2_all, l * N_EXPERTS,
                       ln3_g[l], ln3_b[l])

        seqs_p = lambda c0, c1: proj[:n_p, c0 * LANES:c1 * LANES].reshape(n_bp, t_p, (c1 - c0) * LANES)
        seqs_s = lambda c0, c1: proj[n_p:, c0 * LANES:c1 * LANES].reshape(n_bs, t_s, (c1 - c0) * LANES)
        outs["pool_p"].append(seqs_p(CB_XA, CB_U)[:, t_p - POOL_BUF:])
        outs["nsa_p"].append(seqs_p(CB_CMPK, CB_WINK).reshape(n_bp, t_p, 4, NSA_DH))
        outs["win_p"].append(seqs_p(CB_WINK, CB_MISC)[:, t_p - min(WINDOW, t_p):]
                             .reshape(n_bp, min(WINDOW, t_p), 2, NSA_DH))
        outs["gla_p"].append(st_p)
        outs["mem_p"].append(mem_kv_p[0])
        outs["pool_s"].append(seqs_s(CB_XA, CB_U))
        outs["chunk_s"].append(v_s.reshape(n_bs, t_s, GROUP_W))
        outs["nsa_s"].append(seqs_s(CB_CMPK, CB_WINK).reshape(n_bs, t_s, 4, NSA_DH))
        outs["win_s"].append(seqs_s(CB_WINK, CB_MISC).reshape(n_bs, t_s, 2, NSA_DH))
        outs["gla_s"].append(st_s)

    st = lambda k: jnp.stack(outs[k])
    return (x[:n_p].reshape(n_bp, t_p, d), x[n_p:].reshape(n_bs, t_s, d), st("pool_p"), st("nsa_p"), st("win_p"),
            st("gla_p"), st("mem_p"), st("pool_s"), st("chunk_s"), st("nsa_s"), st("win_s"), st("gla_s"))
```

```python
import functools

import jax
import jax.numpy as jnp
import numpy as np
from jax import lax
from jax.experimental import pallas as pl
from jax.experimental.pallas import tpu as pltpu

BF = jnp.bfloat16
F32 = jnp.float32

D_MODEL = 2048
DEPTH = 2
GROUP_W = 512
LANES = 128
POOL_WINDOWS = (2, 4, 8, 16)
POOL_BUF = 15
POOL_PRE = 16
GMLP_CHUNK = 128
NSA_DH = 128
NSA_H = 4
CMP_BLOCK = 32
SEL_BLOCK = 64
SEL_TOP = 16
WINDOW = 512
Q_BLOCK = 128
GLA_H = 4
GLA_DK = 64
GLA_DV = 128
GLA_RANK = 16
GLA_TAU = 16.0
GLA_CHUNK = 64
MEM_LEN = 256
MEM_H = 4
MEM_DH = 128
N_EXPERTS = 32
TOP_K = 4
D_FF = 2048
SWIGLU_LIMIT = 7.0
SWIGLU_ALPHA = 1.702
DN_ALPHA = (2.0 * DEPTH) ** 0.25
LN_EPS = 1e-5
NEG_INF = float("-inf")

CB_XA, CB_U, CB_V, CB_QC = 0, 4, 8, 12
CB_CMPK, CB_CMPV, CB_SELK, CB_SELV, CB_WINK, CB_WINV = 16, 17, 18, 19, 20, 21
CB_MISC, CB_QD, CB_KD, CB_VD, CB_GOD = 22, 24, 28, 32, 36
N_CB = 40
D_PROJ = N_CB * LANES
MISC_GATE0 = 0
MISC_LR0 = 12

MOE_TM = 512
MOE_TN = 512
MOE_TN_DOWN = 1024
MOE_TOK = 512
MOE_ZERO_ROWS = 32
VMEM_LIMIT = 56 * 1024 * 1024


def _cparams(sem):
    return pltpu.CompilerParams(dimension_semantics=sem, vmem_limit_bytes=VMEM_LIMIT)


def _layer_norm(x, g, b):
    mu = jnp.mean(x, -1, keepdims=True)
    xc = x - mu
    var = jnp.mean(xc * xc, -1, keepdims=True)
    return xc * lax.rsqrt(var + LN_EPS) * g + b


def _dot(a, b):
    return jnp.dot(a, b, preferred_element_type=F32)


def _dot_nt(a, b):
    return lax.dot_general(a, b, (((1,), (1,)), ((), ())), preferred_element_type=F32)


def _dot_tn(a, b):
    return lax.dot_general(a, b, (((0,), (0,)), ((), ())), preferred_element_type=F32)


def _sigmoid(x):
    return 1.0 / (1.0 + jnp.exp(-x))


def _ln_kernel(x_ref, g_ref, b_ref, o_ref):
    o_ref[...] = _layer_norm(x_ref[...], g_ref[...], b_ref[...])


def ln_rows(x, g, b, tm):
    m, d = x.shape
    assert m % tm == 0, (m, tm)
    return pl.pallas_call(
        _ln_kernel,
        grid=(m // tm,),
        in_specs=[pl.BlockSpec((tm, d), lambda i: (i, 0)),
                  pl.BlockSpec((1, d), lambda i: (0, 0)),
                  pl.BlockSpec((1, d), lambda i: (0, 0))],
        out_specs=pl.BlockSpec((tm, d), lambda i: (i, 0)),
        out_shape=jax.ShapeDtypeStruct((m, d), F32),
        compiler_params=_cparams(("parallel",)),
        name="ln_rows",
    )(x, g.reshape(1, d), b.reshape(1, d))


def _mm_kernel(x_ref, w_ref, o_ref, xb_ref):
    @pl.when(pl.program_id(1) == 0)
    def _():
        xb_ref[...] = x_ref[...].astype(BF)

    o_ref[...] = _dot(xb_ref[...], w_ref[...]).astype(o_ref.dtype)


def matmul(x, w, tm, tn, out_dtype=F32):
    m, k = x.shape
    n = w.shape[1]
    assert m % tm == 0 and n % tn == 0, (m, tm, n, tn)
    return pl.pallas_call(
        _mm_kernel,
        grid=(m // tm, n // tn),
        in_specs=[pl.BlockSpec((tm, k), lambda i, j: (i, 0)),
                  pl.BlockSpec((k, tn), lambda i, j: (0, j))],
        out_specs=pl.BlockSpec((tm, tn), lambda i, j: (i, j)),
        out_shape=jax.ShapeDtypeStruct((m, n), out_dtype),
        scratch_shapes=[pltpu.VMEM((tm, k), BF)],
        compiler_params=_cparams(("parallel", "arbitrary")),
        name="matmul",
    )(x, w)


def _mm_res_ln_kernel(*refs, n_in):
    xs = refs[:n_in]
    w_ref, r_ref, g_ref, b_ref, o_ref = refs[n_in:]
    acc = None
    k0 = 0
    for x_ref in xs:
        kk = x_ref.shape[1]
        part = _dot(x_ref[...].astype(BF), w_ref[k0:k0 + kk, :])
        acc = part if acc is None else acc + part
        k0 += kk
    o_ref[...] = _layer_norm(DN_ALPHA * r_ref[...] + acc, g_ref[...], b_ref[...])


def matmul_res_ln(xs, w, resid, g, b, tm):
    m, d = resid.shape
    assert m % tm == 0, (m, tm)
    in_specs = [pl.BlockSpec((tm, x.shape[1]), lambda i: (i, 0)) for x in xs]
    in_specs += [pl.BlockSpec(w.shape, lambda i: (0, 0)),
                 pl.BlockSpec((tm, d), lambda i: (i, 0)),
                 pl.BlockSpec((1, d), lambda i: (0, 0)),
                 pl.BlockSpec((1, d), lambda i: (0, 0))]
    return pl.pallas_call(
        functools.partial(_mm_res_ln_kernel, n_in=len(xs)),
        grid=(m // tm,),
        in_specs=in_specs,
        out_specs=pl.BlockSpec((tm, d), lambda i: (i, 0)),
        out_shape=jax.ShapeDtypeStruct((m, d), F32),
        compiler_params=_cparams(("parallel",)),
        name="matmul_res_ln",
    )(*xs, w, resid, g.reshape(1, d), b.reshape(1, d))


def _pool_kernel(x_ref, pre_ref, w_ref, sc_ref, o_ref, ext_ref, *, bb, tt, n_tiles, t0):
    j = pl.program_id(1)

    @pl.when(j == 0)
    def _():
        ext_ref[:, 0:POOL_PRE, :] = pre_ref[...]

    x = x_ref[...].reshape(bb, tt, GROUP_W)
    ext_ref[:, POOL_PRE:, :] = x
    t_idx = j * tt + lax.broadcasted_iota(jnp.int32, (1, tt, LANES), 1)
    for g, win in enumerate(POOL_WINDOWS):
        cs = slice(g * LANES, (g + 1) * LANES)
        xg = ext_ref[:, POOL_PRE:POOL_PRE + tt, cs]
        s = xg
        for k in range(1, win):
            s = s + ext_ref[:, POOL_PRE - k:POOL_PRE - k + tt, cs]
        cnt = jnp.minimum(win, t0 + 1 + t_idx).astype(F32)
        d = s / cnt - xg
        y = _dot(d.reshape(bb * tt, LANES).astype(BF), w_ref[g])
        o_ref[:, cs] = (y * sc_ref[:, cs]).astype(o_ref.dtype)
    if n_tiles > 1:
        ext_ref[:, 0:POOL_PRE, :] = ext_ref[:, tt:tt + POOL_PRE, :]


def pool_mixer(proj, row0, n_b, t_len, prefix, pool_w, pool_scale, t0, bb, tt):
    n_tiles = t_len // tt
    rb = bb * tt
    base = row0 // rb
    return pl.pallas_call(
        functools.partial(_pool_kernel, bb=bb, tt=tt, n_tiles=n_tiles, t0=t0),
        grid=(n_b // bb, n_tiles),
        in_specs=[pl.BlockSpec((rb, GROUP_W), lambda i, j: (base + i * n_tiles + j, CB_XA // 4)),
                  pl.BlockSpec((bb, POOL_PRE, GROUP_W), lambda i, j: (i, 0, 0)),
                  pl.BlockSpec((4, LANES, LANES), lambda i, j: (0, 0, 0)),
                  pl.BlockSpec((1, GROUP_W), lambda i, j: (0, 0))],
        out_specs=pl.BlockSpec((rb, GROUP_W), lambda i, j: (i * n_tiles + j, 0)),
        out_shape=jax.ShapeDtypeStruct((n_b * t_len, GROUP_W), BF),
        scratch_shapes=[pltpu.VMEM((bb, POOL_PRE + tt, GROUP_W), F32)],
        compiler_params=_cparams(("parallel", "arbitrary")),
        name="pool_mixer",
    )(proj, prefix, pool_w.astype(BF), pool_scale.reshape(1, GROUP_W))


def _gelu_tanh(x):
    return 0.5 * x * (1.0 + jnp.tanh(np.sqrt(2.0 / np.pi) * (x + 0.044715 * (x * x * x))))


def _gmlp_kernel(u_ref, v_ref, g_ref, b_ref, w_ref, bias_ref, y_ref, vo_ref, *, n_ch):
    for c in range(n_ch):
        rows = slice(c * GMLP_CHUNK, (c + 1) * GMLP_CHUNK)
        u = _gelu_tanh(u_ref[rows, :])
        v = _layer_norm(_gelu_tanh(v_ref[rows, :]), g_ref[...], b_ref[...])
        vo_ref[rows, :] = v
        vb = v.astype(BF)
        for g in range(4):
            cs = slice(g * LANES, (g + 1) * LANES)
            mixed = _dot(w_ref[g], vb[:, cs]) + bias_ref[:, cs]
            y_ref[rows, cs] = (u[:, cs] * mixed).astype(y_ref.dtype)


def gmlp_mixer(proj, row0, n_rows, ln_g, ln_b, w_mix, bias_full, n_ch):
    rb = n_ch * GMLP_CHUNK
    base = row0 // rb
    return pl.pallas_call(
        functools.partial(_gmlp_kernel, n_ch=n_ch),
        grid=(n_rows // rb,),
        in_specs=[pl.BlockSpec((rb, GROUP_W), lambda i: (base + i, CB_U // 4)),
                  pl.BlockSpec((rb, GROUP_W), lambda i: (base + i, CB_V // 4)),
                  pl.BlockSpec((1, GROUP_W), lambda i: (0, 0)),
                  pl.BlockSpec((1, GROUP_W), lambda i: (0, 0)),
                  pl.BlockSpec((4, GMLP_CHUNK, GMLP_CHUNK), lambda i: (0, 0, 0)),
                  pl.BlockSpec((GMLP_CHUNK, GROUP_W), lambda i: (0, 0))],
        out_specs=[pl.BlockSpec((rb, GROUP_W), lambda i: (i, 0)),
                   pl.BlockSpec((rb, GROUP_W), lambda i: (i, 0))],
        out_shape=[jax.ShapeDtypeStruct((n_rows, GROUP_W), BF),
                   jax.ShapeDtypeStruct((n_rows, GROUP_W), F32)],
        compiler_params=_cparams(("parallel",)),
        name="gmlp_mixer",
    )(proj, proj, ln_g.reshape(1, GROUP_W), ln_b.reshape(1, GROUP_W), w_mix, bias_full)


def _cmp_accumulate(row_pair, phi_ref):
    acc = None
    for l2 in range(CMP_BLOCK // 2):
        xa, xb = row_pair(l2)
        part = _dot(jnp.concatenate([xa, xb], axis=1).astype(BF), phi_ref[l2])
        acc = part if acc is None else acc + part
    return acc


def _cmp_kernel(xk_ref, xv_ref, phik_ref, phiv_ref, ok_ref, ov_ref, *, tb):
    for x_ref, phi_ref, o_ref in ((xk_ref, phik_ref, ok_ref), (xv_ref, phiv_ref, ov_ref)):
        o_ref[...] = _cmp_accumulate(
            lambda l2: (x_ref[pl.ds(2 * l2, tb, stride=CMP_BLOCK), :],
                        x_ref[pl.ds(2 * l2 + 1, tb, stride=CMP_BLOCK), :]), phi_ref)


def cmp_project(proj, n_blk, phik2, phiv2, tb):
    assert n_blk % tb == 0, (n_blk, tb)
    rows = tb * CMP_BLOCK
    return pl.pallas_call(
        functools.partial(_cmp_kernel, tb=tb),
        grid=(n_blk // tb,),
        in_specs=[pl.BlockSpec((rows, LANES), lambda i: (i, CB_CMPK)),
                  pl.BlockSpec((rows, LANES), lambda i: (i, CB_CMPV)),
                  pl.BlockSpec(phik2.shape, lambda i: (0, 0, 0)),
                  pl.BlockSpec(phiv2.shape, lambda i: (0, 0, 0))],
        out_specs=[pl.BlockSpec((tb, NSA_DH), lambda i: (i, 0)),
                   pl.BlockSpec((tb, NSA_DH), lambda i: (i, 0))],
        out_shape=[jax.ShapeDtypeStruct((n_blk, NSA_DH), F32),
                   jax.ShapeDtypeStruct((n_blk, NSA_DH), F32)],
        compiler_params=_cparams(("parallel",)),
        name="cmp_project",
    )(proj, proj, phik2, phiv2)


def _cmp_pool_kernel(x_ref, phik_ref, phiv_ref, o_ref):
    for s, phi_ref in ((0, phik_ref), (1, phiv_ref)):
        o_ref[:, s * NSA_DH:(s + 1) * NSA_DH] = _cmp_accumulate(
            lambda l2: (x_ref[:, 2 * l2, s, :], x_ref[:, 2 * l2 + 1, s, :]), phi_ref)


def cmp_pool(cache_blocks, blk0, n_blk, phik2, phiv2, tb):
    assert n_blk % tb == 0 and blk0 % tb == 0, (n_blk, blk0, tb)
    base = blk0 // tb
    return pl.pallas_call(
        _cmp_pool_kernel,
        grid=(n_blk // tb,),
        in_specs=[pl.BlockSpec((tb, CMP_BLOCK, 4, NSA_DH), lambda i: (base + i, 0, 0, 0)),
                  pl.BlockSpec(phik2.shape, lambda i: (0, 0, 0)),
                  pl.BlockSpec(phiv2.shape, lambda i: (0, 0, 0))],
        out_specs=pl.BlockSpec((tb, 2 * NSA_DH), lambda i: (i, 0)),
        out_shape=jax.ShapeDtypeStruct((n_blk, 2 * NSA_DH), F32),
        compiler_params=_cparams(("parallel",)),
        name="cmp_pool",
    )(cache_blocks, phik2, phiv2)


def _alibi_slope(h):
    return 2.0 ** (-8.0 * (h + 1) / NSA_H)


def _head_rows_const(n_q, fn):
    r = lax.broadcasted_iota(jnp.int32, (NSA_H * n_q, 1), 0)
    out = jnp.full((NSA_H * n_q, 1), fn(NSA_H - 1), F32)
    for h in range(NSA_H - 2, -1, -1):
        out = jnp.where(r < (h + 1) * n_q, fn(h), out)
    return out


def _stack_heads(q):
    return jnp.concatenate([q[:, h * NSA_DH:(h + 1) * NSA_DH] for h in range(NSA_H)], axis=0)


def _select_blocks(score, n_blk):
    blk = lax.broadcasted_iota(jnp.int32, score.shape, 0)
    cnt = jnp.zeros(score.shape, F32)
    for i in range(n_blk):
        row = score[i:i + 1, :]
        beats = (row > score) | ((row == score) & (blk > i))
        cnt = cnt + jnp.where(beats, 1.0, 0.0)
    return jnp.where((cnt < float(SEL_TOP)) & (score > NEG_INF), 1.0, 0.0)


def _softmax_segments(segs):
    m = None
    for s, _ in segs:
        mi = jnp.max(s, axis=1, keepdims=True)
        m = mi if m is None else jnp.maximum(m, mi)
    m = jnp.where(m > NEG_INF, m, 0.0)
    den = None
    acc = None
    for s, v in segs:
        e = jnp.exp(s - m)
        d = jnp.sum(e, axis=1, keepdims=True)
        a = _dot(e.astype(BF), v)
        den = d if den is None else den + d
        acc = a if acc is None else acc + a
    return acc / jnp.where(den > 0, den, 1.0)


def _gated_heads(o_ref, gates, o_c, o_s, o_w, n_q):
    for h in range(NSA_H):
        rows = slice(h * n_q, (h + 1) * n_q)
        c = MISC_GATE0 + 3 * h
        out = (gates[:, c:c + 1] * o_c[rows, :] + gates[:, c + 1:c + 2] * o_s[rows, :]
               + gates[:, c + 2:c + 3] * o_w[rows, :])
        o_ref[:, h * NSA_DH:(h + 1) * NSA_DH] = out.astype(o_ref.dtype)


def _nsa_prompt_kernel(q_ref, misc_ref, kc_ref, vc_ref, ks_ref, vs_ref, kw_ref, vw_ref, e_ref, o_ref,
                       imp_ref, mask_ref, m_ref, l_ref, acc_ref, *, t_len):
    n_q = Q_BLOCK
    n_cmp = t_len // CMP_BLOCK
    n_sel = t_len // SEL_BLOCK
    kt = 512
    q0 = pl.program_id(1) * n_q
    q4 = _stack_heads(q_ref[...] * (NSA_DH ** -0.5)).astype(BF)
    slope_r = _head_rows_const(n_q, _alibi_slope)
    qpos_r = q0 + lax.rem(lax.broadcasted_iota(jnp.int32, (NSA_H * n_q, 1), 0), n_q)

    lane = lax.broadcasted_iota(jnp.int32, (1, NSA_H * n_q), 1)
    qpos_l = q0 + lax.rem(lane, n_q)
    slope_l = jnp.full((1, NSA_H * n_q), _alibi_slope(NSA_H - 1), F32)
    for h in range(NSA_H - 2, -1, -1):
        slope_l = jnp.where(lane < (h + 1) * n_q, _alibi_slope(h), slope_l)
    cmp_end = lax.broadcasted_iota(jnp.int32, (n_cmp, 1), 0) * CMP_BLOCK + (CMP_BLOCK - 1)
    s_c = _dot_nt(kc_ref[...].astype(BF), q4) - slope_l * (qpos_l - cmp_end).astype(F32)
    s_c = jnp.where(cmp_end <= qpos_l, s_c, NEG_INF)
    m_c = jnp.max(s_c, axis=0, keepdims=True)
    m_c = jnp.where(m_c > NEG_INF, m_c, 0.0)
    e_c = jnp.exp(s_c - m_c)
    den_c = jnp.sum(e_c, axis=0, keepdims=True)
    p_c = e_c / jnp.where(den_c > 0, den_c, 1.0)
    o_c = _dot_tn(p_c.astype(BF), vc_ref[...].astype(BF))

    imp = p_c[:, 0:n_q]
    for h in range(1, NSA_H):
        imp = imp + p_c[:, h * n_q:(h + 1) * n_q]
    imp_ref[...] = imp
    imp = imp_ref[pl.ds(0, n_sel, stride=2), :] + imp_ref[pl.ds(1, n_sel, stride=2), :]
    blk = lax.broadcasted_iota(jnp.int32, (n_sel, 1), 0)
    cur = (q0 + lax.broadcasted_iota(jnp.int32, (1, n_q), 1)) // SEL_BLOCK
    forced = (blk == 0) | (blk == cur) | (blk == cur - 1)
    score = jnp.where(blk <= cur, jnp.where(forced, jnp.inf, imp), NEG_INF)
    sel = _select_blocks(score, n_sel)
    mask_ref[...] = _dot_tn(sel.astype(BF), e_ref[...])

    m_ref[...] = jnp.full(m_ref.shape, NEG_INF, F32)
    l_ref[...] = jnp.zeros(l_ref.shape, F32)
    acc_ref[...] = jnp.zeros(acc_ref.shape, F32)

    def sel_tile(t, carry):
        k0 = pl.multiple_of(t * kt, kt)
        s = _dot_nt(q4, ks_ref[pl.ds(k0, kt), :].astype(BF))
        d = qpos_r - (k0 + lax.broadcasted_iota(jnp.int32, (1, kt), 1))
        mk = mask_ref[:, pl.ds(k0, kt)]
        mk = jnp.concatenate([mk] * NSA_H, axis=0)
        s = jnp.where((d >= 0) & (mk > 0.5), s - slope_r * d.astype(F32), NEG_INF)
        m_old = m_ref[...]
        m_new = jnp.maximum(m_old, jnp.max(s, axis=1, keepdims=True))
        m_use = jnp.where(m_new > NEG_INF, m_new, 0.0)
        alpha = jnp.exp(m_old - m_use)
        p = jnp.exp(s - m_use)
        l_ref[...] = alpha * l_ref[...] + jnp.sum(p, axis=1, keepdims=True)
        acc_ref[...] = alpha * acc_ref[...] + _dot(p.astype(BF), vs_ref[pl.ds(k0, kt), :].astype(BF))
        m_ref[...] = m_new
        return carry

    lax.fori_loop(0, (q0 + n_q + kt - 1) // kt, sel_tile, 0)
    l_s = l_ref[...]
    o_s = acc_ref[...] / jnp.where(l_s > 0, l_s, 1.0)

    nw = WINDOW + n_q
    w0 = pl.multiple_of(jnp.maximum(q0 - WINDOW, 0), n_q)
    s_w = _dot_nt(q4, kw_ref[pl.ds(w0, nw), :].astype(BF))
    d_w = qpos_r - (w0 + lax.broadcasted_iota(jnp.int32, (1, nw), 1))
    s_w = jnp.where((d_w >= 0) & (d_w < WINDOW), s_w - slope_r * d_w.astype(F32), NEG_INF)
    o_w = _softmax_segments([(s_w, vw_ref[pl.ds(w0, nw), :].astype(BF))])

    _gated_heads(o_ref, _sigmoid(misc_ref[...]), o_c, o_s, o_w, n_q)


def nsa_prompt(proj, n_b, t_len, k_cmp, v_cmp, expand):
    n_qb = t_len // Q_BLOCK
    n_cmp = t_len // CMP_BLOCK
    n_sel = t_len // SEL_BLOCK
    seq = lambda cb: pl.BlockSpec((t_len, LANES), lambda b, i: (b, cb))
    return pl.pallas_call(
        functools.partial(_nsa_prompt_kernel, t_len=t_len),
        grid=(n_b, n_qb),
        in_specs=[pl.BlockSpec((Q_BLOCK, GROUP_W), lambda b, i: (b * n_qb + i, CB_QC // 4)),
                  pl.BlockSpec((Q_BLOCK, LANES), lambda b, i: (b * n_qb + i, CB_MISC)),
                  pl.BlockSpec((n_cmp, NSA_DH), lambda b, i: (b, 0)),
                  pl.BlockSpec((n_cmp, NSA_DH), lambda b, i: (b, 0)),
                  seq(CB_SELK), seq(CB_SELV), seq(CB_WINK), seq(CB_WINV),
                  pl.BlockSpec((n_sel, t_len), lambda b, i: (0, 0))],
        out_specs=pl.BlockSpec((Q_BLOCK, GROUP_W), lambda b, i: (b * n_qb + i, 0)),
        out_shape=jax.ShapeDtypeStruct((n_b * t_len, GROUP_W), BF),
        scratch_shapes=[pltpu.VMEM((n_cmp, Q_BLOCK), F32),
                        pltpu.VMEM((Q_BLOCK, t_len), F32),
                        pltpu.VMEM((NSA_H * Q_BLOCK, 1), F32),
                        pltpu.VMEM((NSA_H * Q_BLOCK, 1), F32),
                        pltpu.VMEM((NSA_H * Q_BLOCK, NSA_DH), F32)],
        compiler_params=_cparams(("parallel", "arbitrary")),
        name="nsa_prompt",
    )(proj, proj, k_cmp, v_cmp, proj, proj, proj, proj, expand)


def _nsa_sample_kernel(pt_ref, q_ref, misc_ref, new4_ref, new2_ref, winc_ref, pair_ref, e_ref,
                       *rest, n_pages, page, t_q):
    pages = rest[:n_pages]
    kvcs = rest[n_pages:2 * n_pages]
    o_ref, kvc_ref, ks_ref, vs_ref = rest[2 * n_pages:]
    past_len = n_pages * page
    cpp = page // CMP_BLOCK
    n_cmp = past_len // CMP_BLOCK
    n_sel = past_len // SEL_BLOCK + 1
    n_r = NSA_H * t_q
    for j in range(n_pages):
        kvc_ref[j * cpp:(j + 1) * cpp, :] = kvcs[j][...]
        ks_ref[j * page:(j + 1) * page, :] = pages[j][:, 2, :].astype(BF)
        vs_ref[j * page:(j + 1) * page, :] = pages[j][:, 3, :].astype(BF)
    kc_ref = kvc_ref.at[:, 0:NSA_DH]
    vc_ref = kvc_ref.at[:, NSA_DH:2 * NSA_DH]
    ksn_ref = new4_ref.at[:, 2 * NSA_DH:3 * NSA_DH]
    vsn_ref = new4_ref.at[:, 3 * NSA_DH:4 * NSA_DH]
    kwn_ref = new2_ref.at[:, 0:NSA_DH]
    vwn_ref = new2_ref.at[:, NSA_DH:2 * NSA_DH]

    q4 = _stack_heads(q_ref[...] * (NSA_DH ** -0.5)).astype(BF)
    slope_r = _head_rows_const(t_q, _alibi_slope)
    qpos_r = past_len + lax.rem(lax.broadcasted_iota(jnp.int32, (n_r, 1), 0), t_q)

    cmp_end = lax.broadcasted_iota(jnp.int32, (1, n_cmp), 1) * CMP_BLOCK + (CMP_BLOCK - 1)
    s_c = _dot_nt(q4, kc_ref[...].astype(BF)) - slope_r * (qpos_r - cmp_end).astype(F32)
    s_c = jnp.where(cmp_end <= qpos_r, s_c, NEG_INF)
    m_c = jnp.max(s_c, axis=1, keepdims=True)
    m_c = jnp.where(m_c > NEG_INF, m_c, 0.0)
    e_c = jnp.exp(s_c - m_c)
    den_c = jnp.sum(e_c, axis=1, keepdims=True)
    p_c = e_c / jnp.where(den_c > 0, den_c, 1.0)
    o_c = _dot(p_c.astype(BF), vc_ref[...].astype(BF))

    imp = p_c[0:t_q, :]
    for h in range(1, NSA_H):
        imp = imp + p_c[h * t_q:(h + 1) * t_q, :]
    hi = imp.astype(BF)
    lo = (imp - hi.astype(F32)).astype(BF)
    imp = _dot(hi, pair_ref[...]) + _dot(lo, pair_ref[...])
    blk = lax.broadcasted_iota(jnp.int32, (1, LANES), 1)
    cur = (past_len + lax.broadcasted_iota(jnp.int32, (t_q, 1), 0)) // SEL_BLOCK
    forced = (blk == 0) | (blk == cur) | (blk == cur - 1)
    score = jnp.where(blk <= cur, jnp.where(forced, jnp.inf, imp), NEG_INF)
    cnt = jnp.zeros(score.shape, F32)
    for i in range(n_sel):
        col = score[:, i:i + 1]
        beats = (col > score) | ((col == score) & (blk > i))
        cnt = cnt + jnp.where(beats, 1.0, 0.0)
    sel = jnp.where((cnt < float(SEL_TOP)) & (score > NEG_INF), 1.0, 0.0)
    mask = _dot(sel.astype(BF), e_ref[...])
    mask = jnp.concatenate([mask] * NSA_H, axis=0)

    d_p = qpos_r - lax.broadcasted_iota(jnp.int32, (1, past_len), 1)
    s_p = jnp.where(mask[:, 0:past_len] > 0.5, _dot_nt(q4, ks_ref[...]) - slope_r * d_p.astype(F32), NEG_INF)
    d_n = qpos_r - (past_len + lax.broadcasted_iota(jnp.int32, (1, t_q), 1))
    s_n = _dot_nt(q4, ksn_ref[...].astype(BF)) - slope_r * d_n.astype(F32)
    s_n = jnp.where((d_n >= 0) & (mask[:, past_len:past_len + t_q] > 0.5), s_n, NEG_INF)
    o_s = _softmax_segments([(s_p, vs_ref[...]), (s_n, vsn_ref[...].astype(BF))])

    n_wc = winc_ref.shape[0]
    d_c = qpos_r - (past_len - n_wc + lax.broadcasted_iota(jnp.int32, (1, n_wc), 1))
    s_wc = _dot_nt(q4, winc_ref[:, 0, :].astype(BF)) - slope_r * d_c.astype(F32)
    s_wc = jnp.where(d_c < WINDOW, s_wc, NEG_INF)
    s_wn = _dot_nt(q4, kwn_ref[...].astype(BF)) - slope_r * d_n.astype(F32)
    s_wn = jnp.where(d_n >= 0, s_wn, NEG_INF)
    o_w = _softmax_segments([(s_wc, winc_ref[:, 1, :].astype(BF)), (s_wn, vwn_ref[...].astype(BF))])

    _gated_heads(o_ref, _sigmoid(misc_ref[...]), o_c, o_s, o_w, t_q)


def nsa_sample(proj, row0, n_b, t_q, cache_nsa, layer, page_table, kvc_pool, cache_win, pair, expand):
    n_pages = page_table.shape[1]
    page = cache_nsa.shape[2]
    cpp = page // CMP_BLOCK
    past_len = n_pages * page
    n_wc = cache_win.shape[2]
    base = row0 // t_q
    assert CB_SELV == CB_CMPK + 3 and CB_CMPK % 4 == 0 and CB_WINV == CB_WINK + 1 and CB_WINK % 2 == 0

    def page_spec(j):
        return pl.BlockSpec((None, None, page, 4, NSA_DH), lambda b, pt: (layer, pt[b, j], 0, 0, 0))

    def cmp_spec(j):
        return pl.BlockSpec((None, cpp, 2 * NSA_DH), lambda b, pt: (pt[b, j], 0, 0))

    in_specs = [pl.BlockSpec((t_q, GROUP_W), lambda b, pt: (base + b, CB_QC // 4)),
                pl.BlockSpec((t_q, LANES), lambda b, pt: (base + b, CB_MISC)),
                pl.BlockSpec((t_q, 4 * NSA_DH), lambda b, pt: (base + b, CB_CMPK // 4)),
                pl.BlockSpec((t_q, 2 * NSA_DH), lambda b, pt: (base + b, CB_WINK // 2)),
                pl.BlockSpec((None, None, n_wc, 2, NSA_DH), lambda b, pt: (layer, b, 0, 0, 0)),
                pl.BlockSpec(pair.shape, lambda b, pt: (0, 0)),
                pl.BlockSpec(expand.shape, lambda b, pt: (0, 0))]
    in_specs += [page_spec(j) for j in range(n_pages)]
    in_specs += [cmp_spec(j) for j in range(n_pages)]
    return pl.pallas_call(
        functools.partial(_nsa_sample_kernel, n_pages=n_pages, page=page, t_q=t_q),
        grid_spec=pltpu.PrefetchScalarGridSpec(
            num_scalar_prefetch=1,
            grid=(n_b,),
            in_specs=in_specs,
            out_specs=pl.BlockSpec((t_q, GROUP_W), lambda b, pt: (b, 0)),
            scratch_shapes=[pltpu.VMEM((past_len // CMP_BLOCK, 2 * NSA_DH), F32),
                            pltpu.VMEM((past_len, NSA_DH), BF),
                            pltpu.VMEM((past_len, NSA_DH), BF)]),
        out_shape=jax.ShapeDtypeStruct((n_b * t_q, GROUP_W), BF),
        compiler_params=_cparams(("arbitrary",)),
        name="nsa_sample",
    )(page_table, proj, proj, proj, proj, cache_win, pair, expand,
      *([cache_nsa] * n_pages), *([kvc_pool] * n_pages))


def _log_sigmoid(z):
    return jnp.minimum(z, 0.0) - jnp.log(1.0 + jnp.exp(-jnp.abs(z)))


def _cumsum_rows(x, tri):
    hi = x.astype(BF)
    lo = (x - hi.astype(F32)).astype(BF)
    return _dot(tri, hi) + _dot(tri, lo)


def _rows_to_col(row):
    n = row.shape[1]
    eye = lax.broadcasted_iota(jnp.int32, (n, n), 0) == lax.broadcasted_iota(jnp.int32, (n, n), 1)
    return jnp.sum(jnp.where(eye, row, 0.0), axis=1, keepdims=True)


def _gla_out(o, god, ng):
    o = o * lax.rsqrt(jnp.mean(o * o, -1, keepdims=True) + LN_EPS) * ng
    return o * (god * _sigmoid(god))


def _gla_prompt_kernel(q_ref, k_ref, v_ref, god_ref, misc_ref, wa_ref, ba_ref, ng_ref, o_ref, s_ref, st_ref,
                       *, n_chunks):
    c = GLA_CHUNK
    st_ref[...] = jnp.zeros(st_ref.shape, F32)
    r_i = lax.broadcasted_iota(jnp.int32, (c, c), 0)
    c_i = lax.broadcasted_iota(jnp.int32, (c, c), 1)
    causal = r_i >= c_i
    tri = jnp.where(causal, 1.0, 0.0).astype(BF)

    def chunk(n, carry):
        rows = pl.ds(pl.multiple_of(n * c, c), c)
        misc = misc_ref[rows, :].astype(BF)
        for h in range(GLA_H):
            cs = slice(h * LANES, (h + 1) * LANES)
            z = _dot(misc, wa_ref[:, cs]) + ba_ref[:, cs]
            bc = _cumsum_rows(_log_sigmoid(z) / GLA_TAU, tri)
            bl = bc[c - 1:c, :]
            q = q_ref[rows, cs] * (GLA_DK ** -0.5)
            k = k_ref[rows, cs]
            v = v_ref[rows, cs].astype(BF)
            qe = (q * jnp.exp(bc)).astype(BF)
            att = jnp.where(causal, _dot_nt(qe, (k * jnp.exp(-bc)).astype(BF)), 0.0)
            st = st_ref[h]
            o = _dot(qe, st.astype(BF)) + _dot(att.astype(BF), v)
            st_ref[h] = _rows_to_col(jnp.exp(bl)) * st + _dot_tn((k * jnp.exp(bl - bc)).astype(BF), v)
            o_ref[rows, cs] = _gla_out(o, god_ref[rows, cs], ng_ref[...]).astype(o_ref.dtype)
        return carry

    lax.fori_loop(0, n_chunks, chunk, 0)
    for h in range(GLA_H):
        s_ref[h] = st_ref[h, 0:GLA_DK, :]


def gla_prompt(proj, n_b, t_len, wa_pad, ba_pad, norm_g):
    w = GLA_H * LANES
    seq = lambda cb0: pl.BlockSpec((t_len, w), lambda b: (b, cb0 // GLA_H))
    return pl.pallas_call(
        functools.partial(_gla_prompt_kernel, n_chunks=t_len // GLA_CHUNK),
        grid=(n_b,),
        in_specs=[seq(CB_QD), seq(CB_KD), seq(CB_VD), seq(CB_GOD),
                  pl.BlockSpec((t_len, LANES), lambda b: (b, CB_MISC)),
                  pl.BlockSpec((LANES, w), lambda b: (0, 0)),
                  pl.BlockSpec((1, w), lambda b: (0, 0)),
                  pl.BlockSpec((1, GLA_DV), lambda b: (0, 0))],
        out_specs=[pl.BlockSpec((t_len, w), lambda b: (b, 0)),
                   pl.BlockSpec((None, GLA_H, GLA_DK, GLA_DV), lambda b: (b, 0, 0, 0))],
        out_shape=[jax.ShapeDtypeStruct((n_b * t_len, GROUP_W), BF),
                   jax.ShapeDtypeStruct((n_b, GLA_H, GLA_DK, GLA_DV), F32)],
        scratch_shapes=[pltpu.VMEM((GLA_H, LANES, GLA_DV), F32)],
        compiler_params=_cparams(("parallel",)),
        name="gla_prompt",
    )(proj, proj, proj, proj, proj, wa_pad, ba_pad, norm_g.reshape(1, GLA_DV))


def _gla_sample_kernel(q_ref, k_ref, v_ref, god_ref, misc_ref, wa_ref, ba_ref, ng_ref, s0_ref, o_ref, s_ref,
                       *, bb, t_q):
    n_r = bb * t_q
    r_i = lax.broadcasted_iota(jnp.int32, (n_r, n_r), 0)
    c_i = lax.broadcasted_iota(jnp.int32, (n_r, n_r), 1)
    causal = (r_i >= c_i) & (r_i // t_q == c_i // t_q)
    tri = jnp.where(causal, 1.0, 0.0).astype(BF)
    row_b = lax.broadcasted_iota(jnp.int32, (n_r, 1), 0) // t_q

    z = _dot(misc_ref[...].astype(BF), wa_ref[...]) + ba_ref[...]
    bc = _cumsum_rows(_log_sigmoid(z) / GLA_TAU, tri)
    q = q_ref[...] * (GLA_DK ** -0.5)
    k = k_ref[...]
    v = v_ref[...].astype(BF)
    qe = q * jnp.exp(bc)
    att = jnp.where(causal, _dot_nt(qe.astype(BF), (k * jnp.exp(-bc)).astype(BF)), 0.0)
    o = _dot(att.astype(BF), v)
    for b in range(bb):
        mine = row_b == b
        bl = bc[(b + 1) * t_q - 1:(b + 1) * t_q, :]
        s0 = s0_ref[b, 0]
        o = o + _dot(jnp.where(mine, qe, 0.0)[:, 0:GLA_DK].astype(BF), s0.astype(BF))
        k2 = jnp.where(mine, k * jnp.exp(bl - bc), 0.0).astype(BF)
        upd = _dot_tn(k2, v)
        s_ref[b, 0] = _rows_to_col(jnp.exp(bl))[0:GLA_DK, :] * s0 + upd[0:GLA_DK, :]
    o_ref[...] = _gla_out(o, god_ref[...], ng_ref[...]).astype(o_ref.dtype)


def gla_sample(proj, row0, n_b, t_q, wa_pad, ba_pad, norm_g, s0, s0_b0, bb):
    rb = bb * t_q
    base = row0 // rb
    s_i0 = s0_b0 // bb
    seq = lambda cb0: pl.BlockSpec((rb, LANES), lambda i, h: (base + i, cb0 + h))
    return pl.pallas_call(
        functools.partial(_gla_sample_kernel, bb=bb, t_q=t_q),
        grid=(n_b // bb, GLA_H),
        in_specs=[seq(CB_QD), seq(CB_KD), seq(CB_VD), seq(CB_GOD),
                  pl.BlockSpec((rb, LANES), lambda i, h: (base + i, CB_MISC)),
                  pl.BlockSpec((LANES, LANES), lambda i, h: (0, h)),
                  pl.BlockSpec((1, LANES), lambda i, h: (0, h)),
                  pl.BlockSpec((1, GLA_DV), lambda i, h: (0, 0)),
                  pl.BlockSpec((bb, 1, GLA_DK, GLA_DV), lambda i, h: (s_i0 + i, h, 0, 0))],
        out_specs=[pl.BlockSpec((rb, GLA_DV), lambda i, h: (i, h)),
                   pl.BlockSpec((bb, 1, GLA_DK, GLA_DV), lambda i, h: (i, h, 0, 0))],
        out_shape=[jax.ShapeDtypeStruct((n_b * t_q, GROUP_W), BF),
                   jax.ShapeDtypeStruct((n_b, GLA_H, GLA_DK, GLA_DV), F32)],
        compiler_params=_cparams(("parallel", "parallel")),
        name="gla_sample",
    )(proj, proj, proj, proj, proj, wa_pad, ba_pad, norm_g.reshape(1, GLA_DV), s0)


def _mem_attn_kernel(q_ref, kv_ref, o_ref):
    q = q_ref[...] * (MEM_DH ** -0.5)
    for h in range(MEM_H):
        cs = slice(h * MEM_DH, (h + 1) * MEM_DH)
        s = _dot_nt(q[:, cs].astype(BF), kv_ref[:, 0, h, :].astype(BF))
        e = jnp.exp(s - jnp.max(s, axis=1, keepdims=True))
        p = e / jnp.sum(e, axis=1, keepdims=True)
        o_ref[:, cs] = _dot(p.astype(BF), kv_ref[:, 1, h, :].astype(BF)).astype(o_ref.dtype)


def mem_attention(qm, row0, n_b, t_len, kv, layer, tq):
    n_t = t_len // tq
    base = row0 // tq
    w = MEM_H * MEM_DH
    return pl.pallas_call(
        _mem_attn_kernel,
        grid=(n_b, n_t),
        in_specs=[pl.BlockSpec((tq, w), lambda b, i: (base + b * n_t + i, 0)),
                  pl.BlockSpec((None, None, MEM_LEN, 2, MEM_H, MEM_DH), lambda b, i: (layer, b, 0, 0, 0, 0))],
        out_specs=pl.BlockSpec((tq, w), lambda b, i: (b * n_t + i, 0)),
        out_shape=jax.ShapeDtypeStruct((n_b * t_len, w), BF),
        compiler_params=_cparams(("parallel", "parallel")),
        name="mem_attention",
    )(qm, kv)


def _router_kernel(x_ref, w_ref, b_ref, tri_ref, idx_ref, gate_ref, rank_ref, cnt_ref, run_ref):
    @pl.when(pl.program_id(0) == 0)
    def _():
        run_ref[...] = jnp.zeros(run_ref.shape, F32)

    logits = jnp.dot(x_ref[...], w_ref[...], precision=lax.Precision.HIGHEST,
                     preferred_element_type=F32) + b_ref[...]
    lane = lax.broadcasted_iota(jnp.int32, logits.shape, 1).astype(F32)
    vals = jnp.where(lane < N_EXPERTS, logits, NEG_INF)
    idx_out = jnp.zeros(logits.shape, F32)
    e_out = jnp.zeros(logits.shape, F32)
    den = None
    top0 = None
    onehots = []
    for k in range(TOP_K):
        m = jnp.max(vals, axis=1, keepdims=True)
        idx = jnp.min(jnp.where(vals == m, lane, float(LANES)), axis=1, keepdims=True)
        if k == 0:
            top0 = m
        e = jnp.exp(m - top0)
        den = e if den is None else den + e
        idx_out = jnp.where(lane == k, idx, idx_out)
        e_out = jnp.where(lane == k, e, e_out)
        onehots.append(jnp.where(lane == idx, 1.0, 0.0))
        vals = jnp.where(lane == idx, NEG_INF, vals)
    idx_ref[...] = idx_out.astype(jnp.int32)
    gate_ref[...] = e_out / den

    chosen = onehots[0] + onehots[1] + onehots[2] + onehots[3]
    before = _dot(tri_ref[...], chosen.astype(BF)) + run_ref[...]
    rank_out = jnp.zeros(logits.shape, F32)
    for k in range(TOP_K):
        r = jnp.sum(before * onehots[k], axis=1, keepdims=True)
        rank_out = jnp.where(lane == k, r, rank_out)
    rank_ref[...] = rank_out.astype(jnp.int32)
    run_ref[...] = run_ref[...] + jnp.sum(chosen, axis=0, keepdims=True)
    cnt_ref[...] = run_ref[...]


def moe_router(x, w_pad, b_pad, tm):
    m, d = x.shape
    assert m % tm == 0, (m, tm)
    tri = jnp.asarray(np.tril(np.ones((tm, tm), np.float32), -1), BF)
    row = lambda dt: jax.ShapeDtypeStruct((m, LANES), dt)
    return pl.pallas_call(
        _router_kernel,
        grid=(m // tm,),
        in_specs=[pl.BlockSpec((tm, d), lambda i: (i, 0)),
                  pl.BlockSpec((d, LANES), lambda i: (0, 0)),
                  pl.BlockSpec((1, LANES), lambda i: (0, 0)),
                  pl.BlockSpec((tm, tm), lambda i: (0, 0))],
        out_specs=[pl.BlockSpec((tm, LANES), lambda i: (i, 0)),
                   pl.BlockSpec((tm, LANES), lambda i: (i, 0)),
                   pl.BlockSpec((tm, LANES), lambda i: (i, 0)),
                   pl.BlockSpec((1, LANES), lambda i: (0, 0))],
        out_shape=[row(jnp.int32), row(F32), row(jnp.int32), jax.ShapeDtypeStruct((1, LANES), F32)],
        scratch_shapes=[pltpu.VMEM((1, LANES), F32)],
        compiler_params=_cparams(("arbitrary",)),
        name="moe_router",
    )(x, w_pad, b_pad, tri)


def _slot_ids_to_smem(slot_ref, ids_ref, sem):
    cp = pltpu.make_async_copy(slot_ref.at[0], ids_ref, sem)
    cp.start()
    cp.wait()


def _dispatch_kernel(pad0_ref, npad_ref, slot_ref, x_ref, xg_ref, ids_ref, zero_ref, sem_ids, sem, sem_z, *, tn):
    @pl.when(pl.program_id(0) == 0)
    def _():
        zero_ref[...] = jnp.zeros(zero_ref.shape, zero_ref.dtype)

        def per_expert(e, carry):
            zr = zero_ref.shape[0]
            row0 = pad0_ref[e]
            n_one = jnp.minimum((-row0) & (zr - 1), npad_ref[e])
            row_al = row0 + n_one
            n_full = (npad_ref[e] - n_one) // zr

            def start_full(r, c):
                dst = xg_ref.at[pl.ds(pl.multiple_of(row_al + r * zr, zr), zr), :]
                pltpu.make_async_copy(zero_ref, dst, sem_z).start()
                return c

            def wait_full(r, c):
                pltpu.make_async_copy(zero_ref, xg_ref.at[pl.ds(0, zr), :], sem_z).wait()
                return c

            def start_one(r, c):
                pltpu.make_async_copy(zero_ref.at[pl.ds(0, 1), :], xg_ref.at[pl.ds(row0 + r, 1), :], sem_z).start()
                return c

            def wait_one(r, c):
                pltpu.make_async_copy(zero_ref.at[pl.ds(0, 1), :], xg_ref.at[pl.ds(0, 1), :], sem_z).wait()
                return c

            lax.fori_loop(0, n_full, start_full, 0)
            lax.fori_loop(0, n_one, start_one, 0)
            lax.fori_loop(0, n_full, wait_full, 0)
            lax.fori_loop(0, n_one, wait_one, 0)
            return carry

        lax.fori_loop(0, pad0_ref.shape[0], per_expert, 0)

    _slot_ids_to_smem(slot_ref, ids_ref, sem_ids)

    def issue(t, carry):
        for k in range(TOP_K):
            pltpu.make_async_copy(x_ref.at[pl.ds(t, 1), :],
                                  xg_ref.at[pl.ds(ids_ref[0, t * TOP_K + k], 1), :], sem).start()
        return carry

    lax.fori_loop(0, tn, issue, 0, unroll=8)
    for k in range(TOP_K):
        pltpu.make_async_copy(x_ref, xg_ref.at[pl.ds(0, tn), :], sem).wait()


def moe_dispatch(x, slots, pad0, npad, cap):
    n, d = x.shape
    tn = MOE_TOK
    assert n % tn == 0, (n, tn)
    assert MOE_ZERO_ROWS & (MOE_ZERO_ROWS - 1) == 0 and MOE_TM % MOE_ZERO_ROWS == 0 and MOE_ZERO_ROWS % 8 == 0
    return pl.pallas_call(
        functools.partial(_dispatch_kernel, tn=tn),
        grid_spec=pltpu.PrefetchScalarGridSpec(
            num_scalar_prefetch=2,
            grid=(n // tn,),
            in_specs=[pl.BlockSpec((1, 1, TOP_K * tn), lambda i, p0, np_: (i, 0, 0)),
                      pl.BlockSpec((tn, d), lambda i, p0, np_: (i, 0))],
            out_specs=pl.BlockSpec(memory_space=pl.ANY),
            scratch_shapes=[pltpu.SMEM((1, TOP_K * tn), jnp.int32),
                            pltpu.VMEM((MOE_ZERO_ROWS, d), x.dtype),
                            pltpu.SemaphoreType.DMA(()),
                            pltpu.SemaphoreType.DMA(()),
                            pltpu.SemaphoreType.DMA(())]),
        out_shape=jax.ShapeDtypeStruct((cap, d), x.dtype),
        compiler_params=_cparams(("arbitrary",)),
        name="moe_dispatch",
    )(pad0, npad, slots.reshape(n // tn, 1, TOP_K * tn), x)


def _moe_up_kernel(be_ref, nu_ref, x_ref, wg_ref, wu_ref, bg_ref, bu_ref, o_ref, wgb_ref, wub_ref):
    i = pl.program_id(1)
    changed = jnp.logical_or(i == 0, be_ref[i] != be_ref[jnp.maximum(i - 1, 0)])

    @pl.when(changed)
    def _():
        wgb_ref[...] = wg_ref[...].astype(BF)
        wub_ref[...] = wu_ref[...].astype(BF)

    @pl.when(i < nu_ref[0])
    def _():
        x = x_ref[...].astype(BF)
        g = jnp.minimum(_dot(x, wgb_ref[...]) + bg_ref[...], SWIGLU_LIMIT)
        u = jnp.clip(_dot(x, wub_ref[...]) + bu_ref[...], -SWIGLU_LIMIT, SWIGLU_LIMIT)
        o_ref[...] = ((u + 1.0) * g * _sigmoid(SWIGLU_ALPHA * g)).astype(o_ref.dtype)

    @pl.when(i >= nu_ref[0])
    def _():
        o_ref[...] = jnp.zeros(o_ref.shape, o_ref.dtype)


def moe_up(xg, blk_expert, n_used, w1, b1):
    cap, d = xg.shape
    n_blk = cap // MOE_TM
    n_j = D_FF // MOE_TN
    return pl.pallas_call(
        _moe_up_kernel,
        grid_spec=pltpu.PrefetchScalarGridSpec(
            num_scalar_prefetch=2,
            grid=(n_j, n_blk),
            in_specs=[pl.BlockSpec((MOE_TM, d), lambda j, i, be, nu: (jnp.minimum(i, nu[0] - 1), 0)),
                      pl.BlockSpec((None, d, MOE_TN), lambda j, i, be, nu: (be[i], 0, j)),
                      pl.BlockSpec((None, d, MOE_TN), lambda j, i, be, nu: (be[i], 0, n_j + j)),
                      pl.BlockSpec((None, 1, MOE_TN), lambda j, i, be, nu: (be[i], 0, j)),
                      pl.BlockSpec((None, 1, MOE_TN), lambda j, i, be, nu: (be[i], 0, n_j + j))],
            out_specs=pl.BlockSpec((MOE_TM, MOE_TN), lambda j, i, be, nu: (i, j)),
            scratch_shapes=[pltpu.VMEM((d, MOE_TN), BF), pltpu.VMEM((d, MOE_TN), BF)]),
        out_shape=jax.ShapeDtypeStruct((cap, D_FF), BF),
        compiler_params=_cparams(("arbitrary", "arbitrary")),
        name="moe_up",
    )(blk_expert, n_used, xg, w1, w1, b1, b1)


def _moe_down_kernel(be_ref, nu_ref, a_ref, w_ref, b_ref, o_ref, wb_ref):
    i = pl.program_id(1)
    changed = jnp.logical_or(i == 0, be_ref[i] != be_ref[jnp.maximum(i - 1, 0)])

    @pl.when(changed)
    def _():
        wb_ref[...] = w_ref[...].astype(BF)

    @pl.when(i < nu_ref[0])
    def _():
        o_ref[...] = _dot(a_ref[...], wb_ref[...]) + b_ref[...]

    @pl.when(i >= nu_ref[0])
    def _():
        o_ref[...] = jnp.zeros(o_ref.shape, o_ref.dtype)


def moe_down(act, blk_expert, n_used, w2, b2):
    cap, f = act.shape
    d = w2.shape[2]
    n_blk = cap // MOE_TM
    return pl.pallas_call(
        _moe_down_kernel,
        grid_spec=pltpu.PrefetchScalarGridSpec(
            num_scalar_prefetch=2,
            grid=(d // MOE_TN_DOWN, n_blk),
            in_specs=[pl.BlockSpec((MOE_TM, f), lambda j, i, be, nu: (jnp.minimum(i, nu[0] - 1), 0)),
                      pl.BlockSpec((None, f, MOE_TN_DOWN), lambda j, i, be, nu: (be[i], 0, j)),
                      pl.BlockSpec((None, 1, MOE_TN_DOWN), lambda j, i, be, nu: (be[i], 0, j))],
            out_specs=pl.BlockSpec((MOE_TM, MOE_TN_DOWN), lambda j, i, be, nu: (i, j)),
            scratch_shapes=[pltpu.VMEM((f, MOE_TN_DOWN), BF)]),
        out_shape=jax.ShapeDtypeStruct((cap, d), F32),
        compiler_params=_cparams(("arbitrary", "arbitrary")),
        name="moe_down",
    )(blk_expert, n_used, act, w2, b2)


def _combine_ln_kernel(slot_ref, yb_hbm, gate_ref, x_ref, g_ref, b_ref, o_ref, ids_ref, buf_ref, sem_ids, sem, *, tn):
    _slot_ids_to_smem(slot_ref, ids_ref, sem_ids)

    def issue(t, carry):
        for k in range(TOP_K):
            pltpu.make_async_copy(yb_hbm.at[pl.ds(ids_ref[0, t * TOP_K + k], 1), :],
                                  buf_ref.at[k, pl.ds(t, 1), :], sem).start()
        return carry

    lax.fori_loop(0, tn, issue, 0, unroll=8)
    for k in range(TOP_K):
        pltpu.make_async_copy(yb_hbm.at[pl.ds(0, tn), :], buf_ref.at[k], sem).wait()
    gate = gate_ref[...]
    y = gate[:, 0:1] * buf_ref[0]
    for k in range(1, TOP_K):
        y = y + gate[:, k:k + 1] * buf_ref[k]
    o_ref[...] = _layer_norm(DN_ALPHA * x_ref[...] + y, g_ref[...], b_ref[...])


def moe_combine_ln(yb, slots, gate, x, g, b):
    n, d = x.shape
    tn = MOE_TOK
    assert n % tn == 0, (n, tn)
    return pl.pallas_call(
        functools.partial(_combine_ln_kernel, tn=tn),
        grid=(n // tn,),
        in_specs=[pl.BlockSpec((1, 1, TOP_K * tn), lambda i: (i, 0, 0)),
                  pl.BlockSpec(memory_space=pl.ANY),
                  pl.BlockSpec((tn, LANES), lambda i: (i, 0)),
                  pl.BlockSpec((tn, d), lambda i: (i, 0)),
                  pl.BlockSpec((1, d), lambda i: (0, 0)),
                  pl.BlockSpec((1, d), lambda i: (0, 0))],
        out_specs=pl.BlockSpec((tn, d), lambda i: (i, 0)),
        out_shape=jax.ShapeDtypeStruct((n, d), F32),
        scratch_shapes=[pltpu.SMEM((1, TOP_K * tn), jnp.int32),
                        pltpu.VMEM((TOP_K, tn, d), F32),
                        pltpu.SemaphoreType.DMA(()),
                        pltpu.SemaphoreType.DMA(())],
        compiler_params=_cparams(("arbitrary",)),
        name="moe_combine_ln",
    )(slots.reshape(n // tn, 1, TOP_K * tn), yb, gate, x, g.reshape(1, d), b.reshape(1, d))


def moe_ffn_ln(x, router_w, router_b, w1, b1, w2, b2, e0, ln_g, ln_b):
    n, d = x.shape
    w_pad = jnp.pad(router_w, ((0, 0), (0, LANES - N_EXPERTS)))
    b_pad = jnp.pad(router_b, (0, LANES - N_EXPERTS)).reshape(1, LANES)
    idx, gate, rank, cnt = moe_router(x, w_pad, b_pad, 512)

    counts = cnt[0, :N_EXPERTS].astype(jnp.int32)
    padded = (counts + MOE_TM - 1) // MOE_TM * MOE_TM
    pad_end = jnp.cumsum(padded)
    pad_start = pad_end - padded
    onehot = idx[:, :TOP_K, None] == jnp.arange(N_EXPERTS, dtype=jnp.int32)
    slots = (jnp.sum(jnp.where(onehot, pad_start, 0), axis=-1) + rank[:, :TOP_K]).reshape(-1)
    n_blk = -(-(n * TOP_K + N_EXPERTS * (MOE_TM - 1)) // MOE_TM)
    blk_row0 = jnp.arange(n_blk, dtype=jnp.int32) * MOE_TM
    blk_expert = e0 + jnp.minimum(jnp.sum(pad_end[None, :] <= blk_row0[:, None], axis=1), N_EXPERTS - 1)
    n_used = (pad_end[-1] // MOE_TM).astype(jnp.int32).reshape(1)

    cap = n_blk * MOE_TM
    pad0 = jnp.concatenate([pad_start + counts, pad_end[-1:]]).astype(jnp.int32)
    npad = jnp.concatenate([padded - counts, cap - pad_end[-1:]]).astype(jnp.int32)
    xg = moe_dispatch(x, slots, pad0, npad, cap)
    act = moe_up(xg, blk_expert.astype(jnp.int32), n_used, w1, b1)
    yb = moe_down(act, blk_expert.astype(jnp.int32), n_used, w2, b2)
    return moe_combine_ln(yb, slots, gate, x, ln_g, ln_b)


def _pad_heads(w):
    lead = w.shape[:-1]
    w = w.reshape(lead + (GLA_H, GLA_DK))
    w = jnp.pad(w, [(0, 0)] * len(lead) + [(0, 0), (0, LANES - GLA_DK)])
    return w.reshape(lead + (GLA_H * LANES,))


def _tail_column_map():
    c = np.cumsum((3 * NSA_H, GLA_H * GLA_DK, GLA_H * GLA_DK, GROUP_W, GLA_RANK, GROUP_W))
    dst = np.zeros(c[-1], np.int64)
    src = np.arange(c[-1])
    rel = lambda cb: (cb - CB_MISC) * LANES
    dst[:c[0]] = rel(CB_MISC) + MISC_GATE0 + src[:c[0]]
    for lo, hi, cb in ((c[0], c[1], CB_QD), (c[1], c[2], CB_KD)):
        k = src[lo:hi] - lo
        dst[lo:hi] = rel(cb) + (k // GLA_DK) * LANES + k % GLA_DK
    dst[c[2]:c[3]] = rel(CB_VD) + src[c[2]:c[3]] - c[2]
    dst[c[3]:c[4]] = rel(CB_MISC) + MISC_LR0 + src[c[3]:c[4]] - c[3]
    dst[c[4]:c[5]] = rel(CB_GOD) + src[c[4]:c[5]] - c[4]
    return dst


def _relayout_w_in(w):
    wb = w.astype(BF)
    n_keep = CB_MISC * LANES
    dst = _tail_column_map()
    sel = np.zeros((dst.shape[0], D_PROJ - n_keep), np.float32)
    sel[np.arange(dst.shape[0]), dst] = 1.0
    tail = jnp.dot(wb[:, n_keep:], jnp.asarray(sel, BF), preferred_element_type=F32).astype(BF)
    return jnp.concatenate([wb[:, :n_keep], tail], axis=1)


def kernel(x_prompt, x_sample, cache_pool, cache_nsa_kv, cache_win_kv, state_gla, cache_mem_kv, page_table,
           mem_prompt, ln_in_g, ln_in_b, w_in, w_out, pool_w, pool_scale, gmlp_ln_g, gmlp_ln_b, gmlp_ws, gmlp_bs,
           nsa_phi, gla_wa, gla_ba, gla_norm_g, ln1_g, ln1_b, mem_wq, mem_wkv, mem_wo, ln2_g, ln2_b,
           router_w, router_b, moe_w1, moe_b1, moe_w2, moe_b2, ln3_g, ln3_b):
    n_bp, t_p, d = x_prompt.shape
    n_bs, t_s, _ = x_sample.shape
    n_p = n_bp * t_p
    n_s = n_bs * t_s
    n_pool = cache_nsa_kv.shape[1]
    page = cache_nsa_kv.shape[2]
    past_len = page_table.shape[1] * page
    n_wc = cache_win_kv.shape[2]
    assert t_p % 512 == 0 and n_s % 512 == 0 and t_s == 8 and past_len % SEL_BLOCK == 0 and n_wc == WINDOW

    x = jnp.concatenate([ln_rows(x_prompt.reshape(n_p, d), ln_in_g, ln_in_b, 512),
                         ln_rows(x_sample.reshape(n_s, d), ln_in_g, ln_in_b, 512)], axis=0)
    n_tok = n_p + n_s
    tm_big = 1024 if n_tok % 1024 == 0 else 512
    cpp = page // CMP_BLOCK
    cache_blocks = cache_nsa_kv.reshape(DEPTH * n_pool * cpp, CMP_BLOCK, 4, NSA_DH)
    state0 = state_gla.reshape(DEPTH * n_bs, GLA_H, GLA_DK, GLA_DV)
    w1_all = moe_w1.reshape(DEPTH * N_EXPERTS, d, 2 * D_FF)
    b1_all = moe_b1.reshape(DEPTH * N_EXPERTS, 1, 2 * D_FF)
    w2_all = moe_w2.reshape(DEPTH * N_EXPERTS, D_FF, d)
    b2_all = moe_b2.reshape(DEPTH * N_EXPERTS, 1, d)

    expand_p = (np.arange(t_p)[None, :] // SEL_BLOCK == np.arange(t_p // SEL_BLOCK)[:, None])
    expand_p = jnp.asarray(expand_p, BF)
    expand_s = (np.arange(past_len + LANES)[None, :] // SEL_BLOCK == np.arange(LANES)[:, None])
    expand_s = jnp.asarray(expand_s, BF)
    pair_s = jnp.asarray(np.arange(past_len // CMP_BLOCK)[:, None] // 2 == np.arange(LANES)[None, :], BF)
    bb_s = LANES // t_s
    eye_bb = jnp.eye(bb_s, dtype=F32)

    outs = {k: [] for k in ("pool_p", "nsa_p", "win_p", "gla_p", "mem_p", "pool_s", "chunk_s", "nsa_s", "win_s",
                            "gla_s")}
    for l in range(DEPTH):
        proj = matmul(x, _relayout_w_in(w_in[l]), tm_big, 1024)

        pre_p = jnp.zeros((n_bp, POOL_PRE, GROUP_W), F32)
        pre_s = jnp.pad(cache_pool[l], ((0, 0), (1, 0), (0, 0)))
        ya = jnp.concatenate([
            pool_mixer(proj, 0, n_bp, t_p, pre_p, pool_w[l], pool_scale[l], 0, 1, 512),
            pool_mixer(proj, n_p, n_bs, t_s, pre_s, pool_w[l], pool_scale[l], past_len, bb_s, t_s)], axis=0)

        tril = jnp.tril(jnp.ones((GMLP_CHUNK, GMLP_CHUNK), F32))
        w_mix_p = (gmlp_ws[l] * tril).astype(BF)
        bias_p = jnp.repeat(gmlp_bs[l].T, LANES, axis=1)
        ws_s = gmlp_ws[l][:, :t_s, :t_s] * tril[:t_s, :t_s]
        w_mix_s = jnp.einsum('ab,gts->gatbs', eye_bb, ws_s).reshape(4, LANES, LANES).astype(BF)
        bias_s = jnp.tile(jnp.repeat(gmlp_bs[l][:, :t_s].T, LANES, axis=1), (bb_s, 1))
        yb_p, _ = gmlp_mixer(proj, 0, n_p, gmlp_ln_g[l], gmlp_ln_b[l], w_mix_p, bias_p, 4)
        yb_s, v_s = gmlp_mixer(proj, n_p, n_s, gmlp_ln_g[l], gmlp_ln_b[l], w_mix_s, bias_s, 4)
        yb = jnp.concatenate([yb_p, yb_s], axis=0)

        phi2 = nsa_phi[l].astype(BF).reshape(2, CMP_BLOCK // 2, 2 * NSA_DH, NSA_DH)
        kc_p, vc_p = cmp_project(proj, n_p // CMP_BLOCK, phi2[0], phi2[1], t_p // CMP_BLOCK)
        yc_p = nsa_prompt(proj, n_bp, t_p, kc_p, vc_p, expand_p)
        kvc_pool = cmp_pool(cache_blocks, l * n_pool * cpp, n_pool * cpp, phi2[0], phi2[1], 128)
        yc_s = nsa_sample(proj, n_p, n_bs, t_s, cache_nsa_kv, l, page_table,
                          kvc_pool.reshape(n_pool, cpp, 2 * NSA_DH), cache_win_kv, pair_s, expand_s)
        yc = jnp.concatenate([yc_p, yc_s], axis=0)

        wa_pad = jnp.zeros((LANES, GLA_H * LANES), F32).at[MISC_LR0:MISC_LR0 + GLA_RANK].set(_pad_heads(gla_wa[l]))
        wa_pad = wa_pad.astype(BF)
        ba_pad = _pad_heads(gla_ba[l]).reshape(1, GLA_H * LANES)
        yd_p, st_p = gla_prompt(proj, n_bp, t_p, wa_pad, ba_pad, gla_norm_g[l])
        yd_s, st_s = gla_sample(proj, n_p, n_bs, t_s, wa_pad, ba_pad, gla_norm_g[l], state0, l * n_bs, bb_s)
        yd = jnp.concatenate([yd_p, yd_s], axis=0)

        x = matmul_res_ln([ya, yb, yc, yd], w_out[l].astype(BF), x, ln1_g[l], ln1_b[l], 256)

        qm = matmul(x, mem_wq[l].astype(BF), tm_big, 512)
        mem_kv_p = matmul(mem_prompt.reshape(n_bp * MEM_LEN, d), mem_wkv[l].astype(BF), 512, 512)
        mem_kv_p = mem_kv_p.reshape(1, n_bp, MEM_LEN, 2, MEM_H, MEM_DH)
        om = jnp.concatenate([
            mem_attention(qm, 0, n_bp, t_p, mem_kv_p, 0, 512),
            mem_attention(qm, n_p, n_bs, t_s, cache_mem_kv, l, t_s)], axis=0)
        x = matmul_res_ln([om], mem_wo[l].astype(BF), x, ln2_g[l], ln2_b[l], 256)

        x = moe_ffn_ln(x, router_w[l], router_b[l], w1_all, b1_all, w2_all, b2_all, l * N_EXPERTS,
                       ln3_g[l], ln3_b[l])

        seqs_p = lambda c0, c1: proj[:n_p, c0 * LANES:c1 * LANES].reshape(n_bp, t_p, (c1 - c0) * LANES)
        seqs_s = lambda c0, c1: proj[n_p:, c0 * LANES:c1 * LANES].reshape(n_bs, t_s, (c1 - c0) * LANES)
        outs["pool_p"].append(seqs_p(CB_XA, CB_U)[:, t_p - POOL_BUF:])
        outs["nsa_p"].append(seqs_p(CB_CMPK, CB_WINK).reshape(n_bp, t_p, 4, NSA_DH))
        outs["win_p"].append(seqs_p(CB_WINK, CB_MISC)[:, t_p - min(WINDOW, t_p):]
                             .reshape(n_bp, min(WINDOW, t_p), 2, NSA_DH))
        outs["gla_p"].append(st_p)
        outs["mem_p"].append(mem_kv_p[0])
        outs["pool_s"].append(seqs_s(CB_XA, CB_U))
        outs["chunk_s"].append(v_s.reshape(n_bs, t_s, GROUP_W))
        outs["nsa_s"].append(seqs_s(CB_CMPK, CB_WINK).reshape(n_bs, t_s, 4, NSA_DH))
        outs["win_s"].append(seqs_s(CB_WINK, CB_MISC).reshape(n_bs, t_s, 2, NSA_DH))
        outs["gla_s"].append(st_s)

    st = lambda k: jnp.stack(outs[k])
    return (x[:n_p].reshape(n_bp, t_p, d), x[n_p:].reshape(n_bs, t_s, d), st("pool_p"), st("nsa_p"), st("win_p"),
            st("gla_p"), st("mem_p"), st("pool_s"), st("chunk_s"), st("nsa_s"), st("win_s"), st("gla_s"))
```

```python
import functools

import jax
import jax.numpy as jnp
import numpy as np
from jax import lax
from jax.experimental import pallas as pl
from jax.experimental.pallas import tpu as pltpu

BF = jnp.bfloat16
F32 = jnp.float32

D_MODEL = 2048
DEPTH = 2
GROUP_W = 512
LANES = 128
POOL_WINDOWS = (2, 4, 8, 16)
POOL_BUF = 15
POOL_PRE = 16
GMLP_CHUNK = 128
NSA_DH = 128
NSA_H = 4
CMP_BLOCK = 32
SEL_BLOCK = 64
SEL_TOP = 16
WINDOW = 512
Q_BLOCK = 128
GLA_H = 4
GLA_DK = 64
GLA_DV = 128
GLA_RANK = 16
GLA_TAU = 16.0
GLA_CHUNK = 64
MEM_LEN = 256
MEM_H = 4
MEM_DH = 128
N_EXPERTS = 32
TOP_K = 4
D_FF = 2048
SWIGLU_LIMIT = 7.0
SWIGLU_ALPHA = 1.702
DN_ALPHA = (2.0 * DEPTH) ** 0.25
LN_EPS = 1e-5
NEG_INF = float("-inf")

CB_XA, CB_U, CB_V, CB_QC = 0, 4, 8, 12
CB_CMPK, CB_CMPV, CB_SELK, CB_SELV, CB_WINK, CB_WINV = 16, 17, 18, 19, 20, 21
CB_MISC, CB_QD, CB_KD, CB_VD, CB_GOD = 22, 24, 28, 32, 36
N_CB = 40
D_PROJ = N_CB * LANES
MISC_GATE0 = 0
MISC_LR0 = 12

MOE_TM = 512
MOE_TN = 512
MOE_TN_DOWN = 1024
MOE_TOK = 512
MOE_ZERO_ROWS = 32
VMEM_LIMIT = 56 * 1024 * 1024


def _cparams(sem):
    return pltpu.CompilerParams(dimension_semantics=sem, vmem_limit_bytes=VMEM_LIMIT)


def _layer_norm(x, g, b):
    mu = jnp.mean(x, -1, keepdims=True)
    xc = x - mu
    var = jnp.mean(xc * xc, -1, keepdims=True)
    return xc * lax.rsqrt(var + LN_EPS) * g + b


def _dot(a, b):
    return jnp.dot(a, b, preferred_element_type=F32)


def _dot_nt(a, b):
    return lax.dot_general(a, b, (((1,), (1,)), ((), ())), preferred_element_type=F32)


def _dot_tn(a, b):
    return lax.dot_general(a, b, (((0,), (0,)), ((), ())), preferred_element_type=F32)


def _sigmoid(x):
    return 1.0 / (1.0 + jnp.exp(-x))


def _ln_kernel(x_ref, g_ref, b_ref, o_ref):
    o_ref[...] = _layer_norm(x_ref[...], g_ref[...], b_ref[...])


def ln_rows(x, g, b, tm):
    m, d = x.shape
    assert m % tm == 0, (m, tm)
    return pl.pallas_call(
        _ln_kernel,
        grid=(m // tm,),
        in_specs=[pl.BlockSpec((tm, d), lambda i: (i, 0)),
                  pl.BlockSpec((1, d), lambda i: (0, 0)),
                  pl.BlockSpec((1, d), lambda i: (0, 0))],
        out_specs=pl.BlockSpec((tm, d), lambda i: (i, 0)),
        out_shape=jax.ShapeDtypeStruct((m, d), F32),
        compiler_params=_cparams(("parallel",)),
        name="ln_rows",
    )(x, g.reshape(1, d), b.reshape(1, d))


def _mm_kernel(x_ref, w_ref, o_ref, xb_ref):
    @pl.when(pl.program_id(1) == 0)
    def _():
        xb_ref[...] = x_ref[...].astype(BF)

    o_ref[...] = _dot(xb_ref[...], w_ref[...]).astype(o_ref.dtype)


def matmul(x, w, tm, tn, out_dtype=F32):
    m, k = x.shape
    n = w.shape[1]
    assert m % tm == 0 and n % tn == 0, (m, tm, n, tn)
    return pl.pallas_call(
        _mm_kernel,
        grid=(m // tm, n // tn),
        in_specs=[pl.BlockSpec((tm, k), lambda i, j: (i, 0)),
                  pl.BlockSpec((k, tn), lambda i, j: (0, j))],
        out_specs=pl.BlockSpec((tm, tn), lambda i, j: (i, j)),
        out_shape=jax.ShapeDtypeStruct((m, n), out_dtype),
        scratch_shapes=[pltpu.VMEM((tm, k), BF)],
        compiler_params=_cparams(("parallel", "arbitrary")),
        name="matmul",
    )(x, w)


def _mm_res_ln_kernel(*refs, n_in):
    xs = refs[:n_in]
    w_ref, r_ref, g_ref, b_ref, o_ref = refs[n_in:]
    acc = None
    k0 = 0
    for x_ref in xs:
        kk = x_ref.shape[1]
        part = _dot(x_ref[...].astype(BF), w_ref[k0:k0 + kk, :])
        acc = part if acc is None else acc + part
        k0 += kk
    o_ref[...] = _layer_norm(DN_ALPHA * r_ref[...] + acc, g_ref[...], b_ref[...])


def matmul_res_ln(xs, w, resid, g, b, tm):
    m, d = resid.shape
    assert m % tm == 0, (m, tm)
    in_specs = [pl.BlockSpec((tm, x.shape[1]), lambda i: (i, 0)) for x in xs]
    in_specs += [pl.BlockSpec(w.shape, lambda i: (0, 0)),
                 pl.BlockSpec((tm, d), lambda i: (i, 0)),
                 pl.BlockSpec((1, d), lambda i: (0, 0)),
                 pl.BlockSpec((1, d), lambda i: (0, 0))]
    return pl.pallas_call(
        functools.partial(_mm_res_ln_kernel, n_in=len(xs)),
        grid=(m // tm,),
        in_specs=in_specs,
        out_specs=pl.BlockSpec((tm, d), lambda i: (i, 0)),
        out_shape=jax.ShapeDtypeStruct((m, d), F32),
        compiler_params=_cparams(("parallel",)),
        name="matmul_res_ln",
    )(*xs, w, resid, g.reshape(1, d), b.reshape(1, d))


def _pool_kernel(x_ref, pre_ref, w_ref, sc_ref, o_ref, ext_ref, *, bb, tt, n_tiles, t0):
    j = pl.program_id(1)

    @pl.when(j == 0)
    def _():
        ext_ref[:, 0:POOL_PRE, :] = pre_ref[...]

    x = x_ref[...].reshape(bb, tt, GROUP_W)
    ext_ref[:, POOL_PRE:, :] = x
    t_idx = j * tt + lax.broadcasted_iota(jnp.int32, (1, tt, LANES), 1)
    for g, win in enumerate(POOL_WINDOWS):
        cs = slice(g * LANES, (g + 1) * LANES)
        xg = ext_ref[:, POOL_PRE:POOL_PRE + tt, cs]
        s = xg
        for k in range(1, win):
            s = s + ext_ref[:, POOL_PRE - k:POOL_PRE - k + tt, cs]
        cnt = jnp.minimum(win, t0 + 1 + t_idx).astype(F32)
        d = s / cnt - xg
        y = _dot(d.reshape(bb * tt, LANES).astype(BF), w_ref[g])
        o_ref[:, cs] = (y * sc_ref[:, cs]).astype(o_ref.dtype)
    if n_tiles > 1:
        ext_ref[:, 0:POOL_PRE, :] = ext_ref[:, tt:tt + POOL_PRE, :]


def pool_mixer(proj, row0, n_b, t_len, prefix, pool_w, pool_scale, t0, bb, tt):
    n_tiles = t_len // tt
    rb = bb * tt
    base = row0 // rb
    return pl.pallas_call(
        functools.partial(_pool_kernel, bb=bb, tt=tt, n_tiles=n_tiles, t0=t0),
        grid=(n_b // bb, n_tiles),
        in_specs=[pl.BlockSpec((rb, GROUP_W), lambda i, j: (base + i * n_tiles + j, CB_XA // 4)),
                  pl.BlockSpec((bb, POOL_PRE, GROUP_W), lambda i, j: (i, 0, 0)),
                  pl.BlockSpec((4, LANES, LANES), lambda i, j: (0, 0, 0)),
                  pl.BlockSpec((1, GROUP_W), lambda i, j: (0, 0))],
        out_specs=pl.BlockSpec((rb, GROUP_W), lambda i, j: (i * n_tiles + j, 0)),
        out_shape=jax.ShapeDtypeStruct((n_b * t_len, GROUP_W), BF),
        scratch_shapes=[pltpu.VMEM((bb, POOL_PRE + tt, GROUP_W), F32)],
        compiler_params=_cparams(("parallel", "arbitrary")),
        name="pool_mixer",
    )(proj, prefix, pool_w.astype(BF), pool_scale.reshape(1, GROUP_W))


def _gelu_tanh(x):
    return 0.5 * x * (1.0 + jnp.tanh(np.sqrt(2.0 / np.pi) * (x + 0.044715 * (x * x * x))))


def _gmlp_kernel(u_ref, v_ref, g_ref, b_ref, w_ref, bias_ref, y_ref, vo_ref, *, n_ch):
    for c in range(n_ch):
        rows = slice(c * GMLP_CHUNK, (c + 1) * GMLP_CHUNK)
        u = _gelu_tanh(u_ref[rows, :])
        v = _layer_norm(_gelu_tanh(v_ref[rows, :]), g_ref[...], b_ref[...])
        vo_ref[rows, :] = v
        vb = v.astype(BF)
        for g in range(4):
            cs = slice(g * LANES, (g + 1) * LANES)
            mixed = _dot(w_ref[g], vb[:, cs]) + bias_ref[:, cs]
            y_ref[rows, cs] = (u[:, cs] * mixed).astype(y_ref.dtype)


def gmlp_mixer(proj, row0, n_rows, ln_g, ln_b, w_mix, bias_full, n_ch):
    rb = n_ch * GMLP_CHUNK
    base = row0 // rb
    return pl.pallas_call(
        functools.partial(_gmlp_kernel, n_ch=n_ch),
        grid=(n_rows // rb,),
        in_specs=[pl.BlockSpec((rb, GROUP_W), lambda i: (base + i, CB_U // 4)),
                  pl.BlockSpec((rb, GROUP_W), lambda i: (base + i, CB_V // 4)),
                  pl.BlockSpec((1, GROUP_W), lambda i: (0, 0)),
                  pl.BlockSpec((1, GROUP_W), lambda i: (0, 0)),
                  pl.BlockSpec((4, GMLP_CHUNK, GMLP_CHUNK), lambda i: (0, 0, 0)),
                  pl.BlockSpec((GMLP_CHUNK, GROUP_W), lambda i: (0, 0))],
        out_specs=[pl.BlockSpec((rb, GROUP_W), lambda i: (i, 0)),
                   pl.BlockSpec((rb, GROUP_W), lambda i: (i, 0))],
        out_shape=[jax.ShapeDtypeStruct((n_rows, GROUP_W), BF),
                   jax.ShapeDtypeStruct((n_rows, GROUP_W), F32)],
        compiler_params=_cparams(("parallel",)),
        name="gmlp_mixer",
    )(proj, proj, ln_g.reshape(1, GROUP_W), ln_b.reshape(1, GROUP_W), w_mix, bias_full)


def _cmp_accumulate(row_pair, phi_ref):
    acc = None
    for l2 in range(CMP_BLOCK // 2):
        xa, xb = row_pair(l2)
        part = _dot(jnp.concatenate([xa, xb], axis=1).astype(BF), phi_ref[l2])
        acc = part if acc is None else acc + part
    return acc


def _cmp_kernel(xk_ref, xv_ref, phik_ref, phiv_ref, ok_ref, ov_ref, *, tb):
    for x_ref, phi_ref, o_ref in ((xk_ref, phik_ref, ok_ref), (xv_ref, phiv_ref, ov_ref)):
        o_ref[...] = _cmp_accumulate(
            lambda l2: (x_ref[pl.ds(2 * l2, tb, stride=CMP_BLOCK), :],
                        x_ref[pl.ds(2 * l2 + 1, tb, stride=CMP_BLOCK), :]), phi_ref)


def cmp_project(proj, n_blk, phik2, phiv2, tb):
    assert n_blk % tb == 0, (n_blk, tb)
    rows = tb * CMP_BLOCK
    return pl.pallas_call(
        functools.partial(_cmp_kernel, tb=tb),
        grid=(n_blk // tb,),
        in_specs=[pl.BlockSpec((rows, LANES), lambda i: (i, CB_CMPK)),
                  pl.BlockSpec((rows, LANES), lambda i: (i, CB_CMPV)),
                  pl.BlockSpec(phik2.shape, lambda i: (0, 0, 0)),
                  pl.BlockSpec(phiv2.shape, lambda i: (0, 0, 0))],
        out_specs=[pl.BlockSpec((tb, NSA_DH), lambda i: (i, 0)),
                   pl.BlockSpec((tb, NSA_DH), lambda i: (i, 0))],
        out_shape=[jax.ShapeDtypeStruct((n_blk, NSA_DH), F32),
                   jax.ShapeDtypeStruct((n_blk, NSA_DH), F32)],
        compiler_params=_cparams(("parallel",)),
        name="cmp_project",
    )(proj, proj, phik2, phiv2)


def _cmp_pool_kernel(x_ref, phik_ref, phiv_ref, o_ref):
    for s, phi_ref in ((0, phik_ref), (1, phiv_ref)):
        o_ref[:, s * NSA_DH:(s + 1) * NSA_DH] = _cmp_accumulate(
            lambda l2: (x_ref[:, 2 * l2, s, :], x_ref[:, 2 * l2 + 1, s, :]), phi_ref)


def cmp_pool(cache_blocks, blk0, n_blk, phik2, phiv2, tb):
    assert n_blk % tb == 0 and blk0 % tb == 0, (n_blk, blk0, tb)
    base = blk0 // tb
    return pl.pallas_call(
        _cmp_pool_kernel,
        grid=(n_blk // tb,),
        in_specs=[pl.BlockSpec((tb, CMP_BLOCK, 4, NSA_DH), lambda i: (base + i, 0, 0, 0)),
                  pl.BlockSpec(phik2.shape, lambda i: (0, 0, 0)),
                  pl.BlockSpec(phiv2.shape, lambda i: (0, 0, 0))],
        out_specs=pl.BlockSpec((tb, 2 * NSA_DH), lambda i: (i, 0)),
        out_shape=jax.ShapeDtypeStruct((n_blk, 2 * NSA_DH), F32),
        compiler_params=_cparams(("parallel",)),
        name="cmp_pool",
    )(cache_blocks, phik2, phiv2)


def _alibi_slope(h):
    return 2.0 ** (-8.0 * (h + 1) / NSA_H)


def _head_rows_const(n_q, fn):
    r = lax.broadcasted_iota(jnp.int32, (NSA_H * n_q, 1), 0)
    out = jnp.full((NSA_H * n_q, 1), fn(NSA_H - 1), F32)
    for h in range(NSA_H - 2, -1, -1):
        out = jnp.where(r < (h + 1) * n_q, fn(h), out)
    return out


def _stack_heads(q):
    return jnp.concatenate([q[:, h * NSA_DH:(h + 1) * NSA_DH] for h in range(NSA_H)], axis=0)


def _select_blocks(score, n_blk):
    blk = lax.broadcasted_iota(jnp.int32, score.shape, 0)
    cnt = jnp.zeros(score.shape, F32)
    for i in range(n_blk):
        row = score[i:i + 1, :]
        beats = (row > score) | ((row == score) & (blk > i))
        cnt = cnt + jnp.where(beats, 1.0, 0.0)
    return jnp.where((cnt < float(SEL_TOP)) & (score > NEG_INF), 1.0, 0.0)


def _softmax_segments(segs):
    m = None
    for s, _ in segs:
        mi = jnp.max(s, axis=1, keepdims=True)
        m = mi if m is None else jnp.maximum(m, mi)
    m = jnp.where(m > NEG_INF, m, 0.0)
    den = None
    acc = None
    for s, v in segs:
        e = jnp.exp(s - m)
        d = jnp.sum(e, axis=1, keepdims=True)
        a = _dot(e.astype(BF), v)
        den = d if den is None else den + d
        acc = a if acc is None else acc + a
    return acc / jnp.where(den > 0, den, 1.0)


def _gated_heads(o_ref, gates, o_c, o_s, o_w, n_q):
    for h in range(NSA_H):
        rows = slice(h * n_q, (h + 1) * n_q)
        c = MISC_GATE0 + 3 * h
        out = (gates[:, c:c + 1] * o_c[rows, :] + gates[:, c + 1:c + 2] * o_s[rows, :]
               + gates[:, c + 2:c + 3] * o_w[rows, :])
        o_ref[:, h * NSA_DH:(h + 1) * NSA_DH] = out.astype(o_ref.dtype)


def _nsa_prompt_kernel(q_ref, misc_ref, kc_ref, vc_ref, ks_ref, vs_ref, kw_ref, vw_ref, e_ref, o_ref,
                       imp_ref, mask_ref, m_ref, l_ref, acc_ref, *, t_len):
    n_q = Q_BLOCK
    n_cmp = t_len // CMP_BLOCK
    n_sel = t_len // SEL_BLOCK
    kt = 512
    q0 = pl.program_id(1) * n_q
    q4 = _stack_heads(q_ref[...] * (NSA_DH ** -0.5)).astype(BF)
    slope_r = _head_rows_const(n_q, _alibi_slope)
    qpos_r = q0 + lax.rem(lax.broadcasted_iota(jnp.int32, (NSA_H * n_q, 1), 0), n_q)

    lane = lax.broadcasted_iota(jnp.int32, (1, NSA_H * n_q), 1)
    qpos_l = q0 + lax.rem(lane, n_q)
    slope_l = jnp.full((1, NSA_H * n_q), _alibi_slope(NSA_H - 1), F32)
    for h in range(NSA_H - 2, -1, -1):
        slope_l = jnp.where(lane < (h + 1) * n_q, _alibi_slope(h), slope_l)
    cmp_end = lax.broadcasted_iota(jnp.int32, (n_cmp, 1), 0) * CMP_BLOCK + (CMP_BLOCK - 1)
    s_c = _dot_nt(kc_ref[...].astype(BF), q4) - slope_l * (qpos_l - cmp_end).astype(F32)
    s_c = jnp.where(cmp_end <= qpos_l, s_c, NEG_INF)
    m_c = jnp.max(s_c, axis=0, keepdims=True)
    m_c = jnp.where(m_c > NEG_INF, m_c, 0.0)
    e_c = jnp.exp(s_c - m_c)
    den_c = jnp.sum(e_c, axis=0, keepdims=True)
    p_c = e_c / jnp.where(den_c > 0, den_c, 1.0)
    o_c = _dot_tn(p_c.astype(BF), vc_ref[...].astype(BF))

    imp = p_c[:, 0:n_q]
    for h in range(1, NSA_H):
        imp = imp + p_c[:, h * n_q:(h + 1) * n_q]
    imp_ref[...] = imp
    imp = imp_ref[pl.ds(0, n_sel, stride=2), :] + imp_ref[pl.ds(1, n_sel, stride=2), :]
    blk = lax.broadcasted_iota(jnp.int32, (n_sel, 1), 0)
    cur = (q0 + lax.broadcasted_iota(jnp.int32, (1, n_q), 1)) // SEL_BLOCK
    forced = (blk == 0) | (blk == cur) | (blk == cur - 1)
    score = jnp.where(blk <= cur, jnp.where(forced, jnp.inf, imp), NEG_INF)
    sel = _select_blocks(score, n_sel)
    mask_ref[...] = _dot_tn(sel.astype(BF), e_ref[...])

    m_ref[...] = jnp.full(m_ref.shape, NEG_INF, F32)
    l_ref[...] = jnp.zeros(l_ref.shape, F32)
    acc_ref[...] = jnp.zeros(acc_ref.shape, F32)

    def sel_tile(t, carry):
        k0 = pl.multiple_of(t * kt, kt)
        s = _dot_nt(q4, ks_ref[pl.ds(k0, kt), :].astype(BF))
        d = qpos_r - (k0 + lax.broadcasted_iota(jnp.int32, (1, kt), 1))
        mk = mask_ref[:, pl.ds(k0, kt)]
        mk = jnp.concatenate([mk] * NSA_H, axis=0)
        s = jnp.where((d >= 0) & (mk > 0.5), s - slope_r * d.astype(F32), NEG_INF)
        m_old = m_ref[...]
        m_new = jnp.maximum(m_old, jnp.max(s, axis=1, keepdims=True))
        m_use = jnp.where(m_new > NEG_INF, m_new, 0.0)
        alpha = jnp.exp(m_old - m_use)
        p = jnp.exp(s - m_use)
        l_ref[...] = alpha * l_ref[...] + jnp.sum(p, axis=1, keepdims=True)
        acc_ref[...] = alpha * acc_ref[...] + _dot(p.astype(BF), vs_ref[pl.ds(k0, kt), :].astype(BF))
        m_ref[...] = m_new
        return carry

    lax.fori_loop(0, (q0 + n_q + kt - 1) // kt, sel_tile, 0)
    l_s = l_ref[...]
    o_s = acc_ref[...] / jnp.where(l_s > 0, l_s, 1.0)

    nw = WINDOW + n_q
    w0 = pl.multiple_of(jnp.maximum(q0 - WINDOW, 0), n_q)
    s_w = _dot_nt(q4, kw_ref[pl.ds(w0, nw), :].astype(BF))
    d_w = qpos_r - (w0 + lax.broadcasted_iota(jnp.int32, (1, nw), 1))
    s_w = jnp.where((d_w >= 0) & (d_w < WINDOW), s_w - slope_r * d_w.astype(F32), NEG_INF)
    o_w = _softmax_segments([(s_w, vw_ref[pl.ds(w0, nw), :].astype(BF))])

    _gated_heads(o_ref, _sigmoid(misc_ref[...]), o_c, o_s, o_w, n_q)


def nsa_prompt(proj, n_b, t_len, k_cmp, v_cmp, expand):
    n_qb = t_len // Q_BLOCK
    n_cmp = t_len // CMP_BLOCK
    n_sel = t_len // SEL_BLOCK
    seq = lambda cb: pl.BlockSpec((t_len, LANES), lambda b, i: (b, cb))
    return pl.pallas_call(
        functools.partial(_nsa_prompt_kernel, t_len=t_len),
        grid=(n_b, n_qb),
        in_specs=[pl.BlockSpec((Q_BLOCK, GROUP_W), lambda b, i: (b * n_qb + i, CB_QC // 4)),
                  pl.BlockSpec((Q_BLOCK, LANES), lambda b, i: (b * n_qb + i, CB_MISC)),
                  pl.BlockSpec((n_cmp, NSA_DH), lambda b, i: (b, 0)),
                  pl.BlockSpec((n_cmp, NSA_DH), lambda b, i: (b, 0)),
                  seq(CB_SELK), seq(CB_SELV), seq(CB_WINK), seq(CB_WINV),
                  pl.BlockSpec((n_sel, t_len), lambda b, i: (0, 0))],
        out_specs=pl.BlockSpec((Q_BLOCK, GROUP_W), lambda b, i: (b * n_qb + i, 0)),
        out_shape=jax.ShapeDtypeStruct((n_b * t_len, GROUP_W), BF),
        scratch_shapes=[pltpu.VMEM((n_cmp, Q_BLOCK), F32),
                        pltpu.VMEM((Q_BLOCK, t_len), F32),
                        pltpu.VMEM((NSA_H * Q_BLOCK, 1), F32),
                        pltpu.VMEM((NSA_H * Q_BLOCK, 1), F32),
                        pltpu.VMEM((NSA_H * Q_BLOCK, NSA_DH), F32)],
        compiler_params=_cparams(("parallel", "arbitrary")),
        name="nsa_prompt",
    )(proj, proj, k_cmp, v_cmp, proj, proj, proj, proj, expand)


def _nsa_sample_kernel(pt_ref, q_ref, misc_ref, new4_ref, new2_ref, winc_ref, pair_ref, e_ref,
                       *rest, n_pages, page, t_q):
    pages = rest[:n_pages]
    kvcs = rest[n_pages:2 * n_pages]
    o_ref, kvc_ref, ks_ref, vs_ref = rest[2 * n_pages:]
    past_len = n_pages * page
    cpp = page // CMP_BLOCK
    n_cmp = past_len // CMP_BLOCK
    n_sel = past_len // SEL_BLOCK + 1
    n_r = NSA_H * t_q
    for j in range(n_pages):
        kvc_ref[j * cpp:(j + 1) * cpp, :] = kvcs[j][...]
        ks_ref[j * page:(j + 1) * page, :] = pages[j][:, 2, :].astype(BF)
        vs_ref[j * page:(j + 1) * page, :] = pages[j][:, 3, :].astype(BF)
    kc_ref = kvc_ref.at[:, 0:NSA_DH]
    vc_ref = kvc_ref.at[:, NSA_DH:2 * NSA_DH]
    ksn_ref = new4_ref.at[:, 2 * NSA_DH:3 * NSA_DH]
    vsn_ref = new4_ref.at[:, 3 * NSA_DH:4 * NSA_DH]
    kwn_ref = new2_ref.at[:, 0:NSA_DH]
    vwn_ref = new2_ref.at[:, NSA_DH:2 * NSA_DH]

    q4 = _stack_heads(q_ref[...] * (NSA_DH ** -0.5)).astype(BF)
    slope_r = _head_rows_const(t_q, _alibi_slope)
    qpos_r = past_len + lax.rem(lax.broadcasted_iota(jnp.int32, (n_r, 1), 0), t_q)

    cmp_end = lax.broadcasted_iota(jnp.int32, (1, n_cmp), 1) * CMP_BLOCK + (CMP_BLOCK - 1)
    s_c = _dot_nt(q4, kc_ref[...].astype(BF)) - slope_r * (qpos_r - cmp_end).astype(F32)
    s_c = jnp.where(cmp_end <= qpos_r, s_c, NEG_INF)
    m_c = jnp.max(s_c, axis=1, keepdims=True)
    m_c = jnp.where(m_c > NEG_INF, m_c, 0.0)
    e_c = jnp.exp(s_c - m_c)
    den_c = jnp.sum(e_c, axis=1, keepdims=True)
    p_c = e_c / jnp.where(den_c > 0, den_c, 1.0)
    o_c = _dot(p_c.astype(BF), vc_ref[...].astype(BF))

    imp = p_c[0:t_q, :]
    for h in range(1, NSA_H):
        imp = imp + p_c[h * t_q:(h + 1) * t_q, :]
    hi = imp.astype(BF)
    lo = (imp - hi.astype(F32)).astype(BF)
    imp = _dot(hi, pair_ref[...]) + _dot(lo, pair_ref[...])
    blk = lax.broadcasted_iota(jnp.int32, (1, LANES), 1)
    cur = (past_len + lax.broadcasted_iota(jnp.int32, (t_q, 1), 0)) // SEL_BLOCK
    forced = (blk == 0) | (blk == cur) | (blk == cur - 1)
    score = jnp.where(blk <= cur, jnp.where(forced, jnp.inf, imp), NEG_INF)
    cnt = jnp.zeros(score.shape, F32)
    for i in range(n_sel):
        col = score[:, i:i + 1]
        beats = (col > score) | ((col == score) & (blk > i))
        cnt = cnt + jnp.where(beats, 1.0, 0.0)
    sel = jnp.where((cnt < float(SEL_TOP)) & (score > NEG_INF), 1.0, 0.0)
    mask = _dot(sel.astype(BF), e_ref[...])
    mask = jnp.concatenate([mask] * NSA_H, axis=0)

    d_p = qpos_r - lax.broadcasted_iota(jnp.int32, (1, past_len), 1)
    s_p = jnp.where(mask[:, 0:past_len] > 0.5, _dot_nt(q4, ks_ref[...]) - slope_r * d_p.astype(F32), NEG_INF)
    d_n = qpos_r - (past_len + lax.broadcasted_iota(jnp.int32, (1, t_q), 1))
    s_n = _dot_nt(q4, ksn_ref[...].astype(BF)) - slope_r * d_n.astype(F32)
    s_n = jnp.where((d_n >= 0) & (mask[:, past_len:past_len + t_q] > 0.5), s_n, NEG_INF)
    o_s = _softmax_segments([(s_p, vs_ref[...]), (s_n, vsn_ref[...].astype(BF))])

    n_wc = winc_ref.shape[0]
    d_c = qpos_r - (past_len - n_wc + lax.broadcasted_iota(jnp.int32, (1, n_wc), 1))
    s_wc = _dot_nt(q4, winc_ref[:, 0, :].astype(BF)) - slope_r * d_c.astype(F32)
    s_wc = jnp.where(d_c < WINDOW, s_wc, NEG_INF)
    s_wn = _dot_nt(q4, kwn_ref[...].astype(BF)) - slope_r * d_n.astype(F32)
    s_wn = jnp.where(d_n >= 0, s_wn, NEG_INF)
    o_w = _softmax_segments([(s_wc, winc_ref[:, 1, :].astype(BF)), (s_wn, vwn_ref[...].astype(BF))])

    _gated_heads(o_ref, _sigmoid(misc_ref[...]), o_c, o_s, o_w, t_q)


def nsa_sample(proj, row0, n_b, t_q, cache_nsa, layer, page_table, kvc_pool, cache_win, pair, expand):
    n_pages = page_table.shape[1]
    page = cache_nsa.shape[2]
    cpp = page // CMP_BLOCK
    past_len = n_pages * page
    n_wc = cache_win.shape[2]
    base = row0 // t_q
    assert CB_SELV == CB_CMPK + 3 and CB_CMPK % 4 == 0 and CB_WINV == CB_WINK + 1 and CB_WINK % 2 == 0

    def page_spec(j):
        return pl.BlockSpec((None, None, page, 4, NSA_DH), lambda b, pt: (layer, pt[b, j], 0, 0, 0))

    def cmp_spec(j):
        return pl.BlockSpec((None, cpp, 2 * NSA_DH), lambda b, pt: (pt[b, j], 0, 0))

    in_specs = [pl.BlockSpec((t_q, GROUP_W), lambda b, pt: (base + b, CB_QC // 4)),
                pl.BlockSpec((t_q, LANES), lambda b, pt: (base + b, CB_MISC)),
                pl.BlockSpec((t_q, 4 * NSA_DH), lambda b, pt: (base + b, CB_CMPK // 4)),
                pl.BlockSpec((t_q, 2 * NSA_DH), lambda b, pt: (base + b, CB_WINK // 2)),
                pl.BlockSpec((None, None, n_wc, 2, NSA_DH), lambda b, pt: (layer, b, 0, 0, 0)),
                pl.BlockSpec(pair.shape, lambda b, pt: (0, 0)),
                pl.BlockSpec(expand.shape, lambda b, pt: (0, 0))]
    in_specs += [page_spec(j) for j in range(n_pages)]
    in_specs += [cmp_spec(j) for j in range(n_pages)]
    return pl.pallas_call(
        functools.partial(_nsa_sample_kernel, n_pages=n_pages, page=page, t_q=t_q),
        grid_spec=pltpu.PrefetchScalarGridSpec(
            num_scalar_prefetch=1,
            grid=(n_b,),
            in_specs=in_specs,
            out_specs=pl.BlockSpec((t_q, GROUP_W), lambda b, pt: (b, 0)),
            scratch_shapes=[pltpu.VMEM((past_len // CMP_BLOCK, 2 * NSA_DH), F32),
                            pltpu.VMEM((past_len, NSA_DH), BF),
                            pltpu.VMEM((past_len, NSA_DH), BF)]),
        out_shape=jax.ShapeDtypeStruct((n_b * t_q, GROUP_W), BF),
        compiler_params=_cparams(("arbitrary",)),
        name="nsa_sample",
    )(page_table, proj, proj, proj, proj, cache_win, pair, expand,
      *([cache_nsa] * n_pages), *([kvc_pool] * n_pages))


def _log_sigmoid(z):
    return jnp.minimum(z, 0.0) - jnp.log(1.0 + jnp.exp(-jnp.abs(z)))


def _cumsum_rows(x, tri):
    hi = x.astype(BF)
    lo = (x - hi.astype(F32)).astype(BF)
    return _dot(tri, hi) + _dot(tri, lo)


def _rows_to_col(row):
    n = row.shape[1]
    eye = lax.broadcasted_iota(jnp.int32, (n, n), 0) == lax.broadcasted_iota(jnp.int32, (n, n), 1)
    return jnp.sum(jnp.where(eye, row, 0.0), axis=1, keepdims=True)


def _gla_out(o, god, ng):
    o = o * lax.rsqrt(jnp.mean(o * o, -1, keepdims=True) + LN_EPS) * ng
    return o * (god * _sigmoid(god))


def _gla_prompt_kernel(q_ref, k_ref, v_ref, god_ref, misc_ref, wa_ref, ba_ref, ng_ref, o_ref, s_ref, st_ref,
                       *, n_chunks):
    c = GLA_CHUNK
    st_ref[...] = jnp.zeros(st_ref.shape, F32)
    r_i = lax.broadcasted_iota(jnp.int32, (c, c), 0)
    c_i = lax.broadcasted_iota(jnp.int32, (c, c), 1)
    causal = r_i >= c_i
    tri = jnp.where(causal, 1.0, 0.0).astype(BF)

    def chunk(n, carry):
        rows = pl.ds(pl.multiple_of(n * c, c), c)
        misc = misc_ref[rows, :].astype(BF)
        for h in range(GLA_H):
            cs = slice(h * LANES, (h + 1) * LANES)
            z = _dot(misc, wa_ref[:, cs]) + ba_ref[:, cs]
            bc = _cumsum_rows(_log_sigmoid(z) / GLA_TAU, tri)
            bl = bc[c - 1:c, :]
            q = q_ref[rows, cs] * (GLA_DK ** -0.5)
            k = k_ref[rows, cs]
            v = v_ref[rows, cs].astype(BF)
            qe = (q * jnp.exp(bc)).astype(BF)
            att = jnp.where(causal, _dot_nt(qe, (k * jnp.exp(-bc)).astype(BF)), 0.0)
            st = st_ref[h]
            o = _dot(qe, st.astype(BF)) + _dot(att.astype(BF), v)
            st_ref[h] = _rows_to_col(jnp.exp(bl)) * st + _dot_tn((k * jnp.exp(bl - bc)).astype(BF), v)
            o_ref[rows, cs] = _gla_out(o, god_ref[rows, cs], ng_ref[...]).astype(o_ref.dtype)
        return carry

    lax.fori_loop(0, n_chunks, chunk, 0)
    for h in range(GLA_H):
        s_ref[h] = st_ref[h, 0:GLA_DK, :]


def gla_prompt(proj, n_b, t_len, wa_pad, ba_pad, norm_g):
    w = GLA_H * LANES
    seq = lambda cb0: pl.BlockSpec((t_len, w), lambda b: (b, cb0 // GLA_H))
    return pl.pallas_call(
        functools.partial(_gla_prompt_kernel, n_chunks=t_len // GLA_CHUNK),
        grid=(n_b,),
        in_specs=[seq(CB_QD), seq(CB_KD), seq(CB_VD), seq(CB_GOD),
                  pl.BlockSpec((t_len, LANES), lambda b: (b, CB_MISC)),
                  pl.BlockSpec((LANES, w), lambda b: (0, 0)),
                  pl.BlockSpec((1, w), lambda b: (0, 0)),
                  pl.BlockSpec((1, GLA_DV), lambda b: (0, 0))],
        out_specs=[pl.BlockSpec((t_len, w), lambda b: (b, 0)),
                   pl.BlockSpec((None, GLA_H, GLA_DK, GLA_DV), lambda b: (b, 0, 0, 0))],
        out_shape=[jax.ShapeDtypeStruct((n_b * t_len, GROUP_W), BF),
                   jax.ShapeDtypeStruct((n_b, GLA_H, GLA_DK, GLA_DV), F32)],
        scratch_shapes=[pltpu.VMEM((GLA_H, LANES, GLA_DV), F32)],
        compiler_params=_cparams(("parallel",)),
        name="gla_prompt",
    )(proj, proj, proj, proj, proj, wa_pad, ba_pad, norm_g.reshape(1, GLA_DV))


def _gla_sample_kernel(q_ref, k_ref, v_ref, god_ref, misc_ref, wa_ref, ba_ref, ng_ref, s0_ref, o_ref, s_ref,
                       *, bb, t_q):
    n_r = bb * t_q
    r_i = lax.broadcasted_iota(jnp.int32, (n_r, n_r), 0)
    c_i = lax.broadcasted_iota(jnp.int32, (n_r, n_r), 1)
    causal = (r_i >= c_i) & (r_i // t_q == c_i // t_q)
    tri = jnp.where(causal, 1.0, 0.0).astype(BF)
    row_b = lax.broadcasted_iota(jnp.int32, (n_r, 1), 0) // t_q

    z = _dot(misc_ref[...].astype(BF), wa_ref[...]) + ba_ref[...]
    bc = _cumsum_rows(_log_sigmoid(z) / GLA_TAU, tri)
    q = q_ref[...] * (GLA_DK ** -0.5)
    k = k_ref[...]
    v = v_ref[...].astype(BF)
    qe = q * jnp.exp(bc)
    att = jnp.where(causal, _dot_nt(qe.astype(BF), (k * jnp.exp(-bc)).astype(BF)), 0.0)
    o = _dot(att.astype(BF), v)
    for b in range(bb):
        mine = row_b == b
        bl = bc[(b + 1) * t_q - 1:(b + 1) * t_q, :]
        s0 = s0_ref[b, 0]
        o = o + _dot(jnp.where(mine, qe, 0.0)[:, 0:GLA_DK].astype(BF), s0.astype(BF))
        k2 = jnp.where(mine, k * jnp.exp(bl - bc), 0.0).astype(BF)
        upd = _dot_tn(k2, v)
        s_ref[b, 0] = _rows_to_col(jnp.exp(bl))[0:GLA_DK, :] * s0 + upd[0:GLA_DK, :]
    o_ref[...] = _gla_out(o, god_ref[...], ng_ref[...]).astype(o_ref.dtype)


def gla_sample(proj, row0, n_b, t_q, wa_pad, ba_pad, norm_g, s0, s0_b0, bb):
    rb = bb * t_q
    base = row0 // rb
    s_i0 = s0_b0 // bb
    seq = lambda cb0: pl.BlockSpec((rb, LANES), lambda i, h: (base + i, cb0 + h))
    return pl.pallas_call(
        functools.partial(_gla_sample_kernel, bb=bb, t_q=t_q),
        grid=(n_b // bb, GLA_H),
        in_specs=[seq(CB_QD), seq(CB_KD), seq(CB_VD), seq(CB_GOD),
                  pl.BlockSpec((rb, LANES), lambda i, h: (base + i, CB_MISC)),
                  pl.BlockSpec((LANES, LANES), lambda i, h: (0, h)),
                  pl.BlockSpec((1, LANES), lambda i, h: (0, h)),
                  pl.BlockSpec((1, GLA_DV), lambda i, h: (0, 0)),
                  pl.BlockSpec((bb, 1, GLA_DK, GLA_DV), lambda i, h: (s_i0 + i, h, 0, 0))],
        out_specs=[pl.BlockSpec((rb, GLA_DV), lambda i, h: (i, h)),
                   pl.BlockSpec((bb, 1, GLA_DK, GLA_DV), lambda i, h: (i, h, 0, 0))],
        out_shape=[jax.ShapeDtypeStruct((n_b * t_q, GROUP_W), BF),
                   jax.ShapeDtypeStruct((n_b, GLA_H, GLA_DK, GLA_DV), F32)],
        compiler_params=_cparams(("parallel", "parallel")),
        name="gla_sample",
    )(proj, proj, proj, proj, proj, wa_pad, ba_pad, norm_g.reshape(1, GLA_DV), s0)


def _mem_attn_kernel(q_ref, kv_ref, o_ref):
    q = q_ref[...] * (MEM_DH ** -0.5)
    for h in range(MEM_H):
        cs = slice(h * MEM_DH, (h + 1) * MEM_DH)
        s = _dot_nt(q[:, cs].astype(BF), kv_ref[:, 0, h, :].astype(BF))
        e = jnp.exp(s - jnp.max(s, axis=1, keepdims=True))
        p = e / jnp.sum(e, axis=1, keepdims=True)
        o_ref[:, cs] = _dot(p.astype(BF), kv_ref[:, 1, h, :].astype(BF)).astype(o_ref.dtype)


def mem_attention(qm, row0, n_b, t_len, kv, layer, tq):
    n_t = t_len // tq
    base = row0 // tq
    w = MEM_H * MEM_DH
    return pl.pallas_call(
        _mem_attn_kernel,
        grid=(n_b, n_t),
        in_specs=[pl.BlockSpec((tq, w), lambda b, i: (base + b * n_t + i, 0)),
                  pl.BlockSpec((None, None, MEM_LEN, 2, MEM_H, MEM_DH), lambda b, i: (layer, b, 0, 0, 0, 0))],
        out_specs=pl.BlockSpec((tq, w), lambda b, i: (b * n_t + i, 0)),
        out_shape=jax.ShapeDtypeStruct((n_b * t_len, w), BF),
        compiler_params=_cparams(("parallel", "parallel")),
        name="mem_attention",
    )(qm, kv)


def _router_kernel(x_ref, w_ref, b_ref, tri_ref, idx_ref, gate_ref, rank_ref, cnt_ref, run_ref):
    @pl.when(pl.program_id(0) == 0)
    def _():
        run_ref[...] = jnp.zeros(run_ref.shape, F32)

    logits = jnp.dot(x_ref[...], w_ref[...], precision=lax.Precision.HIGHEST,
                     preferred_element_type=F32) + b_ref[...]
    lane = lax.broadcasted_iota(jnp.int32, logits.shape, 1).astype(F32)
    vals = jnp.where(lane < N_EXPERTS, logits, NEG_INF)
    idx_out = jnp.zeros(logits.shape, F32)
    e_out = jnp.zeros(logits.shape, F32)
    den = None
    top0 = None
    onehots = []
    for k in range(TOP_K):
        m = jnp.max(vals, axis=1, keepdims=True)
        idx = jnp.min(jnp.where(vals == m, lane, float(LANES)), axis=1, keepdims=True)
        if k == 0:
            top0 = m
        e = jnp.exp(m - top0)
        den = e if den is None else den + e
        idx_out = jnp.where(lane == k, idx, idx_out)
        e_out = jnp.where(lane == k, e, e_out)
        onehots.append(jnp.where(lane == idx, 1.0, 0.0))
        vals = jnp.where(lane == idx, NEG_INF, vals)
    idx_ref[...] = idx_out.astype(jnp.int32)
    gate_ref[...] = e_out / den

    chosen = onehots[0] + onehots[1] + onehots[2] + onehots[3]
    before = _dot(tri_ref[...], chosen.astype(BF)) + run_ref[...]
    rank_out = jnp.zeros(logits.shape, F32)
    for k in range(TOP_K):
        r = jnp.sum(before * onehots[k], axis=1, keepdims=True)
        rank_out = jnp.where(lane == k, r, rank_out)
    rank_ref[...] = rank_out.astype(jnp.int32)
    run_ref[...] = run_ref[...] + jnp.sum(chosen, axis=0, keepdims=True)
    cnt_ref[...] = run_ref[...]


def moe_router(x, w_pad, b_pad, tm):
    m, d = x.shape
    assert m % tm == 0, (m, tm)
    tri = jnp.asarray(np.tril(np.ones((tm, tm), np.float32), -1), BF)
    row = lambda dt: jax.ShapeDtypeStruct((m, LANES), dt)
    return pl.pallas_call(
        _router_kernel,
        grid=(m // tm,),
        in_specs=[pl.BlockSpec((tm, d), lambda i: (i, 0)),
                  pl.BlockSpec((d, LANES), lambda i: (0, 0)),
                  pl.BlockSpec((1, LANES), lambda i: (0, 0)),
                  pl.BlockSpec((tm, tm), lambda i: (0, 0))],
        out_specs=[pl.BlockSpec((tm, LANES), lambda i: (i, 0)),
                   pl.BlockSpec((tm, LANES), lambda i: (i, 0)),
                   pl.BlockSpec((tm, LANES), lambda i: (i, 0)),
                   pl.BlockSpec((1, LANES), lambda i: (0, 0))],
        out_shape=[row(jnp.int32), row(F32), row(jnp.int32), jax.ShapeDtypeStruct((1, LANES), F32)],
        scratch_shapes=[pltpu.VMEM((1, LANES), F32)],
        compiler_params=_cparams(("arbitrary",)),
        name="moe_router",
    )(x, w_pad, b_pad, tri)


def _slot_ids_to_smem(slot_ref, ids_ref, sem):
    cp = pltpu.make_async_copy(slot_ref.at[0], ids_ref, sem)
    cp.start()
    cp.wait()


def _dispatch_kernel(pad0_ref, npad_ref, slot_ref, x_ref, xg_ref, ids_ref, zero_ref, sem_ids, sem, sem_z, *, tn):
    @pl.when(pl.program_id(0) == 0)
    def _():
        zero_ref[...] = jnp.zeros(zero_ref.shape, zero_ref.dtype)

        def per_expert(e, carry):
            zr = zero_ref.shape[0]
            row0 = pad0_ref[e]
            n_one = jnp.minimum((-row0) & (zr - 1), npad_ref[e])
            row_al = row0 + n_one
            n_full = (npad_ref[e] - n_one) // zr

            def start_full(r, c):
                dst = xg_ref.at[pl.ds(pl.multiple_of(row_al + r * zr, zr), zr), :]
                pltpu.make_async_copy(zero_ref, dst, sem_z).start()
                return c

            def wait_full(r, c):
                pltpu.make_async_copy(zero_ref, xg_ref.at[pl.ds(0, zr), :], sem_z).wait()
                return c

            def start_one(r, c):
                pltpu.make_async_copy(zero_ref.at[pl.ds(0, 1), :], xg_ref.at[pl.ds(row0 + r, 1), :], sem_z).start()
                return c

            def wait_one(r, c):
                pltpu.make_async_copy(zero_ref.at[pl.ds(0, 1), :], xg_ref.at[pl.ds(0, 1), :], sem_z).wait()
                return c

            lax.fori_loop(0, n_full, start_full, 0)
            lax.fori_loop(0, n_one, start_one, 0)
            lax.fori_loop(0, n_full, wait_full, 0)
            lax.fori_loop(0, n_one, wait_one, 0)
            return carry

        lax.fori_loop(0, pad0_ref.shape[0], per_expert, 0)

    _slot_ids_to_smem(slot_ref, ids_ref, sem_ids)

    def issue(t, carry):
        for k in range(TOP_K):
            pltpu.make_async_copy(x_ref.at[pl.ds(t, 1), :],
                                  xg_ref.at[pl.ds(ids_ref[0, t * TOP_K + k], 1), :], sem).start(priority=k % 2)
        return carry

    lax.fori_loop(0, tn, issue, 0, unroll=8)
    for k in range(TOP_K):
        pltpu.make_async_copy(x_ref, xg_ref.at[pl.ds(0, tn), :], sem).wait()


def moe_dispatch(x, slots, pad0, npad, cap):
    n, d = x.shape
    tn = MOE_TOK
    assert n % tn == 0, (n, tn)
    assert MOE_ZERO_ROWS & (MOE_ZERO_ROWS - 1) == 0 and MOE_TM % MOE_ZERO_ROWS == 0 and MOE_ZERO_ROWS % 8 == 0
    return pl.pallas_call(
        functools.partial(_dispatch_kernel, tn=tn),
        grid_spec=pltpu.PrefetchScalarGridSpec(
            num_scalar_prefetch=2,
            grid=(n // tn,),
            in_specs=[pl.BlockSpec((1, 1, TOP_K * tn), lambda i, p0, np_: (i, 0, 0)),
                      pl.BlockSpec((tn, d), lambda i, p0, np_: (i, 0))],
            out_specs=pl.BlockSpec(memory_space=pl.ANY),
            scratch_shapes=[pltpu.SMEM((1, TOP_K * tn), jnp.int32),
                            pltpu.VMEM((MOE_ZERO_ROWS, d), x.dtype),
                            pltpu.SemaphoreType.DMA(()),
                            pltpu.SemaphoreType.DMA(()),
                            pltpu.SemaphoreType.DMA(())]),
        out_shape=jax.ShapeDtypeStruct((cap, d), x.dtype),
        compiler_params=_cparams(("arbitrary",)),
        name="moe_dispatch",
    )(pad0, npad, slots.reshape(n // tn, 1, TOP_K * tn), x)


def _moe_up_kernel(be_ref, nu_ref, x_ref, wg_ref, wu_ref, bg_ref, bu_ref, o_ref, wgb_ref, wub_ref):
    i = pl.program_id(1)
    changed = jnp.logical_or(i == 0, be_ref[i] != be_ref[jnp.maximum(i - 1, 0)])

    @pl.when(changed)
    def _():
        wgb_ref[...] = wg_ref[...].astype(BF)
        wub_ref[...] = wu_ref[...].astype(BF)

    @pl.when(i < nu_ref[0])
    def _():
        x = x_ref[...].astype(BF)
        g = jnp.minimum(_dot(x, wgb_ref[...]) + bg_ref[...], SWIGLU_LIMIT)
        u = jnp.clip(_dot(x, wub_ref[...]) + bu_ref[...], -SWIGLU_LIMIT, SWIGLU_LIMIT)
        o_ref[...] = ((u + 1.0) * g * _sigmoid(SWIGLU_ALPHA * g)).astype(o_ref.dtype)

    @pl.when(i >= nu_ref[0])
    def _():
        o_ref[...] = jnp.zeros(o_ref.shape, o_ref.dtype)


def moe_up(xg, blk_expert, n_used, w1, b1):
    cap, d = xg.shape
    n_blk = cap // MOE_TM
    n_j = D_FF // MOE_TN
    return pl.pallas_call(
        _moe_up_kernel,
        grid_spec=pltpu.PrefetchScalarGridSpec(
            num_scalar_prefetch=2,
            grid=(n_j, n_blk),
            in_specs=[pl.BlockSpec((MOE_TM, d), lambda j, i, be, nu: (jnp.minimum(i, nu[0] - 1), 0)),
                      pl.BlockSpec((None, d, MOE_TN), lambda j, i, be, nu: (be[i], 0, j)),
                      pl.BlockSpec((None, d, MOE_TN), lambda j, i, be, nu: (be[i], 0, n_j + j)),
                      pl.BlockSpec((None, 1, MOE_TN), lambda j, i, be, nu: (be[i], 0, j)),
                      pl.BlockSpec((None, 1, MOE_TN), lambda j, i, be, nu: (be[i], 0, n_j + j))],
            out_specs=pl.BlockSpec((MOE_TM, MOE_TN), lambda j, i, be, nu: (i, j)),
            scratch_shapes=[pltpu.VMEM((d, MOE_TN), BF), pltpu.VMEM((d, MOE_TN), BF)]),
        out_shape=jax.ShapeDtypeStruct((cap, D_FF), BF),
        compiler_params=_cparams(("arbitrary", "arbitrary")),
        name="moe_up",
    )(blk_expert, n_used, xg, w1, w1, b1, b1)


def _moe_down_kernel(be_ref, nu_ref, a_ref, w_ref, b_ref, o_ref, wb_ref):
    i = pl.program_id(1)
    changed = jnp.logical_or(i == 0, be_ref[i] != be_ref[jnp.maximum(i - 1, 0)])

    @pl.when(changed)
    def _():
        wb_ref[...] = w_ref[...].astype(BF)

    @pl.when(i < nu_ref[0])
    def _():
        o_ref[...] = _dot(a_ref[...], wb_ref[...]) + b_ref[...]

    @pl.when(i >= nu_ref[0])
    def _():
        o_ref[...] = jnp.zeros(o_ref.shape, o_ref.dtype)


def moe_down(act, blk_expert, n_used, w2, b2):
    cap, f = act.shape
    d = w2.shape[2]
    n_blk = cap // MOE_TM
    return pl.pallas_call(
        _moe_down_kernel,
        grid_spec=pltpu.PrefetchScalarGridSpec(
            num_scalar_prefetch=2,
            grid=(d // MOE_TN_DOWN, n_blk),
            in_specs=[pl.BlockSpec((MOE_TM, f), lambda j, i, be, nu: (jnp.minimum(i, nu[0] - 1), 0)),
                      pl.BlockSpec((None, f, MOE_TN_DOWN), lambda j, i, be, nu: (be[i], 0, j)),
                      pl.BlockSpec((None, 1, MOE_TN_DOWN), lambda j, i, be, nu: (be[i], 0, j))],
            out_specs=pl.BlockSpec((MOE_TM, MOE_TN_DOWN), lambda j, i, be, nu: (i, j)),
            scratch_shapes=[pltpu.VMEM((f, MOE_TN_DOWN), BF)]),
        out_shape=jax.ShapeDtypeStruct((cap, d), F32),
        compiler_params=_cparams(("arbitrary", "arbitrary")),
        name="moe_down",
    )(blk_expert, n_used, act, w2, b2)


def _combine_ln_kernel(slot_ref, yb_hbm, gate_ref, x_ref, g_ref, b_ref, o_ref, ids_ref, buf_ref, sem_ids, sem, *, tn):
    _slot_ids_to_smem(slot_ref, ids_ref, sem_ids)

    def issue(t, carry):
        for k in range(TOP_K):
            pltpu.make_async_copy(yb_hbm.at[pl.ds(ids_ref[0, t * TOP_K + k], 1), :],
                                  buf_ref.at[k, pl.ds(t, 1), :], sem).start(priority=k % 2)
        return carry

    lax.fori_loop(0, tn, issue, 0, unroll=8)
    for k in range(TOP_K):
        pltpu.make_async_copy(yb_hbm.at[pl.ds(0, tn), :], buf_ref.at[k], sem).wait()
    gate = gate_ref[...]
    y = gate[:, 0:1] * buf_ref[0]
    for k in range(1, TOP_K):
        y = y + gate[:, k:k + 1] * buf_ref[k]
    o_ref[...] = _layer_norm(DN_ALPHA * x_ref[...] + y, g_ref[...], b_ref[...])


def moe_combine_ln(yb, slots, gate, x, g, b):
    n, d = x.shape
    tn = MOE_TOK
    assert n % tn == 0, (n, tn)
    return pl.pallas_call(
        functools.partial(_combine_ln_kernel, tn=tn),
        grid=(n // tn,),
        in_specs=[pl.BlockSpec((1, 1, TOP_K * tn), lambda i: (i, 0, 0)),
                  pl.BlockSpec(memory_space=pl.ANY),
                  pl.BlockSpec((tn, LANES), lambda i: (i, 0)),
                  pl.BlockSpec((tn, d), lambda i: (i, 0)),
                  pl.BlockSpec((1, d), lambda i: (0, 0)),
                  pl.BlockSpec((1, d), lambda i: (0, 0))],
        out_specs=pl.BlockSpec((tn, d), lambda i: (i, 0)),
        out_shape=jax.ShapeDtypeStruct((n, d), F32),
        scratch_shapes=[pltpu.SMEM((1, TOP_K * tn), jnp.int32),
                        pltpu.VMEM((TOP_K, tn, d), F32),
                        pltpu.SemaphoreType.DMA(()),
                        pltpu.SemaphoreType.DMA(())],
        compiler_params=_cparams(("arbitrary",)),
        name="moe_combine_ln",
    )(slots.reshape(n // tn, 1, TOP_K * tn), yb, gate, x, g.reshape(1, d), b.reshape(1, d))


def moe_ffn_ln(x, router_w, router_b, w1, b1, w2, b2, e0, ln_g, ln_b):
    n, d = x.shape
    w_pad = jnp.pad(router_w, ((0, 0), (0, LANES - N_EXPERTS)))
    b_pad = jnp.pad(router_b, (0, LANES - N_EXPERTS)).reshape(1, LANES)
    idx, gate, rank, cnt = moe_router(x, w_pad, b_pad, 512)

    counts = cnt[0, :N_EXPERTS].astype(jnp.int32)
    padded = (counts + MOE_TM - 1) // MOE_TM * MOE_TM
    pad_end = jnp.cumsum(padded)
    pad_start = pad_end - padded
    onehot = idx[:, :TOP_K, None] == jnp.arange(N_EXPERTS, dtype=jnp.int32)
    slots = (jnp.sum(jnp.where(onehot, pad_start, 0), axis=-1) + rank[:, :TOP_K]).reshape(-1)
    n_blk = -(-(n * TOP_K + N_EXPERTS * (MOE_TM - 1)) // MOE_TM)
    blk_row0 = jnp.arange(n_blk, dtype=jnp.int32) * MOE_TM
    blk_expert = e0 + jnp.minimum(jnp.sum(pad_end[None, :] <= blk_row0[:, None], axis=1), N_EXPERTS - 1)
    n_used = (pad_end[-1] // MOE_TM).astype(jnp.int32).reshape(1)

    cap = n_blk * MOE_TM
    pad0 = jnp.concatenate([pad_start + counts, pad_end[-1:]]).astype(jnp.int32)
    npad = jnp.concatenate([padded - counts, cap - pad_end[-1:]]).astype(jnp.int32)
    xg = moe_dispatch(x, slots, pad0, npad, cap)
    act = moe_up(xg, blk_expert.astype(jnp.int32), n_used, w1, b1)
    yb = moe_down(act, blk_expert.astype(jnp.int32), n_used, w2, b2)
    return moe_combine_ln(yb, slots, gate, x, ln_g, ln_b)


def _pad_heads(w):
    lead = w.shape[:-1]
    w = w.reshape(lead + (GLA_H, GLA_DK))
    w = jnp.pad(w, [(0, 0)] * len(lead) + [(0, 0), (0, LANES - GLA_DK)])
    return w.reshape(lead + (GLA_H * LANES,))


def _tail_column_map():
    c = np.cumsum((3 * NSA_H, GLA_H * GLA_DK, GLA_H * GLA_DK, GROUP_W, GLA_RANK, GROUP_W))
    dst = np.zeros(c[-1], np.int64)
    src = np.arange(c[-1])
    rel = lambda cb: (cb - CB_MISC) * LANES
    dst[:c[0]] = rel(CB_MISC) + MISC_GATE0 + src[:c[0]]
    for lo, hi, cb in ((c[0], c[1], CB_QD), (c[1], c[2], CB_KD)):
        k = src[lo:hi] - lo
        dst[lo:hi] = rel(cb) + (k // GLA_DK) * LANES + k % GLA_DK
    dst[c[2]:c[3]] = rel(CB_VD) + src[c[2]:c[3]] - c[2]
    dst[c[3]:c[4]] = rel(CB_MISC) + MISC_LR0 + src[c[3]:c[4]] - c[3]
    dst[c[4]:c[5]] = rel(CB_GOD) + src[c[4]:c[5]] - c[4]
    return dst


def _relayout_w_in(w):
    wb = w.astype(BF)
    n_keep = CB_MISC * LANES
    dst = _tail_column_map()
    sel = np.zeros((dst.shape[0], D_PROJ - n_keep), np.float32)
    sel[np.arange(dst.shape[0]), dst] = 1.0
    tail = jnp.dot(wb[:, n_keep:], jnp.asarray(sel, BF), preferred_element_type=F32).astype(BF)
    return jnp.concatenate([wb[:, :n_keep], tail], axis=1)


def kernel(x_prompt, x_sample, cache_pool, cache_nsa_kv, cache_win_kv, state_gla, cache_mem_kv, page_table,
           mem_prompt, ln_in_g, ln_in_b, w_in, w_out, pool_w, pool_scale, gmlp_ln_g, gmlp_ln_b, gmlp_ws, gmlp_bs,
           nsa_phi, gla_wa, gla_ba, gla_norm_g, ln1_g, ln1_b, mem_wq, mem_wkv, mem_wo, ln2_g, ln2_b,
           router_w, router_b, moe_w1, moe_b1, moe_w2, moe_b2, ln3_g, ln3_b):
    n_bp, t_p, d = x_prompt.shape
    n_bs, t_s, _ = x_sample.shape
    n_p = n_bp * t_p
    n_s = n_bs * t_s
    n_pool = cache_nsa_kv.shape[1]
    page = cache_nsa_kv.shape[2]
    past_len = page_table.shape[1] * page
    n_wc = cache_win_kv.shape[2]
    assert t_p % 512 == 0 and n_s % 512 == 0 and t_s == 8 and past_len % SEL_BLOCK == 0 and n_wc == WINDOW

    x = jnp.concatenate([ln_rows(x_prompt.reshape(n_p, d), ln_in_g, ln_in_b, 512),
                         ln_rows(x_sample.reshape(n_s, d), ln_in_g, ln_in_b, 512)], axis=0)
    n_tok = n_p + n_s
    tm_big = 1024 if n_tok % 1024 == 0 else 512
    cpp = page // CMP_BLOCK
    cache_blocks = cache_nsa_kv.reshape(DEPTH * n_pool * cpp, CMP_BLOCK, 4, NSA_DH)
    state0 = state_gla.reshape(DEPTH * n_bs, GLA_H, GLA_DK, GLA_DV)
    w1_all = moe_w1.reshape(DEPTH * N_EXPERTS, d, 2 * D_FF)
    b1_all = moe_b1.reshape(DEPTH * N_EXPERTS, 1, 2 * D_FF)
    w2_all = moe_w2.reshape(DEPTH * N_EXPERTS, D_FF, d)
    b2_all = moe_b2.reshape(DEPTH * N_EXPERTS, 1, d)

    expand_p = (np.arange(t_p)[None, :] // SEL_BLOCK == np.arange(t_p // SEL_BLOCK)[:, None])
    expand_p = jnp.asarray(expand_p, BF)
    expand_s = (np.arange(past_len + LANES)[None, :] // SEL_BLOCK == np.arange(LANES)[:, None])
    expand_s = jnp.asarray(expand_s, BF)
    pair_s = jnp.asarray(np.arange(past_len // CMP_BLOCK)[:, None] // 2 == np.arange(LANES)[None, :], BF)
    bb_s = LANES // t_s
    eye_bb = jnp.eye(bb_s, dtype=F32)

    outs = {k: [] for k in ("pool_p", "nsa_p", "win_p", "gla_p", "mem_p", "pool_s", "chunk_s", "nsa_s", "win_s",
                            "gla_s")}
    for l in range(DEPTH):
        proj = matmul(x, _relayout_w_in(w_in[l]), tm_big, 1024)

        pre_p = jnp.zeros((n_bp, POOL_PRE, GROUP_W), F32)
        pre_s = jnp.pad(cache_pool[l], ((0, 0), (1, 0), (0, 0)))
        ya = jnp.concatenate([
            pool_mixer(proj, 0, n_bp, t_p, pre_p, pool_w[l], pool_scale[l], 0, 1, 512),
            pool_mixer(proj, n_p, n_bs, t_s, pre_s, pool_w[l], pool_scale[l], past_len, bb_s, t_s)], axis=0)

        tril = jnp.tril(jnp.ones((GMLP_CHUNK, GMLP_CHUNK), F32))
        w_mix_p = (gmlp_ws[l] * tril).astype(BF)
        bias_p = jnp.repeat(gmlp_bs[l].T, LANES, axis=1)
        ws_s = gmlp_ws[l][:, :t_s, :t_s] * tril[:t_s, :t_s]
        w_mix_s = jnp.einsum('ab,gts->gatbs', eye_bb, ws_s).reshape(4, LANES, LANES).astype(BF)
        bias_s = jnp.tile(jnp.repeat(gmlp_bs[l][:, :t_s].T, LANES, axis=1), (bb_s, 1))
        yb_p, _ = gmlp_mixer(proj, 0, n_p, gmlp_ln_g[l], gmlp_ln_b[l], w_mix_p, bias_p, 4)
        yb_s, v_s = gmlp_mixer(proj, n_p, n_s, gmlp_ln_g[l], gmlp_ln_b[l], w_mix_s, bias_s, 4)
        yb = jnp.concatenate([yb_p, yb_s], axis=0)

        phi2 = nsa_phi[l].astype(BF).reshape(2, CMP_BLOCK // 2, 2 * NSA_DH, NSA_DH)
        kc_p, vc_p = cmp_project(proj, n_p // CMP_BLOCK, phi2[0], phi2[1], t_p // CMP_BLOCK)
        yc_p = nsa_prompt(proj, n_bp, t_p, kc_p, vc_p, expand_p)
        kvc_pool = cmp_pool(cache_blocks, l * n_pool * cpp, n_pool * cpp, phi2[0], phi2[1], 128)
        yc_s = nsa_sample(proj, n_p, n_bs, t_s, cache_nsa_kv, l, page_table,
                          kvc_pool.reshape(n_pool, cpp, 2 * NSA_DH), cache_win_kv, pair_s, expand_s)
        yc = jnp.concatenate([yc_p, yc_s], axis=0)

        wa_pad = jnp.zeros((LANES, GLA_H * LANES), F32).at[MISC_LR0:MISC_LR0 + GLA_RANK].set(_pad_heads(gla_wa[l]))
        wa_pad = wa_pad.astype(BF)
        ba_pad = _pad_heads(gla_ba[l]).reshape(1, GLA_H * LANES)
        yd_p, st_p = gla_prompt(proj, n_bp, t_p, wa_pad, ba_pad, gla_norm_g[l])
        yd_s, st_s = gla_sample(proj, n_p, n_bs, t_s, wa_pad, ba_pad, gla_norm_g[l], state0, l * n_bs, bb_s)
        yd = jnp.concatenate([yd_p, yd_s], axis=0)

        x = matmul_res_ln([ya, yb, yc, yd], w_out[l].astype(BF), x, ln1_g[l], ln1_b[l], 256)

        qm = matmul(x, mem_wq[l].astype(BF), tm_big, 512)
        mem_kv_p = matmul(mem_prompt.reshape(n_bp * MEM_LEN, d), mem_wkv[l].astype(BF), 512, 512)
        mem_kv_p = mem_kv_p.reshape(1, n_bp, MEM_LEN, 2, MEM_H, MEM_DH)
        om = jnp.concatenate([
            mem_attention(qm, 0, n_bp, t_p, mem_kv_p, 0, 512),
            mem_attention(qm, n_p, n_bs, t_s, cache_mem_kv, l, t_s)], axis=0)
        x = matmul_res_ln([om], mem_wo[l].astype(BF), x, ln2_g[l], ln2_b[l], 256)

        x = moe_ffn_ln(x, router_w[l], router_b[l], w1_all, b1_all, w2_all, b2_all, l * N_EXPERTS,
                       ln3_g[l], ln3_b[l])

        seqs_p = lambda c0, c1: proj[:n_p, c0 * LANES:c1 * LANES].reshape(n_bp, t_p, (c1 - c0) * LANES)
        seqs_s = lambda c0, c1: proj[n_p:, c0 * LANES:c1 * LANES].reshape(n_bs, t_s, (c1 - c0) * LANES)
        outs["pool_p"].append(seqs_p(CB_XA, CB_U)[:, t_p - POOL_BUF:])
        outs["nsa_p"].append(seqs_p(CB_CMPK, CB_WINK).reshape(n_bp, t_p, 4, NSA_DH))
        outs["win_p"].append(seqs_p(CB_WINK, CB_MISC)[:, t_p - min(WINDOW, t_p):]
                             .reshape(n_bp, min(WINDOW, t_p), 2, NSA_DH))
        outs["gla_p"].append(st_p)
        outs["mem_p"].append(mem_kv_p[0])
        outs["pool_s"].append(seqs_s(CB_XA, CB_U))
        outs["chunk_s"].append(v_s.reshape(n_bs, t_s, GROUP_W))
        outs["nsa_s"].append(seqs_s(CB_CMPK, CB_WINK).reshape(n_bs, t_s, 4, NSA_DH))
        outs["win_s"].append(seqs_s(CB_WINK, CB_MISC).reshape(n_bs, t_s, 2, NSA_DH))
        outs["gla_s"].append(st_s)

    st = lambda k: jnp.stack(outs[k])
    return (x[:n_p].reshape(n_bp, t_p, d), x[n_p:].reshape(n_bs, t_s, d), st("pool_p"), st("nsa_p"), st("win_p"),
            st("gla_p"), st("mem_p"), st("pool_s"), st("chunk_s"), st("nsa_s"), st("win_s"), st("gla_s"))
```
